```python
import math
import jax, jax.numpy as jnp
from jax import lax
import numpy as np


D_MODEL = 1024
BATCH = 2
SEQ = 8192
DEPTH = 4
DEC_BATCH = 128
DEC_SEQ = 4
PAST_LEN = 2048
PAGE_SIZE = 128

N_A_LAYERS = DEPTH // 2
N_B_LAYERS = DEPTH - N_A_LAYERS
D_RNN = D_MODEL
N_RG_BLOCKS = 8
RG_BLOCK = D_RNN // N_RG_BLOCKS
RG_CONV_W = 4
RG_C = 8.0
HEAD_DIM = 64
N_HEADS = D_MODEL // (2 * HEAD_DIM)
D_K = N_HEADS * 2 * HEAD_DIM
D_V = N_HEADS * 2 * HEAD_DIM
ATTN_SCALE = HEAD_DIM ** -0.5
D_FF = 3 * D_MODEL
FFN_CONV_W = 3
N_BUCKETS = 32
MAX_DISTANCE = 128
Q_BLOCK = 128
EPS = 1e-6
NEG_INF = -1e30

kernel_name = 'yoco_rglru_diffattn_step'


def rmsnorm(x, g):
    xf = x.astype(jnp.float32)
    xf = xf * lax.rsqrt(jnp.mean(xf * xf, axis=-1, keepdims=True) + EPS)
    return (xf * g.astype(jnp.float32)).astype(x.dtype)


def causal_dwconv(x_full, w, b):
    width = w.shape[0]
    t_len = x_full.shape[1] - width + 1
    out = b
    for j in range(width):
        out = out + x_full[:, j:j + t_len] * w[j]
    return out


def block_diag(x, w, b):
    bsz, t_len, _ = x.shape
    xb = x.reshape(bsz, t_len, N_RG_BLOCKS, RG_BLOCK)
    return jnp.einsum('btnc,ncd->btnd', xb, w).reshape(bsz, t_len, D_RNN) + b


def rglru_mixer(x, h0, conv_buf, positions, w_in, conv_w, conv_b, w_a, b_a, w_x, b_x, lam, w_out):
    u = x @ w_in
    gate_branch, rec = u[..., :D_RNN], u[..., D_RNN:]
    x_full = jnp.concatenate([conv_buf.astype(rec.dtype), rec], axis=1)
    xc = causal_dwconv(x_full, conv_w, conv_b)
    new_buf = x_full[:, -(RG_CONV_W - 1):]
    r = jax.nn.sigmoid(block_diag(xc, w_a, b_a).astype(jnp.float32))
    i = jax.nn.sigmoid(block_diag(xc, w_x, b_x).astype(jnp.float32))
    log_a = -RG_C * r * jax.nn.softplus(-lam.astype(jnp.float32))
    reset = (positions == 0)[None, :, None]
    a = jnp.where(reset, 0.0, jnp.exp(log_a))
    mult = jnp.where(reset, 1.0, jnp.sqrt(-jnp.expm1(2.0 * log_a)))
    bterm = mult * i * xc.astype(jnp.float32)
    bterm = bterm.at[:, 0].add(a[:, 0] * h0.astype(jnp.float32))

    def combine(left, right):
        return (left[0] * right[0], right[0] * left[1] + right[1])

    _, h = lax.associative_scan(combine, (a, bterm), axis=1)
    y = (h.astype(x.dtype) * jax.nn.gelu(gate_branch)) @ w_out
    return y, h[:, -1].astype(x.dtype), new_buf


def conv_ffn(x, buf, w_up, conv_w, conv_b, w_down):
    u = x @ w_up
    u_full = jnp.concatenate([buf.astype(u.dtype), u], axis=1)
    uc = causal_dwconv(u_full, conv_w, conv_b)
    new_buf = u_full[:, -(FFN_CONV_W - 1):]
    val, gate = uc[..., :D_FF], uc[..., D_FF:]
    return (jax.nn.gelu(gate) * val) @ w_down, new_buf


def rel_bucket(rel):
    n = jnp.maximum(rel, 0)
    max_exact = N_BUCKETS // 2
    nf = jnp.maximum(n, 1).astype(jnp.float32)
    large = max_exact + (jnp.log(nf / max_exact) / math.log(MAX_DISTANCE / max_exact)
                         * (N_BUCKETS - max_exact)).astype(jnp.int32)
    large = jnp.minimum(large, N_BUCKETS - 1)
    return jnp.where(n < max_exact, n, large)


def diff_attn_core(q, q_pos, segments, rel_bias, lam):
    scores = []
    for k, v, k_pos in segments:
        s = jnp.einsum('bqhcd,bkhcd->bhcqk', q, k).astype(jnp.float32) * ATTN_SCALE
        rel = q_pos[:, None] - k_pos[None, :]
        bias = jnp.transpose(jnp.take(rel_bias, rel_bucket(rel), axis=0), (2, 0, 1)).astype(jnp.float32)
        s = jnp.where((rel >= 0)[None, None, None], s + bias[None, :, None], NEG_INF)
        scores.append(s)
    p = jax.nn.softmax(jnp.concatenate(scores, axis=-1), axis=-1)
    w = p[:, :, 0] - lam * p[:, :, 1]
    out = 0.0
    off = 0
    for k, v, k_pos in segments:
        n = k.shape[1]
        out = out + jnp.einsum('bhqk,bkhe->bqhe', w[..., off:off + n].astype(v.dtype), v)
        off += n
    return out


def diff_attn_layer(x, positions, segments, w_q, lq1, lk1, lq2, lk2, subln, w_o, rel_bias, layer_idx):
    bsz, t_len, _ = x.shape
    q = (x @ w_q).reshape(bsz, t_len, N_HEADS, 2, HEAD_DIM)
    lam_init = 0.8 - 0.6 * math.exp(-0.3 * layer_idx)
    lam = (jnp.exp(jnp.sum(lq1.astype(jnp.float32) * lk1.astype(jnp.float32)))
           - jnp.exp(jnp.sum(lq2.astype(jnp.float32) * lk2.astype(jnp.float32))) + lam_init)
    if t_len % Q_BLOCK == 0:
        nb = t_len // Q_BLOCK
        qb = jnp.moveaxis(q.reshape(bsz, nb, Q_BLOCK, N_HEADS, 2, HEAD_DIM), 1, 0)
        pb = positions.reshape(nb, Q_BLOCK)
        o = lax.map(lambda qp: diff_attn_core(qp[0], qp[1], segments, rel_bias, lam), (qb, pb))
        o = jnp.moveaxis(o, 0, 1).reshape(bsz, t_len, N_HEADS, 2 * HEAD_DIM)
    else:
        o = diff_attn_core(q, positions, segments, rel_bias, lam)
    o = rmsnorm(o, subln) * (1.0 - lam_init)
    return o.reshape(bsz, t_len, D_V) @ w_o


def shared_kv(x, kv_norm, w_kv):
    bsz, t_len, _ = x.shape
    kv = rmsnorm(x, kv_norm) @ w_kv
    k = kv[..., :D_K].reshape(bsz, t_len, N_HEADS, 2, HEAD_DIM)
    v = kv[..., D_K:].reshape(bsz, t_len, N_HEADS, 2 * HEAD_DIM)
    return k, v


def run_trunk(x, pos0, rg_h0, rg_buf0, ffn_buf0, past, p):
    t_len = x.shape[1]
    positions = pos0 + jnp.arange(t_len, dtype=jnp.int32)
    new_h, new_rbuf, new_fbuf = [], [], []
    segments = None
    k_new = None
    v_new = None
    for l in range(DEPTH):
        xn = rmsnorm(x, p['norm_mix'][l])
        if l < N_A_LAYERS:
            y, h_last, rbuf = rglru_mixer(xn, rg_h0[l], rg_buf0[l], positions,
                                          p['rg_w_in'][l], p['rg_conv_w'][l], p['rg_conv_b'][l],
                                          p['rg_w_a'][l], p['rg_b_a'][l], p['rg_w_x'][l], p['rg_b_x'][l],
                                          p['rg_lambda'][l], p['rg_w_out'][l])
            new_h.append(h_last)
            new_rbuf.append(rbuf)
        else:
            if l == N_A_LAYERS:
                k_new, v_new = shared_kv(x, p['kv_norm'], p['w_kv'])
                segments = [(k_new, v_new, positions)]
                if past is not None:
                    pk, pv = past
                    segments = [(pk, pv, jnp.arange(pk.shape[1], dtype=jnp.int32))] + segments
            j = l - N_A_LAYERS
            y = diff_attn_layer(xn, positions, segments, p['attn_w_q'][j], p['lam_q1'][j], p['lam_k1'][j],
                                p['lam_q2'][j], p['lam_k2'][j], p['attn_subln'][j], p['attn_w_o'][j],
                                p['rel_bias'], l)
        x = x + y
        y, fbuf = conv_ffn(rmsnorm(x, p['norm_ffn'][l]), ffn_buf0[l], p['ffn_w_up'][l],
                           p['ffn_conv_w'][l], p['ffn_conv_b'][l], p['ffn_w_down'][l])
        new_fbuf.append(fbuf)
        x = x + y
    return (rmsnorm(x, p['final_norm']), jnp.stack(new_h), jnp.stack(new_rbuf),
            jnp.stack(new_fbuf), k_new, v_new)


def setup_inputs(seed: int = 0) -> dict:
    key = jax.random.key(seed)
    ks = iter(jax.random.split(key, 48))
    f32 = jnp.float32

    def nrm(shape, scale):
        return jax.random.normal(next(ks), shape, f32) * scale

    n_pages = PAST_LEN // PAGE_SIZE
    used = DEC_BATCH * n_pages
    n_pool = used + max(1, used // 4)
    page_table = jax.random.permutation(next(ks), n_pool)[:used].reshape(DEC_BATCH, n_pages).astype(jnp.int32)
    u_lam = jax.random.uniform(next(ks), (N_A_LAYERS, D_RNN), f32, 0.9, 0.999)
    return {
        'x_prompt': nrm((BATCH, SEQ, D_MODEL), 1.0),
        'x_sample': nrm((DEC_BATCH, DEC_SEQ, D_MODEL), 1.0),
        'state_rglru_h': nrm((N_A_LAYERS, DEC_BATCH, D_RNN), 0.5),
        'state_rglru_conv': nrm((N_A_LAYERS, DEC_BATCH, RG_CONV_W - 1, D_RNN), 0.5),
        'state_ffn_conv': nrm((DEPTH, DEC_BATCH, FFN_CONV_W - 1, 2 * D_FF), 0.5),
        'cache_k': nrm((n_pool, PAGE_SIZE, N_HEADS, 2, HEAD_DIM), 1.0),
        'cache_v': nrm((n_pool, PAGE_SIZE, N_HEADS, 2 * HEAD_DIM), 1.0),
        'page_table': page_table,
        'rel_bias': nrm((N_BUCKETS, N_HEADS), 0.3),
        'norm_mix': 1.0 + nrm((DEPTH, D_MODEL), 0.02),
        'norm_ffn': 1.0 + nrm((DEPTH, D_MODEL), 0.02),
        'final_norm': 1.0 + nrm((D_MODEL,), 0.02),
        'rg_w_in': nrm((N_A_LAYERS, D_MODEL, 2 * D_RNN), D_MODEL ** -0.5),
        'rg_conv_w': nrm((N_A_LAYERS, RG_CONV_W, D_RNN), RG_CONV_W ** -0.5),
        'rg_conv_b': nrm((N_A_LAYERS, D_RNN), 0.02),
        'rg_w_a': nrm((N_A_LAYERS, N_RG_BLOCKS, RG_BLOCK, RG_BLOCK), RG_BLOCK ** -0.5),
        'rg_b_a': nrm((N_A_LAYERS, D_RNN), 0.1),
        'rg_w_x': nrm((N_A_LAYERS, N_RG_BLOCKS, RG_BLOCK, RG_BLOCK), RG_BLOCK ** -0.5),
        'rg_b_x': nrm((N_A_LAYERS, D_RNN), 0.1),
        'rg_lambda': jnp.log(u_lam / (1.0 - u_lam)),
        'rg_w_out': nrm((N_A_LAYERS, D_RNN, D_MODEL), D_RNN ** -0.5),
        'kv_norm': 1.0 + nrm((D_MODEL,), 0.02),
        'w_kv': nrm((D_MODEL, D_K + D_V), D_MODEL ** -0.5),
        'attn_w_q': nrm((N_B_LAYERS, D_MODEL, D_K), D_MODEL ** -0.5),
        'lam_q1': nrm((N_B_LAYERS, HEAD_DIM), 0.1),
        'lam_k1': nrm((N_B_LAYERS, HEAD_DIM), 0.1),
        'lam_q2': nrm((N_B_LAYERS, HEAD_DIM), 0.1),
        'lam_k2': nrm((N_B_LAYERS, HEAD_DIM), 0.1),
        'attn_subln': 1.0 + nrm((N_B_LAYERS, 2 * HEAD_DIM), 0.02),
        'attn_w_o': nrm((N_B_LAYERS, D_V, D_MODEL), D_V ** -0.5),
        'ffn_w_up': nrm((DEPTH, D_MODEL, 2 * D_FF), D_MODEL ** -0.5),
        'ffn_conv_w': nrm((DEPTH, FFN_CONV_W, 2 * D_FF), FFN_CONV_W ** -0.5),
        'ffn_conv_b': nrm((DEPTH, 2 * D_FF), 0.02),
        'ffn_w_down': nrm((DEPTH, D_FF, D_MODEL), D_FF ** -0.5),
    }


def reference(x_prompt, x_sample, state_rglru_h, state_rglru_conv, state_ffn_conv, cache_k, cache_v,
              page_table, rel_bias, norm_mix, norm_ffn, final_norm, rg_w_in, rg_conv_w, rg_conv_b,
              rg_w_a, rg_b_a, rg_w_x, rg_b_x, rg_lambda, rg_w_out, kv_norm, w_kv, attn_w_q,
              lam_q1, lam_k1, lam_q2, lam_k2, attn_subln, attn_w_o, ffn_w_up, ffn_conv_w,
              ffn_conv_b, ffn_w_down):
    p = {'rel_bias': rel_bias, 'norm_mix': norm_mix, 'norm_ffn': norm_ffn, 'final_norm': final_norm,
         'rg_w_in': rg_w_in, 'rg_conv_w': rg_conv_w, 'rg_conv_b': rg_conv_b, 'rg_w_a': rg_w_a,
         'rg_b_a': rg_b_a, 'rg_w_x': rg_w_x, 'rg_b_x': rg_b_x, 'rg_lambda': rg_lambda,
         'rg_w_out': rg_w_out, 'kv_norm': kv_norm, 'w_kv': w_kv, 'attn_w_q': attn_w_q,
         'lam_q1': lam_q1, 'lam_k1': lam_k1, 'lam_q2': lam_q2, 'lam_k2': lam_k2,
         'attn_subln': attn_subln, 'attn_w_o': attn_w_o, 'ffn_w_up': ffn_w_up,
         'ffn_conv_w': ffn_conv_w, 'ffn_conv_b': ffn_conv_b, 'ffn_w_down': ffn_w_down}
    dt = x_prompt.dtype
    bp = x_prompt.shape[0]
    yp, hp, rbp, fbp, kp, vp = run_trunk(
        x_prompt, 0,
        jnp.zeros((N_A_LAYERS, bp, D_RNN), dt),
        jnp.zeros((N_A_LAYERS, bp, RG_CONV_W - 1, D_RNN), dt),
        jnp.zeros((DEPTH, bp, FFN_CONV_W - 1, 2 * D_FF), dt),
        None, p)
    bs = x_sample.shape[0]
    past_len = page_table.shape[1] * cache_k.shape[1]
    past_k = jnp.take(cache_k, page_table, axis=0).reshape(bs, past_len, N_HEADS, 2, HEAD_DIM)
    past_v = jnp.take(cache_v, page_table, axis=0).reshape(bs, past_len, N_HEADS, 2 * HEAD_DIM)
    ys, hs, rbs, fbs, ks, vs = run_trunk(x_sample, past_len, state_rglru_h, state_rglru_conv,
                                         state_ffn_conv, (past_k, past_v), p)
    return (yp, ys, hp, rbp, fbp, kp, vp, hs, rbs, fbs, ks, vs)
```

```python
import functools
import math

import jax
import jax.numpy as jnp
from jax import lax
from jax.experimental import pallas as pl
from jax.experimental.pallas import tpu as pltpu

EPS = 1e-6
RG_C = 8.0
HEAD_DIM = 64
N_BUCKETS = 32
MAX_DISTANCE = 128
NEG_INF = -1e30
ATTN_SCALE = HEAD_DIM ** -0.5

V7X_LANES = 128
V7X_SUBLANES = 8
V7X_VMEM_BYTES = 64 * 2 ** 20
VMEM_LIMIT = V7X_VMEM_BYTES - 8 * 2 ** 20

BF16 = jnp.bfloat16
F32 = jnp.float32


def _bucket_starts():
    max_exact = N_BUCKETS // 2
    starts = list(range(max_exact + 1))
    for b in range(max_exact + 1, N_BUCKETS):
        n = starts[-1]
        while True:
            n += 1
            large = max_exact + int(math.log(n / max_exact) / math.log(MAX_DISTANCE / max_exact)
                                    * (N_BUCKETS - max_exact))
            if min(large, N_BUCKETS - 1) >= b:
                break
        starts.append(n)
    return tuple(starts)


BUCKET_STARTS = _bucket_starts()
FAR_DISTANCE = BUCKET_STARTS[-1]


def _cparams(sem, vmem=VMEM_LIMIT):
    return pltpu.CompilerParams(dimension_semantics=sem, vmem_limit_bytes=vmem)


def _rms(x, g):
    return x * lax.rsqrt(jnp.mean(x * x, axis=-1, keepdims=True) + EPS) * g


def _dot(a, b):
    return jnp.dot(a, b, preferred_element_type=F32)


def _dot_nt(a, b):
    return lax.dot_general(a, b, (((1,), (1,)), ((), ())), preferred_element_type=F32)


def _norm_mm_kernel(x_ref, g_ref, w_ref, *out_refs, splits):
    xn = _rms(x_ref[...], g_ref[...]).astype(BF16)
    y = _dot(xn, w_ref[...])
    for o_ref, (lo, hi, scale) in zip(out_refs, splits):
        part = y[:, lo:hi]
        if scale != 1.0:
            part = part * scale
        o_ref[...] = part.astype(o_ref.dtype)


def norm_matmul(x, g, w, splits, dtypes, tm):
    r, d = x.shape
    n = w.shape[1]
    out_shape = [jax.ShapeDtypeStruct((r, hi - lo), dt) for (lo, hi, _), dt in zip(splits, dtypes)]
    out_specs = [pl.BlockSpec((tm, hi - lo), lambda i: (i, 0)) for (lo, hi, _) in splits]
    return pl.pallas_call(
        functools.partial(_norm_mm_kernel, splits=tuple(splits)),
        grid=(r // tm,),
        in_specs=[pl.BlockSpec((tm, d), lambda i: (i, 0)),
                  pl.BlockSpec((1, d), lambda i: (0, 0)),
                  pl.BlockSpec((d, n), lambda i: (0, 0))],
        out_specs=out_specs,
        out_shape=out_shape,
        name="norm_matmul",
        compiler_params=_cparams(("parallel",)),
    )(x, g, w)


def _res_mm_kernel(x_ref, a_ref, w_ref, o_ref):
    o_ref[...] = x_ref[...] + _dot(a_ref[...].astype(BF16), w_ref[...])


def residual_matmul(x, a, w, tm):
    r, d = x.shape
    k = a.shape[1]
    return pl.pallas_call(
        _res_mm_kernel,
        grid=(r // tm,),
        in_specs=[pl.BlockSpec((tm, d), lambda i: (i, 0)),
                  pl.BlockSpec((tm, k), lambda i: (i, 0)),
                  pl.BlockSpec((k, d), lambda i: (0, 0))],
        out_specs=pl.BlockSpec((tm, d), lambda i: (i, 0)),
        out_shape=jax.ShapeDtypeStruct((r, d), F32),
        name="residual_matmul",
        compiler_params=_cparams(("parallel",)),
    )(x, a, w)


def _rg_kernel(x_ref, h0_ref, cb0_ref, g_ref, win_ref, cw_ref, cbias_ref, wax_ref, ba_ref, bx_ref,
               lam_ref, wout_ref, out_ref, hlast_ref, cbout_ref,
               rbuf, a_scr, b_scr, h_scr, *, tm, ts, pos0, conv_w, n_blocks):
    ti = pl.program_id(1)
    d = a_scr.shape[1]
    hdr = rbuf.shape[0] - tm
    nprev = (conv_w - 1) * ts

    @pl.when(ti == 0)
    def _():
        rbuf[hdr - nprev:hdr, :] = cb0_ref[0]
        h_scr[...] = h0_ref[0]

    x = x_ref[0]
    xn = _rms(x, g_ref[...]).astype(BF16)
    u = _dot(xn, win_ref[...])
    gate = u[:, :d]
    rbuf[hdr:hdr + tm, :] = u[:, d:]
    xc = cbias_ref[...] + rbuf[hdr:hdr + tm, :] * cw_ref[conv_w - 1:conv_w, :]
    for j in range(conv_w - 1):
        off = hdr - (conv_w - 1 - j) * ts
        xc = xc + rbuf[off:off + tm, :] * cw_ref[j:j + 1, :]
    new_prev = rbuf[hdr + tm - nprev:hdr + tm, :]
    cbout_ref[0] = new_prev
    rbuf[hdr - nprev:hdr, :] = new_prev

    xcb = xc.astype(BF16)
    blk = d // n_blocks
    ras, rxs = [], []
    for n in range(n_blocks):
        rr = _dot(xcb[:, n * blk:(n + 1) * blk], wax_ref[n])
        ras.append(rr[:, :blk])
        rxs.append(rr[:, blk:])
    r = jax.nn.sigmoid(jnp.concatenate(ras, axis=1) + ba_ref[...])
    i = jax.nn.sigmoid(jnp.concatenate(rxs, axis=1) + bx_ref[...])
    z = -lam_ref[...]
    softplus = jnp.maximum(z, 0.0) + jnp.log(1.0 + jnp.exp(-jnp.abs(z)))
    log_a = (-RG_C) * r * softplus
    a = jnp.exp(log_a)
    mult = jnp.sqrt(1.0 - a * a)
    if pos0 == 0:
        row = lax.broadcasted_iota(jnp.int32, (tm, 1), 0) + ti * tm
        reset = row < ts
        a = jnp.where(reset, 0.0, a)
        mult = jnp.where(reset, 1.0, mult)
    a_scr[...] = a
    b_scr[...] = mult * i * xc

    if ts == 1:
        def group(gi, h):
            base = pl.multiple_of(gi * V7X_SUBLANES, V7X_SUBLANES)
            for rr_ in range(V7X_SUBLANES):
                h = a_scr[pl.ds(base + rr_, 1), :] * h + b_scr[pl.ds(base + rr_, 1), :]
                b_scr[pl.ds(base + rr_, 1), :] = h
            return h
        h = lax.fori_loop(0, tm // V7X_SUBLANES, group, h_scr[...])
    else:
        h = h_scr[...]
        for t in range(tm // ts):
            h = a_scr[t * ts:(t + 1) * ts, :] * h + b_scr[t * ts:(t + 1) * ts, :]
            b_scr[t * ts:(t + 1) * ts, :] = h
    h_scr[...] = h
    hlast_ref[0] = h

    hg = (b_scr[...] * jax.nn.gelu(gate)).astype(BF16)
    out_ref[0] = x + _dot(hg, wout_ref[...])


def rglru_layer(x, h0, cb0, g, w_in, conv_w, conv_b, w_ax, b_a, b_x, lam, w_out, *, tm, ts, pos0):
    bk, r, d = x.shape
    cw = conv_w.shape[0]
    nprev = (cw - 1) * ts
    hdr = max(V7X_SUBLANES, nprev)
    n_blocks = w_ax.shape[0]
    const2 = lambda b, t: (0, 0)
    kern = functools.partial(_rg_kernel, tm=tm, ts=ts, pos0=pos0, conv_w=cw, n_blocks=n_blocks)
    return pl.pallas_call(
        kern,
        grid=(bk, r // tm),
        in_specs=[pl.BlockSpec((1, tm, d), lambda b, t: (b, t, 0)),
                  pl.BlockSpec((1, ts, d), lambda b, t: (b, 0, 0)),
                  pl.BlockSpec((1, nprev, d), lambda b, t: (b, 0, 0)),
                  pl.BlockSpec((1, d), const2),
                  pl.BlockSpec((d, 2 * d), const2),
                  pl.BlockSpec((cw, d), const2),
                  pl.BlockSpec((1, d), const2),
                  pl.BlockSpec(w_ax.shape, lambda b, t: (0, 0, 0)),
                  pl.BlockSpec((1, d), const2),
                  pl.BlockSpec((1, d), const2),
                  pl.BlockSpec((1, d), const2),
                  pl.BlockSpec((d, d), const2)],
        out_specs=[pl.BlockSpec((1, tm, d), lambda b, t: (b, t, 0)),
                   pl.BlockSpec((1, ts, d), lambda b, t: (b, 0, 0)),
                   pl.BlockSpec((1, nprev, d), lambda b, t: (b, 0, 0))],
        out_shape=[jax.ShapeDtypeStruct((bk, r, d), F32),
                   jax.ShapeDtypeStruct((bk, ts, d), F32),
                   jax.ShapeDtypeStruct((bk, nprev, d), F32)],
        scratch_shapes=[pltpu.VMEM((hdr + tm, d), F32),
                        pltpu.VMEM((tm, d), F32),
                        pltpu.VMEM((tm, d), F32),
                        pltpu.VMEM((ts, d), F32)],
        name="rglru_layer",
        compiler_params=_cparams(("parallel", "arbitrary")),
    )(x, h0, cb0, g, w_in, conv_w, conv_b, w_ax, b_a, b_x, lam, w_out)


def _ffn_kernel(x_ref, fbv0_ref, fbg0_ref, g_ref, wv_ref, wg_ref, cwv_ref, cwg_ref, cbv_ref, cbg_ref,
                wd_ref, gfin_ref, out_ref, fbv_ref, fbg_ref,
                xn_scr, acc_scr, ubv, ubg, carry_v, carry_g, *, tm, ts, conv_w, final_norm):
    ti = pl.program_id(1)
    j = pl.program_id(2)
    nj = pl.num_programs(2)
    hdr = ubv.shape[0] - tm
    nprev = (conv_w - 1) * ts

    @pl.when(j == 0)
    def _():
        xn_scr[...] = _rms(x_ref[0], g_ref[...]).astype(BF16)
        acc_scr[...] = jnp.zeros_like(acc_scr)

    xn = xn_scr[...]

    def conv(w_ref, ub, carry, fb0_ref, cw_ref, cb_ref, fb_ref):
        @pl.when(ti == 0)
        def _():
            ub[hdr - nprev:hdr, :] = fb0_ref[0]

        @pl.when(ti > 0)
        def _():
            ub[hdr - nprev:hdr, :] = carry[j]

        ub[hdr:hdr + tm, :] = _dot(xn, w_ref[...])
        uc = cb_ref[...] + ub[hdr:hdr + tm, :] * cw_ref[conv_w - 1:conv_w, :]
        for jj in range(conv_w - 1):
            off = hdr - (conv_w - 1 - jj) * ts
            uc = uc + ub[off:off + tm, :] * cw_ref[jj:jj + 1, :]
        new_prev = ub[hdr + tm - nprev:hdr + tm, :]
        carry[j] = new_prev
        fb_ref[0, 0] = new_prev
        return uc

    val = conv(wv_ref, ubv, carry_v, fbv0_ref, cwv_ref, cbv_ref, fbv_ref)
    gte = conv(wg_ref, ubg, carry_g, fbg0_ref, cwg_ref, cbg_ref, fbg_ref)
    hmid = (jax.nn.gelu(gte) * val).astype(BF16)
    acc_scr[...] += _dot(hmid, wd_ref[...])

    @pl.when(j == nj - 1)
    def _():
        y = x_ref[0] + acc_scr[...]
        if final_norm:
            y = _rms(y, gfin_ref[...])
        out_ref[0] = y


def ffn_layer(x, fb0, g, w_up, conv_w, conv_b, w_down, g_final, *, tm, tn, ts, final_norm):
    bk, r, d = x.shape
    f = w_down.shape[0]
    cw = conv_w.shape[0]
    nprev = (cw - 1) * ts
    hdr = max(V7X_SUBLANES, nprev)
    nc = f // tn
    const2 = lambda b, t, j: (0, 0)
    kern = functools.partial(_ffn_kernel, tm=tm, ts=ts, conv_w=cw, final_norm=final_norm)
    out, fbv, fbg = pl.pallas_call(
        kern,
        grid=(bk, r // tm, nc),
        in_specs=[pl.BlockSpec((1, tm, d), lambda b, t, j: (b, t, 0)),
                  pl.BlockSpec((1, nprev, tn), lambda b, t, j: (b, 0, j)),
                  pl.BlockSpec((1, nprev, tn), lambda b, t, j: (b, 0, nc + j)),
                  pl.BlockSpec((1, d), const2),
                  pl.BlockSpec((d, tn), lambda b, t, j: (0, j)),
                  pl.BlockSpec((d, tn), lambda b, t, j: (0, nc + j)),
                  pl.BlockSpec((cw, tn), lambda b, t, j: (0, j)),
                  pl.BlockSpec((cw, tn), lambda b, t, j: (0, nc + j)),
                  pl.BlockSpec((1, tn), lambda b, t, j: (0, j)),
                  pl.BlockSpec((1, tn), lambda b, t, j: (0, nc + j)),
                  pl.BlockSpec((tn, d), lambda b, t, j: (j, 0)),
                  pl.BlockSpec((1, d), const2)],
        out_specs=[pl.BlockSpec((1, tm, d), lambda b, t, j: (b, t, 0)),
                   pl.BlockSpec((1, 1, nprev, tn), lambda b, t, j: (b, t, 0, j)),
                   pl.BlockSpec((1, 1, nprev, tn), lambda b, t, j: (b, t, 0, j))],
        out_shape=[jax.ShapeDtypeStruct((bk, r, d), F32),
                   jax.ShapeDtypeStruct((bk, r // tm, nprev, f), F32),
                   jax.ShapeDtypeStruct((bk, r // tm, nprev, f), F32)],
        scratch_shapes=[pltpu.VMEM((tm, d), BF16),
                        pltpu.VMEM((tm, d), F32),
                        pltpu.VMEM((hdr + tm, tn), F32),
                        pltpu.VMEM((hdr + tm, tn), F32),
                        pltpu.VMEM((nc, nprev, tn), F32),
                        pltpu.VMEM((nc, nprev, tn), F32)],
        name="ffn_layer",
        compiler_params=_cparams(("parallel", "arbitrary", "arbitrary")),
    )(x, fb0, fb0, g, w_up, w_up, conv_w, conv_w, conv_b, conv_b, w_down, g_final)
    return out, jnp.concatenate([fbv[:, -1], fbg[:, -1]], axis=-1)


def _shifted_bias(rel, rb_ref, h):
    last = rb_ref[N_BUCKETS - 1, h]
    val = jnp.zeros(rel.shape, F32)
    for b in range(N_BUCKETS - 2, -1, -1):
        val = jnp.where(rel < BUCKET_STARTS[b + 1], rb_ref[b, h] - last, val)
    return jnp.where(rel >= 0, val, NEG_INF)


def _prompt_bias_kernel(rb_ref, o_ref, *, tb):
    h = pl.program_id(0)
    o = pl.program_id(1)
    rel = (lax.broadcasted_iota(jnp.int32, (tb, tb), 0)
           - lax.broadcasted_iota(jnp.int32, (tb, tb), 1) + o * tb)
    o_ref[0, 0] = _shifted_bias(rel, rb_ref, h)


def prompt_bias(rel_bias, tb):
    n_heads = rel_bias.shape[1]
    return pl.pallas_call(
        functools.partial(_prompt_bias_kernel, tb=tb),
        grid=(n_heads, 2),
        in_specs=[pl.BlockSpec(memory_space=pltpu.SMEM)],
        out_specs=pl.BlockSpec((1, 1, tb, tb), lambda h, o: (h, o, 0, 0)),
        out_shape=jax.ShapeDtypeStruct((n_heads, 2, tb, tb), F32),
        name="prompt_bias",
        compiler_params=_cparams(("parallel", "parallel")),
    )(rel_bias)


def _sample_bias_kernel(rb_ref, o_ref, *, tq, page, n_heads):
    rows = 2 * tq
    row = lax.broadcasted_iota(jnp.int32, (rows, 2 * page), 0)
    col = lax.broadcasted_iota(jnp.int32, (rows, 2 * page), 1)
    t = row % tq
    rel = jnp.where(col < page, page + t - col, t - (col - page))
    for h in range(n_heads):
        o_ref[h * rows:(h + 1) * rows, :] = _shifted_bias(rel, rb_ref, h)


def sample_bias(rel_bias, tq, page):
    n_heads = rel_bias.shape[1]
    return pl.pallas_call(
        functools.partial(_sample_bias_kernel, tq=tq, page=page, n_heads=n_heads),
        in_specs=[pl.BlockSpec(memory_space=pltpu.SMEM)],
        out_specs=pl.BlockSpec(memory_space=pltpu.VMEM),
        out_shape=jax.ShapeDtypeStruct((n_heads * 2 * tq, 2 * page), F32),
        name="sample_bias",
    )(rel_bias)


def _lambda(lq1_ref, lk1_ref, lq2_ref, lk2_ref, lam_init):
    s1 = jnp.sum(lq1_ref[...] * lk1_ref[...], axis=1, keepdims=True)
    s2 = jnp.sum(lq2_ref[...] * lk2_ref[...], axis=1, keepdims=True)
    return jnp.exp(s1) - jnp.exp(s2) + lam_init


def _attn_kernel(q_ref, k_ref, v_ref, bias_ref, lq1_ref, lk1_ref, lq2_ref, lk2_ref, subln_ref, o_ref,
                 m_scr, l_scr, acc_scr, *, tb, lam_init):
    qi = pl.program_id(2)
    q = q_ref[0]
    lane = lax.broadcasted_iota(jnp.int32, q.shape, 1)
    qs = (jnp.where(lane < HEAD_DIM, q, jnp.zeros_like(q)),
          jnp.where(lane >= HEAD_DIM, q, jnp.zeros_like(q)))
    m_scr[...] = jnp.full_like(m_scr, NEG_INF)
    l_scr[...] = jnp.zeros_like(l_scr)
    acc_scr[...] = jnp.zeros_like(acc_scr)

    def process(kj, bias):
        start = pl.multiple_of(kj * tb, tb)
        k = k_ref[0, pl.ds(start, tb), :]
        v = v_ref[0, pl.ds(start, tb), :]
        for c in range(2):
            s = _dot_nt(qs[c], k)
            if bias is not None:
                s = s + bias
            m_prev = m_scr[c]
            m_new = jnp.maximum(m_prev, jnp.max(s, axis=1, keepdims=True))
            alpha = jnp.exp(m_prev - m_new)
            p = jnp.exp(s - m_new)
            l_scr[c] = alpha * l_scr[c] + jnp.sum(p, axis=1, keepdims=True)
            acc_scr[c] = alpha * acc_scr[c] + _dot(p.astype(BF16), v)
            m_scr[c] = m_new

    def far(kj, carry):
        process(kj, None)
        return carry

    lax.fori_loop(0, jnp.maximum(qi - 1, 0), far, 0)

    @pl.when(qi >= 1)
    def _():
        process(qi - 1, bias_ref[0, 1])

    process(qi, bias_ref[0, 0])

    lam = _lambda(lq1_ref, lk1_ref, lq2_ref, lk2_ref, lam_init)
    o = acc_scr[0] / l_scr[0] - lam * (acc_scr[1] / l_scr[1])
    o = _rms(o, subln_ref[...]) * (1.0 - lam_init)
    o_ref[0] = o.astype(o_ref.dtype)


def prompt_attention(q, k, v, bias, lq1, lk1, lq2, lk2, subln, *, tb, lam_init):
    b, t, dk = q.shape
    hw = 2 * HEAD_DIM
    n_heads = dk // hw
    vec = lambda shape: pl.BlockSpec(shape, lambda bb, h, i: (0, 0))
    return pl.pallas_call(
        functools.partial(_attn_kernel, tb=tb, lam_init=lam_init),
        grid=(b, n_heads, t // tb),
        in_specs=[pl.BlockSpec((1, tb, hw), lambda bb, h, i: (bb, i, h)),
                  pl.BlockSpec((1, t, hw), lambda bb, h, i: (bb, 0, h)),
                  pl.BlockSpec((1, t, hw), lambda bb, h, i: (bb, 0, h)),
                  pl.BlockSpec((1, 2, tb, tb), lambda bb, h, i: (h, 0, 0, 0)),
                  vec((1, HEAD_DIM)), vec((1, HEAD_DIM)), vec((1, HEAD_DIM)), vec((1, HEAD_DIM)),
                  vec((1, hw))],
        out_specs=pl.BlockSpec((1, tb, hw), lambda bb, h, i: (bb, i, h)),
        out_shape=jax.ShapeDtypeStruct((b, t, dk), BF16),
        scratch_shapes=[pltpu.VMEM((2, tb, 1), F32),
                        pltpu.VMEM((2, tb, 1), F32),
                        pltpu.VMEM((2, tb, hw), F32)],
        name="prompt_attention",
        compiler_params=_cparams(("parallel", "parallel", "arbitrary")),
    )(q, k, v, bias, lq1, lk1, lq2, lk2, subln)


def _sample_attn_kernel(pt_ref, *refs, n_pages, page, tq, n_heads, lam_init):
    k_refs = refs[:n_pages]
    v_refs = refs[n_pages:2 * n_pages]
    (q_ref, kn_ref, vn_ref, bias_ref, lq1_ref, lk1_ref, lq2_ref, lk2_ref, subln_ref,
     o_ref, kbf, vbf) = refs[2 * n_pages:]
    past = n_pages * page
    hw = 2 * HEAD_DIM
    rows = n_heads * 2 * tq
    dk = n_heads * hw

    @pl.when(pl.program_id(0) == 0)
    def _():
        kbf[past:, :] = jnp.zeros((page, dk), BF16)
        vbf[past:, :] = jnp.zeros((page, dk), BF16)

    for p in range(n_pages):
        kbf[p * page:(p + 1) * page, :] = k_refs[p][0].astype(BF16)
        vbf[p * page:(p + 1) * page, :] = v_refs[p][0].astype(BF16)
    nn = kn_ref.shape[1]
    kbf[past:past + nn, :] = kn_ref[0]
    vbf[past:past + nn, :] = vn_ref[0]

    qrep = jnp.concatenate([q_ref[0]] * n_heads, axis=0)
    rr = lax.broadcasted_iota(jnp.int32, (rows, dk), 0)
    cc = lax.broadcasted_iota(jnp.int32, (rows, dk), 1)
    qbd = jnp.where(cc // HEAD_DIM == rr // tq, qrep, 0.0).astype(BF16)

    s = _dot_nt(qbd, kbf[...])
    tail = s[:, past - page:] + bias_ref[...]
    m = jnp.maximum(jnp.max(s[:, :past - page], axis=1, keepdims=True),
                    jnp.max(tail, axis=1, keepdims=True))
    p_head = jnp.exp(s[:, :past - page] - m)
    p_tail = jnp.exp(tail - m)
    l = jnp.sum(p_head, axis=1, keepdims=True) + jnp.sum(p_tail, axis=1, keepdims=True)
    pv = (_dot(p_head.astype(BF16), vbf[:past - page, :])
          + _dot(p_tail.astype(BF16), vbf[past - page:, :]))
    pv = pv / l
    pv = jnp.where(cc // hw == rr // (2 * tq), pv, 0.0)
    o8 = pv[0:2 * tq]
    for h in range(1, n_heads):
        o8 = o8 + pv[h * 2 * tq:(h + 1) * 2 * tq]
    lam = _lambda(lq1_ref, lk1_ref, lq2_ref, lk2_ref, lam_init)
    o = o8[:tq] - lam * o8[tq:]
    outs = []
    for h in range(n_heads):
        outs.append(_rms(o[:, h * hw:(h + 1) * hw], subln_ref[...]) * (1.0 - lam_init))
    o_ref[0] = jnp.concatenate(outs, axis=1)


def sample_attention(page_table, cache_k, cache_v, q8, k_new, v_new, bias, lq1, lk1, lq2, lk2, subln,
                     *, tq, lam_init):
    n_seq, n_pages = page_table.shape
    _, page, dk = cache_k.shape
    hw = 2 * HEAD_DIM
    n_heads = dk // hw
    nn = k_new.shape[1]
    past = n_pages * page

    def page_spec(p):
        return pl.BlockSpec((1, page, dk), lambda b, pt: (pt[b * n_pages + p], 0, 0))

    vec = lambda shape: pl.BlockSpec(shape, lambda b, pt: (0, 0))
    in_specs = ([page_spec(p) for p in range(n_pages)] * 2
                + [pl.BlockSpec((1, 2 * tq, dk), lambda b, pt: (b, 0, 0)),
                   pl.BlockSpec((1, nn, dk), lambda b, pt: (b, 0, 0)),
                   pl.BlockSpec((1, nn, dk), lambda b, pt: (b, 0, 0)),
                   vec(bias.shape),
                   vec((1, HEAD_DIM)), vec((1, HEAD_DIM)), vec((1, HEAD_DIM)), vec((1, HEAD_DIM)),
                   vec((1, hw))])
    grid_spec = pltpu.PrefetchScalarGridSpec(
        num_scalar_prefetch=1,
        grid=(n_seq,),
        in_specs=in_specs,
        out_specs=pl.BlockSpec((1, tq, dk), lambda b, pt: (b, 0, 0)),
        scratch_shapes=[pltpu.VMEM((past + page, dk), BF16),
                        pltpu.VMEM((past + page, dk), BF16)])
    kern = functools.partial(_sample_attn_kernel, n_pages=n_pages, page=page, tq=tq,
                             n_heads=n_heads, lam_init=lam_init)
    return pl.pallas_call(
        kern,
        grid_spec=grid_spec,
        out_shape=jax.ShapeDtypeStruct((n_seq, tq, dk), F32),
        name="sample_attention",
        compiler_params=_cparams(("arbitrary",)),
    )(page_table.reshape(-1), *([cache_k] * n_pages), *([cache_v] * n_pages),
      q8, k_new, v_new, bias, lq1, lk1, lq2, lk2, subln)


def _pick_tile(n, target):
    t = min(n, target)
    while n % t:
        t //= 2
    return t


def _run_trunk(x, pos0, ts, rg_h0, rg_cb0, ffn_fb0, w, attn_fn):
    bk, r, d = x.shape
    depth = w['norm_mix'].shape[0]
    n_a = w['rg_w_in'].shape[0]
    tm_rg = _pick_tile(r, 256)
    tm_ffn = _pick_tile(r, 1024)
    tm_mm = _pick_tile(bk * r, 512)
    dk = w['attn_w_q'].shape[2]
    new_h, new_cb, new_fb = [], [], []
    k_new = v_new = kb = vb = None
    for l in range(depth):
        if l < n_a:
            x, h_last, cb = rglru_layer(
                x, rg_h0[l], rg_cb0[l], w['norm_mix'][l][None], w['rg_w_in'][l], w['rg_conv_w'][l],
                w['rg_conv_b'][l][None], w['rg_w_ax'][l], w['rg_b_a'][l][None], w['rg_b_x'][l][None],
                w['rg_lambda'][l][None], w['rg_w_out'][l], tm=tm_rg, ts=ts, pos0=pos0)
            new_h.append(h_last)
            new_cb.append(cb)
        else:
            x2 = x.reshape(bk * r, d)
            if l == n_a:
                k_new, v_new, kb, vb = norm_matmul(
                    x2, w['kv_norm'][None], w['w_kv'],
                    [(0, dk, 1.0), (dk, 2 * dk, 1.0), (0, dk, 1.0), (dk, 2 * dk, 1.0)],
                    [F32, F32, BF16, BF16], tm_mm)
            j = l - n_a
            q_dt = attn_fn.q_dtype
            (q,) = norm_matmul(x2, w['norm_mix'][l][None], w['attn_w_q'][j],
                               [(0, dk, ATTN_SCALE)], [q_dt], tm_mm)
            o = attn_fn(j, l, q, kb, vb)
            x = residual_matmul(x2, o, w['attn_w_o'][j], tm_mm).reshape(bk, r, d)
        last = l == depth - 1
        x, fb = ffn_layer(x, ffn_fb0[l], w['norm_ffn'][l][None], w['ffn_w_up'][l], w['ffn_conv_w'][l],
                          w['ffn_conv_b'][l][None], w['ffn_w_down'][l], w['final_norm'][None],
                          tm=tm_ffn, tn=512, ts=ts, final_norm=last)
        new_fb.append(fb)
    return x, new_h, new_cb, new_fb, k_new, v_new


def _lam_init(layer_idx):
    return 0.8 - 0.6 * math.exp(-0.3 * layer_idx)


def kernel(x_prompt, x_sample, state_rglru_h, state_rglru_conv, state_ffn_conv, cache_k, cache_v,
           page_table, rel_bias, norm_mix, norm_ffn, final_norm, rg_w_in, rg_conv_w, rg_conv_b,
           rg_w_a, rg_b_a, rg_w_x, rg_b_x, rg_lambda, rg_w_out, kv_norm, w_kv, attn_w_q,
           lam_q1, lam_k1, lam_q2, lam_k2, attn_subln, attn_w_o, ffn_w_up, ffn_conv_w,
           ffn_conv_b, ffn_w_down):
    bp, t_p, d = x_prompt.shape
    n_seq, tq, _ = x_sample.shape
    depth = norm_mix.shape[0]
    n_a = rg_w_in.shape[0]
    n_pool, page, n_heads, _, hd = cache_k.shape
    assert hd == HEAD_DIM
    dk = n_heads * 2 * hd
    f2 = ffn_w_up.shape[2]
    past_len = page_table.shape[1] * page
    assert page >= FAR_DISTANCE and past_len >= 2 * page and 2 * tq == V7X_SUBLANES

    w = {
        'norm_mix': norm_mix, 'norm_ffn': norm_ffn, 'final_norm': final_norm,
        'rg_w_in': rg_w_in.astype(BF16), 'rg_conv_w': rg_conv_w, 'rg_conv_b': rg_conv_b,
        'rg_w_ax': jnp.concatenate([rg_w_a, rg_w_x], axis=-1).astype(BF16),
        'rg_b_a': rg_b_a, 'rg_b_x': rg_b_x, 'rg_lambda': rg_lambda,
        'rg_w_out': rg_w_out.astype(BF16), 'kv_norm': kv_norm, 'w_kv': w_kv.astype(BF16),
        'attn_w_q': attn_w_q.astype(BF16), 'attn_w_o': attn_w_o.astype(BF16),
        'ffn_w_up': ffn_w_up.astype(BF16), 'ffn_conv_w': ffn_conv_w, 'ffn_conv_b': ffn_conv_b,
        'ffn_w_down': ffn_w_down.astype(BF16),
    }
    lam_vecs = lambda j: (lam_q1[j][None], lam_k1[j][None], lam_q2[j][None], lam_k2[j][None])

    tb = _pick_tile(t_p, 512)
    assert tb >= FAR_DISTANCE
    bias_p = prompt_bias(rel_bias, tb)

    def attn_prompt(j, l, q, kb, vb):
        o = prompt_attention(q.reshape(bp, t_p, dk), kb.reshape(bp, t_p, dk), vb.reshape(bp, t_p, dk),
                             bias_p, *lam_vecs(j), attn_subln[j][None], tb=tb, lam_init=_lam_init(l))
        return o.reshape(bp * t_p, dk)
    attn_prompt.q_dtype = BF16

    cw_rg = rg_conv_w.shape[1]
    cw_ffn = ffn_conv_w.shape[1]
    yp, hp, cbp, fbp, kp, vp = _run_trunk(
        x_prompt, 0, 1,
        jnp.zeros((n_a, bp, 1, d), F32), jnp.zeros((n_a, bp, cw_rg - 1, d), F32),
        jnp.zeros((depth, bp, cw_ffn - 1, f2), F32), w, attn_prompt)

    bias_s = sample_bias(rel_bias, tq, page)
    ck = cache_k.reshape(n_pool, page, dk)
    cv = cache_v.reshape(n_pool, page, dk)
    n_new = 16

    def to_seq_major(a2):
        return a2.reshape(tq, n_seq, -1).transpose(1, 0, 2)

    def attn_sample(j, l, q, kb, vb):
        q4 = to_seq_major(q)
        q8 = jnp.concatenate([q4, q4], axis=1)
        pad = ((0, 0), (0, n_new - tq), (0, 0))
        o = sample_attention(page_table, ck, cv, q8, jnp.pad(to_seq_major(kb), pad),
                             jnp.pad(to_seq_major(vb), pad), bias_s, *lam_vecs(j), attn_subln[j][None],
                             tq=tq, lam_init=_lam_init(l))
        return o.transpose(1, 0, 2).reshape(tq * n_seq, dk)
    attn_sample.q_dtype = F32

    def tmajor(a):
        return a.transpose(0, 2, 1, 3).reshape(a.shape[0], 1, a.shape[2] * n_seq, a.shape[3])

    def smajor(a2, steps):
        return a2.reshape(steps, n_seq, -1).transpose(1, 0, 2)

    xs = x_sample.transpose(1, 0, 2).reshape(1, tq * n_seq, d)
    ys, hs, cbs, fbs, ks, vs = _run_trunk(
        xs, past_len, n_seq, state_rglru_h[:, None], tmajor(state_rglru_conv), tmajor(state_ffn_conv),
        w, attn_sample)

    return (yp,
            smajor(ys, tq),
            jnp.stack([h[:, 0] for h in hp]),
            jnp.stack(cbp),
            jnp.stack(fbp),
            kp.reshape(bp, t_p, n_heads, 2, hd),
            vp.reshape(bp, t_p, n_heads, 2 * hd),
            jnp.stack([h[0] for h in hs]),
            jnp.stack([smajor(c, cw_rg - 1) for c in cbs]),
            jnp.stack([smajor(fb, cw_ffn - 1) for fb in fbs]),
            smajor(ks[None], tq).reshape(n_seq, tq, n_heads, 2, hd),
            smajor(vs[None], tq).reshape(n_seq, tq, n_heads, 2 * hd))
```

```python
import functools
import math

import jax
import jax.numpy as jnp
from jax import lax
from jax.experimental import pallas as pl
from jax.experimental.pallas import tpu as pltpu

EPS = 1e-6
RG_C = 8.0
HEAD_DIM = 64
N_BUCKETS = 32
MAX_DISTANCE = 128
NEG_INF = -1e30
ATTN_SCALE = HEAD_DIM ** -0.5

V7X_LANES = 128
V7X_SUBLANES = 8
V7X_VMEM_BYTES = 64 * 2 ** 20
VMEM_LIMIT = V7X_VMEM_BYTES - 8 * 2 ** 20

BF16 = jnp.bfloat16
F32 = jnp.float32


def _bucket_starts():
    max_exact = N_BUCKETS // 2
    starts = list(range(max_exact + 1))
    for b in range(max_exact + 1, N_BUCKETS):
        n = starts[-1]
        while True:
            n += 1
            large = max_exact + int(math.log(n / max_exact) / math.log(MAX_DISTANCE / max_exact)
                                    * (N_BUCKETS - max_exact))
            if min(large, N_BUCKETS - 1) >= b:
                break
        starts.append(n)
    return tuple(starts)


BUCKET_STARTS = _bucket_starts()
FAR_DISTANCE = BUCKET_STARTS[-1]


def _cparams(sem, vmem=VMEM_LIMIT):
    return pltpu.CompilerParams(dimension_semantics=sem, vmem_limit_bytes=vmem)


def _rms(x, g):
    return x * lax.rsqrt(jnp.mean(x * x, axis=-1, keepdims=True) + EPS) * g


def _dot(a, b):
    return jnp.dot(a, b, preferred_element_type=F32)


def _dot_nt(a, b):
    return lax.dot_general(a, b, (((1,), (1,)), ((), ())), preferred_element_type=F32)


def _norm_mm_kernel(x_ref, g_ref, w_ref, *out_refs, splits):
    xn = _rms(x_ref[...], g_ref[...]).astype(BF16)
    y = _dot(xn, w_ref[...])
    for o_ref, (lo, hi, scale) in zip(out_refs, splits):
        part = y[:, lo:hi]
        if scale != 1.0:
            part = part * scale
        o_ref[...] = part.astype(o_ref.dtype)


def norm_matmul(x, g, w, splits, dtypes, tm):
    r, d = x.shape
    n = w.shape[1]
    out_shape = [jax.ShapeDtypeStruct((r, hi - lo), dt) for (lo, hi, _), dt in zip(splits, dtypes)]
    out_specs = [pl.BlockSpec((tm, hi - lo), lambda i: (i, 0)) for (lo, hi, _) in splits]
    return pl.pallas_call(
        functools.partial(_norm_mm_kernel, splits=tuple(splits)),
        grid=(r // tm,),
        in_specs=[pl.BlockSpec((tm, d), lambda i: (i, 0)),
                  pl.BlockSpec((1, d), lambda i: (0, 0)),
                  pl.BlockSpec((d, n), lambda i: (0, 0))],
        out_specs=out_specs,
        out_shape=out_shape,
        name="norm_matmul",
        compiler_params=_cparams(("parallel",)),
    )(x, g, w)


def _res_mm_kernel(x_ref, a_ref, w_ref, o_ref):
    o_ref[...] = x_ref[...] + _dot(a_ref[...].astype(BF16), w_ref[...])


def residual_matmul(x, a, w, tm):
    r, d = x.shape
    k = a.shape[1]
    return pl.pallas_call(
        _res_mm_kernel,
        grid=(r // tm,),
        in_specs=[pl.BlockSpec((tm, d), lambda i: (i, 0)),
                  pl.BlockSpec((tm, k), lambda i: (i, 0)),
                  pl.BlockSpec((k, d), lambda i: (0, 0))],
        out_specs=pl.BlockSpec((tm, d), lambda i: (i, 0)),
        out_shape=jax.ShapeDtypeStruct((r, d), F32),
        name="residual_matmul",
        compiler_params=_cparams(("parallel",)),
    )(x, a, w)


def _rg_kernel(x_ref, h0_ref, cb0_ref, g_ref, win_ref, cw_ref, cbias_ref, wax_ref, ba_ref, bx_ref,
               lam_ref, wout_ref, out_ref, hlast_ref, cbout_ref,
               rbuf, a_scr, b_scr, h_scr, *, tm, ts, pos0, conv_w, n_blocks):
    ti = pl.program_id(1)
    d = a_scr.shape[1]
    hdr = rbuf.shape[0] - tm
    nprev = (conv_w - 1) * ts

    @pl.when(ti == 0)
    def _():
        rbuf[hdr - nprev:hdr, :] = cb0_ref[0]
        h_scr[...] = h0_ref[0]

    x = x_ref[0]
    xn = _rms(x, g_ref[...]).astype(BF16)
    u = _dot(xn, win_ref[...])
    gate = u[:, :d]
    rbuf[hdr:hdr + tm, :] = u[:, d:]
    xc = cbias_ref[...] + rbuf[hdr:hdr + tm, :] * cw_ref[conv_w - 1:conv_w, :]
    for j in range(conv_w - 1):
        off = hdr - (conv_w - 1 - j) * ts
        xc = xc + rbuf[off:off + tm, :] * cw_ref[j:j + 1, :]
    new_prev = rbuf[hdr + tm - nprev:hdr + tm, :]
    cbout_ref[0] = new_prev
    rbuf[hdr - nprev:hdr, :] = new_prev

    xcb = xc.astype(BF16)
    blk = d // n_blocks
    ras, rxs = [], []
    for n in range(n_blocks):
        rr = _dot(xcb[:, n * blk:(n + 1) * blk], wax_ref[n])
        ras.append(rr[:, :blk])
        rxs.append(rr[:, blk:])
    r = jax.nn.sigmoid(jnp.concatenate(ras, axis=1) + ba_ref[...])
    i = jax.nn.sigmoid(jnp.concatenate(rxs, axis=1) + bx_ref[...])
    z = -lam_ref[...]
    softplus = jnp.maximum(z, 0.0) + jnp.log(1.0 + jnp.exp(-jnp.abs(z)))
    log_a = (-RG_C) * r * softplus
    a = jnp.exp(log_a)
    mult = jnp.sqrt(1.0 - a * a)
    if pos0 == 0:
        row = lax.broadcasted_iota(jnp.int32, (tm, 1), 0) + ti * tm
        reset = row < ts
        a = jnp.where(reset, 0.0, a)
        mult = jnp.where(reset, 1.0, mult)
    a_scr[...] = a
    b_scr[...] = mult * i * xc

    if ts == 1:
        def group(gi, h):
            base = pl.multiple_of(gi * V7X_SUBLANES, V7X_SUBLANES)
            for rr_ in range(V7X_SUBLANES):
                h = a_scr[pl.ds(base + rr_, 1), :] * h + b_scr[pl.ds(base + rr_, 1), :]
                b_scr[pl.ds(base + rr_, 1), :] = h
            return h
        h = lax.fori_loop(0, tm // V7X_SUBLANES, group, h_scr[...])
    else:
        h = h_scr[...]
        for t in range(tm // ts):
            h = a_scr[t * ts:(t + 1) * ts, :] * h + b_scr[t * ts:(t + 1) * ts, :]
            b_scr[t * ts:(t + 1) * ts, :] = h
    h_scr[...] = h
    hlast_ref[0] = h

    hg = (b_scr[...] * jax.nn.gelu(gate)).astype(BF16)
    out_ref[0] = x + _dot(hg, wout_ref[...])


def rglru_layer(x, h0, cb0, g, w_in, conv_w, conv_b, w_ax, b_a, b_x, lam, w_out, *, tm, ts, pos0):
    bk, r, d = x.shape
    cw = conv_w.shape[0]
    nprev = (cw - 1) * ts
    hdr = max(V7X_SUBLANES, nprev)
    n_blocks = w_ax.shape[0]
    const2 = lambda b, t: (0, 0)
    kern = functools.partial(_rg_kernel, tm=tm, ts=ts, pos0=pos0, conv_w=cw, n_blocks=n_blocks)
    return pl.pallas_call(
        kern,
        grid=(bk, r // tm),
        in_specs=[pl.BlockSpec((1, tm, d), lambda b, t: (b, t, 0)),
                  pl.BlockSpec((1, ts, d), lambda b, t: (b, 0, 0)),
                  pl.BlockSpec((1, nprev, d), lambda b, t: (b, 0, 0)),
                  pl.BlockSpec((1, d), const2),
                  pl.BlockSpec((d, 2 * d), const2),
                  pl.BlockSpec((cw, d), const2),
                  pl.BlockSpec((1, d), const2),
                  pl.BlockSpec(w_ax.shape, lambda b, t: (0, 0, 0)),
                  pl.BlockSpec((1, d), const2),
                  pl.BlockSpec((1, d), const2),
                  pl.BlockSpec((1, d), const2),
                  pl.BlockSpec((d, d), const2)],
        out_specs=[pl.BlockSpec((1, tm, d), lambda b, t: (b, t, 0)),
                   pl.BlockSpec((1, ts, d), lambda b, t: (b, 0, 0)),
                   pl.BlockSpec((1, nprev, d), lambda b, t: (b, 0, 0))],
        out_shape=[jax.ShapeDtypeStruct((bk, r, d), F32),
                   jax.ShapeDtypeStruct((bk, ts, d), F32),
                   jax.ShapeDtypeStruct((bk, nprev, d), F32)],
        scratch_shapes=[pltpu.VMEM((hdr + tm, d), F32),
                        pltpu.VMEM((tm, d), F32),
                        pltpu.VMEM((tm, d), F32),
                        pltpu.VMEM((ts, d), F32)],
        name="rglru_layer",
        compiler_params=_cparams(("parallel", "arbitrary")),
    )(x, h0, cb0, g, w_in, conv_w, conv_b, w_ax, b_a, b_x, lam, w_out)


def _ffn_kernel(x_ref, fbv0_ref, fbg0_ref, g_ref, wv_ref, wg_ref, cwv_ref, cwg_ref, cbv_ref, cbg_ref,
                wd_ref, gfin_ref, out_ref, fbv_ref, fbg_ref,
                xn_scr, acc_scr, ubv, ubg, carry_v, carry_g, *, tm, ts, conv_w, final_norm):
    ti = pl.program_id(1)
    j = pl.program_id(2)
    nj = pl.num_programs(2)
    hdr = ubv.shape[0] - tm
    nprev = (conv_w - 1) * ts

    @pl.when(j == 0)
    def _():
        xn_scr[...] = _rms(x_ref[0], g_ref[...]).astype(BF16)
        acc_scr[...] = jnp.zeros_like(acc_scr)

    xn = xn_scr[...]

    def conv(w_ref, ub, carry, fb0_ref, cw_ref, cb_ref, fb_ref):
        @pl.when(ti == 0)
        def _():
            ub[hdr - nprev:hdr, :] = fb0_ref[0]

        @pl.when(ti > 0)
        def _():
            ub[hdr - nprev:hdr, :] = carry[j]

        ub[hdr:hdr + tm, :] = _dot(xn, w_ref[...])
        uc = cb_ref[...] + ub[hdr:hdr + tm, :] * cw_ref[conv_w - 1:conv_w, :]
        for jj in range(conv_w - 1):
            off = hdr - (conv_w - 1 - jj) * ts
            uc = uc + ub[off:off + tm, :] * cw_ref[jj:jj + 1, :]
        new_prev = ub[hdr + tm - nprev:hdr + tm, :]
        carry[j] = new_prev
        fb_ref[0, 0] = new_prev
        return uc

    val = conv(wv_ref, ubv, carry_v, fbv0_ref, cwv_ref, cbv_ref, fbv_ref)
    gte = conv(wg_ref, ubg, carry_g, fbg0_ref, cwg_ref, cbg_ref, fbg_ref)
    hmid = (jax.nn.gelu(gte) * val).astype(BF16)
    acc_scr[...] += _dot(hmid, wd_ref[...])

    @pl.when(j == nj - 1)
    def _():
        y = x_ref[0] + acc_scr[...]
        if final_norm:
            y = _rms(y, gfin_ref[...])
        out_ref[0] = y


def ffn_layer(x, fb0, g, w_up, conv_w, conv_b, w_down, g_final, *, tm, tn, ts, final_norm):
    bk, r, d = x.shape
    f = w_down.shape[0]
    cw = conv_w.shape[0]
    nprev = (cw - 1) * ts
    hdr = max(V7X_SUBLANES, nprev)
    nc = f // tn
    const2 = lambda b, t, j: (0, 0)
    kern = functools.partial(_ffn_kernel, tm=tm, ts=ts, conv_w=cw, final_norm=final_norm)
    out, fbv, fbg = pl.pallas_call(
        kern,
        grid=(bk, r // tm, nc),
        in_specs=[pl.BlockSpec((1, tm, d), lambda b, t, j: (b, t, 0)),
                  pl.BlockSpec((1, nprev, tn), lambda b, t, j: (b, 0, j)),
                  pl.BlockSpec((1, nprev, tn), lambda b, t, j: (b, 0, nc + j)),
                  pl.BlockSpec((1, d), const2),
                  pl.BlockSpec((d, tn), lambda b, t, j: (0, j)),
                  pl.BlockSpec((d, tn), lambda b, t, j: (0, nc + j)),
                  pl.BlockSpec((cw, tn), lambda b, t, j: (0, j)),
                  pl.BlockSpec((cw, tn), lambda b, t, j: (0, nc + j)),
                  pl.BlockSpec((1, tn), lambda b, t, j: (0, j)),
                  pl.BlockSpec((1, tn), lambda b, t, j: (0, nc + j)),
                  pl.BlockSpec((tn, d), lambda b, t, j: (j, 0)),
                  pl.BlockSpec((1, d), const2)],
        out_specs=[pl.BlockSpec((1, tm, d), lambda b, t, j: (b, t, 0)),
                   pl.BlockSpec((1, 1, nprev, tn), lambda b, t, j: (b, t, 0, j)),
                   pl.BlockSpec((1, 1, nprev, tn), lambda b, t, j: (b, t, 0, j))],
        out_shape=[jax.ShapeDtypeStruct((bk, r, d), F32),
                   jax.ShapeDtypeStruct((bk, r // tm, nprev, f), F32),
                   jax.ShapeDtypeStruct((bk, r // tm, nprev, f), F32)],
        scratch_shapes=[pltpu.VMEM((tm, d), BF16),
                        pltpu.VMEM((tm, d), F32),
                        pltpu.VMEM((hdr + tm, tn), F32),
                        pltpu.VMEM((hdr + tm, tn), F32),
                        pltpu.VMEM((nc, nprev, tn), F32),
                        pltpu.VMEM((nc, nprev, tn), F32)],
        name="ffn_layer",
        compiler_params=_cparams(("parallel", "arbitrary", "arbitrary")),
    )(x, fb0, fb0, g, w_up, w_up, conv_w, conv_w, conv_b, conv_b, w_down, g_final)
    return out, jnp.concatenate([fbv[:, -1], fbg[:, -1]], axis=-1)


def _shifted_bias(rel, rb_ref, h):
    last = rb_ref[N_BUCKETS - 1, h]
    val = jnp.zeros(rel.shape, F32)
    for b in range(N_BUCKETS - 2, -1, -1):
        val = jnp.where(rel < BUCKET_STARTS[b + 1], rb_ref[b, h] - last, val)
    return jnp.where(rel >= 0, val, NEG_INF)


def _near_offsets(tq, tk):
    step = math.gcd(tq, tk)
    lo = -(tq - 1)
    hi = tk - 1 + FAR_DISTANCE - 1
    first = -((-lo) // step) * step
    if first < lo:
        first += step
    count = (hi - first) // step + 1
    return first, step, count


def _prompt_bias_kernel(rb_ref, o_ref, *, tq, tk, first, step):
    h = pl.program_id(0)
    o = pl.program_id(1)
    rel = (lax.broadcasted_iota(jnp.int32, (tk, tq), 1)
           - lax.broadcasted_iota(jnp.int32, (tk, tq), 0) + (first + o * step))
    o_ref[0, 0] = _shifted_bias(rel, rb_ref, h)


def prompt_bias(rel_bias, tq, tk):
    n_heads = rel_bias.shape[1]
    first, step, count = _near_offsets(tq, tk)
    return pl.pallas_call(
        functools.partial(_prompt_bias_kernel, tq=tq, tk=tk, first=first, step=step),
        grid=(n_heads, count),
        in_specs=[pl.BlockSpec(memory_space=pltpu.SMEM)],
        out_specs=pl.BlockSpec((1, 1, tk, tq), lambda h, o: (h, o, 0, 0)),
        out_shape=jax.ShapeDtypeStruct((n_heads, count, tk, tq), F32),
        name="prompt_bias",
        compiler_params=_cparams(("parallel", "parallel")),
    )(rel_bias)


def _sample_bias_kernel(rb_ref, o_ref, *, tq, page, n_heads):
    rows = 2 * tq
    row = lax.broadcasted_iota(jnp.int32, (rows, 2 * page), 0)
    col = lax.broadcasted_iota(jnp.int32, (rows, 2 * page), 1)
    t = row % tq
    rel = jnp.where(col < page, page + t - col, t - (col - page))
    for h in range(n_heads):
        o_ref[h * rows:(h + 1) * rows, :] = _shifted_bias(rel, rb_ref, h)


def sample_bias(rel_bias, tq, page):
    n_heads = rel_bias.shape[1]
    return pl.pallas_call(
        functools.partial(_sample_bias_kernel, tq=tq, page=page, n_heads=n_heads),
        in_specs=[pl.BlockSpec(memory_space=pltpu.SMEM)],
        out_specs=pl.BlockSpec(memory_space=pltpu.VMEM),
        out_shape=jax.ShapeDtypeStruct((n_heads * 2 * tq, 2 * page), F32),
        name="sample_bias",
    )(rel_bias)


def _lambda(lq1_ref, lk1_ref, lq2_ref, lk2_ref, lam_init):
    s1 = jnp.sum(lq1_ref[...] * lk1_ref[...], axis=1, keepdims=True)
    s2 = jnp.sum(lq2_ref[...] * lk2_ref[...], axis=1, keepdims=True)
    return jnp.exp(s1) - jnp.exp(s2) + lam_init


def _attn_kernel(q_ref, k_ref, vt_ref, bias_ref, lq1_ref, lk1_ref, lq2_ref, lk2_ref, subln_ref, o_ref,
                 qs_scr, m_scr, acc_scr, *, tq, tk, lam_init):
    qi = pl.program_id(2)
    hw = 2 * HEAD_DIM
    q = q_ref[0]
    lane = lax.broadcasted_iota(jnp.int32, q.shape, 1)
    qs_scr[:tq, :] = jnp.where(lane < HEAD_DIM, q, jnp.zeros_like(q))
    qs_scr[tq:, :] = jnp.where(lane >= HEAD_DIM, q, jnp.zeros_like(q))
    m_scr[...] = jnp.full_like(m_scr, NEG_INF)
    acc_scr[...] = jnp.zeros_like(acc_scr)

    def scores(kj):
        start = pl.multiple_of(kj * tk, tk)
        k = k_ref[0, pl.ds(start, tk), :]
        return _dot_nt(k, qs_scr[...])

    def update(kj, s, bias):
        start = pl.multiple_of(kj * tk, tk)
        vt = vt_ref[0, 0, :, pl.ds(start, tk)]
        if bias is not None:
            s = s + jnp.concatenate([bias, bias], axis=1)
        m_prev = m_scr[...]
        m_new = jnp.maximum(m_prev, jnp.max(s, axis=0, keepdims=True))
        alpha = jnp.exp(m_prev - m_new)
        p = jnp.exp(s - m_new).astype(BF16)
        acc_scr[...] = alpha * acc_scr[...] + _dot(vt, p)
        m_scr[...] = m_new

    def process(kj, bias):
        update(kj, scores(kj), bias)

    first, step, count = _near_offsets(tq, tk)
    hi = first + (count - 1) * step
    n_far = jnp.maximum(qi * tq - hi + tk - 1, 0) // tk
    k_end = (qi * tq + tq - 1) // tk + 1

    def far_pair(i, carry):
        s0 = scores(2 * i)
        s1 = scores(2 * i + 1)
        update(2 * i, s0, None)
        update(2 * i + 1, s1, None)
        return carry

    lax.fori_loop(0, n_far // 2, far_pair, 0)

    @pl.when(n_far % 2 == 1)
    def _():
        process(n_far - 1, None)

    def near(kj, carry):
        process(kj, bias_ref[0, (qi * tq - kj * tk - first) // step])
        return carry

    lax.fori_loop(n_far, k_end, near, 0)

    lam = _lambda(lq1_ref, lk1_ref, lq2_ref, lk2_ref, lam_init)
    on = acc_scr[:hw, :] / acc_scr[hw:hw + 1, :]
    ot = on[:, :tq] - lam * on[:, tq:]
    ot = ot * lax.rsqrt(jnp.mean(ot * ot, axis=0, keepdims=True) + EPS)
    ot = ot * subln_ref[...] * (1.0 - lam_init)
    o_ref[0] = ot.T.astype(o_ref.dtype)


def prompt_attention(q, k, vt, bias, lq1, lk1, lq2, lk2, subln, *, tq, tk, lam_init):
    b, t, dk = q.shape
    hw = 2 * HEAD_DIM
    n_heads = dk // hw
    vrows = vt.shape[2]
    vec = lambda shape: pl.BlockSpec(shape, lambda bb, h, i: (0, 0))
    return pl.pallas_call(
        functools.partial(_attn_kernel, tq=tq, tk=tk, lam_init=lam_init),
        grid=(b, n_heads, t // tq),
        in_specs=[pl.BlockSpec((1, tq, hw), lambda bb, h, i: (bb, i, h)),
                  pl.BlockSpec((1, t, hw), lambda bb, h, i: (bb, 0, h)),
                  pl.BlockSpec((1, 1, vrows, t), lambda bb, h, i: (bb, h, 0, 0)),
                  pl.BlockSpec((1,) + bias.shape[1:], lambda bb, h, i: (h, 0, 0, 0)),
                  vec((1, HEAD_DIM)), vec((1, HEAD_DIM)), vec((1, HEAD_DIM)), vec((1, HEAD_DIM)),
                  vec((hw, 1))],
        out_specs=pl.BlockSpec((1, tq, hw), lambda bb, h, i: (bb, i, h)),
        out_shape=jax.ShapeDtypeStruct((b, t, dk), BF16),
        scratch_shapes=[pltpu.VMEM((2 * tq, hw), BF16),
                        pltpu.VMEM((1, 2 * tq), F32),
                        pltpu.VMEM((vrows, 2 * tq), F32)],
        name="prompt_attention",
        compiler_params=_cparams(("parallel", "parallel", "arbitrary")),
    )(q, k, vt, bias, lq1, lk1, lq2, lk2, subln)


def _sample_attn_kernel(pt_ref, *refs, n_pages, page, tq, n_heads, lam_init):
    k_refs = refs[:n_pages]
    v_refs = refs[n_pages:2 * n_pages]
    (q_ref, kn_ref, vn_ref, bias_ref, lq1_ref, lk1_ref, lq2_ref, lk2_ref, subln_ref,
     o_ref, kbf, vbf) = refs[2 * n_pages:]
    past = n_pages * page
    hw = 2 * HEAD_DIM
    rows = n_heads * 2 * tq
    dk = n_heads * hw

    @pl.when(pl.program_id(0) == 0)
    def _():
        kbf[past:, :] = jnp.zeros((page, dk), BF16)
        vbf[past:, :] = jnp.zeros((page, dk), BF16)

    for p in range(n_pages):
        kbf[p * page:(p + 1) * page, :] = k_refs[p][0].astype(BF16)
        vbf[p * page:(p + 1) * page, :] = v_refs[p][0].astype(BF16)
    nn = kn_ref.shape[1]
    kbf[past:past + nn, :] = kn_ref[0]
    vbf[past:past + nn, :] = vn_ref[0]

    qrep = jnp.concatenate([q_ref[0]] * n_heads, axis=0)
    rr = lax.broadcasted_iota(jnp.int32, (rows, dk), 0)
    cc = lax.broadcasted_iota(jnp.int32, (rows, dk), 1)
    qbd = jnp.where(cc // HEAD_DIM == rr // tq, qrep, 0.0).astype(BF16)

    s = _dot_nt(qbd, kbf[...])
    tail = s[:, past - page:] + bias_ref[...]
    m = jnp.maximum(jnp.max(s[:, :past - page], axis=1, keepdims=True),
                    jnp.max(tail, axis=1, keepdims=True))
    p_head = jnp.exp(s[:, :past - page] - m)
    p_tail = jnp.exp(tail - m)
    l = jnp.sum(p_head, axis=1, keepdims=True) + jnp.sum(p_tail, axis=1, keepdims=True)
    pv = (_dot(p_head.astype(BF16), vbf[:past - page, :])
          + _dot(p_tail.astype(BF16), vbf[past - page:, :]))
    pv = pv / l
    pv = jnp.where(cc // hw == rr // (2 * tq), pv, 0.0)
    o8 = pv[0:2 * tq]
    for h in range(1, n_heads):
        o8 = o8 + pv[h * 2 * tq:(h + 1) * 2 * tq]
    lam = _lambda(lq1_ref, lk1_ref, lq2_ref, lk2_ref, lam_init)
    o = o8[:tq] - lam * o8[tq:]
    outs = []
    for h in range(n_heads):
        outs.append(_rms(o[:, h * hw:(h + 1) * hw], subln_ref[...]) * (1.0 - lam_init))
    o_ref[0] = jnp.concatenate(outs, axis=1)


def sample_attention(page_table, cache_k, cache_v, q8, k_new, v_new, bias, lq1, lk1, lq2, lk2, subln,
                     *, tq, lam_init):
    n_seq, n_pages = page_table.shape
    _, page, dk = cache_k.shape
    hw = 2 * HEAD_DIM
    n_heads = dk // hw
    nn = k_new.shape[1]
    past = n_pages * page

    def page_spec(p):
        return pl.BlockSpec((1, page, dk), lambda b, pt: (pt[b * n_pages + p], 0, 0))

    vec = lambda shape: pl.BlockSpec(shape, lambda b, pt: (0, 0))
    in_specs = ([page_spec(p) for p in range(n_pages)] * 2
                + [pl.BlockSpec((1, 2 * tq, dk), lambda b, pt: (b, 0, 0)),
                   pl.BlockSpec((1, nn, dk), lambda b, pt: (b, 0, 0)),
                   pl.BlockSpec((1, nn, dk), lambda b, pt: (b, 0, 0)),
                   vec(bias.shape),
                   vec((1, HEAD_DIM)), vec((1, HEAD_DIM)), vec((1, HEAD_DIM)), vec((1, HEAD_DIM)),
                   vec((1, hw))])
    grid_spec = pltpu.PrefetchScalarGridSpec(
        num_scalar_prefetch=1,
        grid=(n_seq,),
        in_specs=in_specs,
        out_specs=pl.BlockSpec((1, tq, dk), lambda b, pt: (b, 0, 0)),
        scratch_shapes=[pltpu.VMEM((past + page, dk), BF16),
                        pltpu.VMEM((past + page, dk), BF16)])
    kern = functools.partial(_sample_attn_kernel, n_pages=n_pages, page=page, tq=tq,
                             n_heads=n_heads, lam_init=lam_init)
    return pl.pallas_call(
        kern,
        grid_spec=grid_spec,
        out_shape=jax.ShapeDtypeStruct((n_seq, tq, dk), F32),
        name="sample_attention",
        compiler_params=_cparams(("arbitrary",)),
    )(page_table.reshape(-1), *([cache_k] * n_pages), *([cache_v] * n_pages),
      q8, k_new, v_new, bias, lq1, lk1, lq2, lk2, subln)


def _pick_tile(n, target):
    t = min(n, target)
    while n % t:
        t //= 2
    return t


def _run_trunk(x, pos0, ts, rg_h0, rg_cb0, ffn_fb0, w, attn_fn):
    bk, r, d = x.shape
    depth = w['norm_mix'].shape[0]
    n_a = w['rg_w_in'].shape[0]
    tm_rg = _pick_tile(r, 256)
    tm_ffn = _pick_tile(r, 1024)
    tm_mm = _pick_tile(bk * r, 512)
    dk = w['attn_w_q'].shape[2]
    new_h, new_cb, new_fb = [], [], []
    k_new = v_new = kb = vb = None
    for l in range(depth):
        if l < n_a:
            x, h_last, cb = rglru_layer(
                x, rg_h0[l], rg_cb0[l], w['norm_mix'][l][None], w['rg_w_in'][l], w['rg_conv_w'][l],
                w['rg_conv_b'][l][None], w['rg_w_ax'][l], w['rg_b_a'][l][None], w['rg_b_x'][l][None],
                w['rg_lambda'][l][None], w['rg_w_out'][l], tm=tm_rg, ts=ts, pos0=pos0)
            new_h.append(h_last)
            new_cb.append(cb)
        else:
            x2 = x.reshape(bk * r, d)
            if l == n_a:
                k_new, v_new, kb, vb = norm_matmul(
                    x2, w['kv_norm'][None], w['w_kv'],
                    [(0, dk, 1.0), (dk, 2 * dk, 1.0), (0, dk, 1.0), (dk, 2 * dk, 1.0)],
                    [F32, F32, BF16, BF16], tm_mm)
            j = l - n_a
            q_dt = attn_fn.q_dtype
            (q,) = norm_matmul(x2, w['norm_mix'][l][None], w['attn_w_q'][j],
                               [(0, dk, ATTN_SCALE)], [q_dt], tm_mm)
            o = attn_fn(j, l, q, kb, vb)
            x = residual_matmul(x2, o, w['attn_w_o'][j], tm_mm).reshape(bk, r, d)
        last = l == depth - 1
        x, fb = ffn_layer(x, ffn_fb0[l], w['norm_ffn'][l][None], w['ffn_w_up'][l], w['ffn_conv_w'][l],
                          w['ffn_conv_b'][l][None], w['ffn_w_down'][l], w['final_norm'][None],
                          tm=tm_ffn, tn=512, ts=ts, final_norm=last)
        new_fb.append(fb)
    return x, new_h, new_cb, new_fb, k_new, v_new


def _lam_init(layer_idx):
    return 0.8 - 0.6 * math.exp(-0.3 * layer_idx)


def kernel(x_prompt, x_sample, state_rglru_h, state_rglru_conv, state_ffn_conv, cache_k, cache_v,
           page_table, rel_bias, norm_mix, norm_ffn, final_norm, rg_w_in, rg_conv_w, rg_conv_b,
           rg_w_a, rg_b_a, rg_w_x, rg_b_x, rg_lambda, rg_w_out, kv_norm, w_kv, attn_w_q,
           lam_q1, lam_k1, lam_q2, lam_k2, attn_subln, attn_w_o, ffn_w_up, ffn_conv_w,
           ffn_conv_b, ffn_w_down):
    bp, t_p, d = x_prompt.shape
    n_seq, tq, _ = x_sample.shape
    depth = norm_mix.shape[0]
    n_a = rg_w_in.shape[0]
    n_pool, page, n_heads, _, hd = cache_k.shape
    assert hd == HEAD_DIM
    dk = n_heads * 2 * hd
    f2 = ffn_w_up.shape[2]
    past_len = page_table.shape[1] * page
    assert page >= FAR_DISTANCE and past_len >= 2 * page and 2 * tq == V7X_SUBLANES

    w = {
        'norm_mix': norm_mix, 'norm_ffn': norm_ffn, 'final_norm': final_norm,
        'rg_w_in': rg_w_in.astype(BF16), 'rg_conv_w': rg_conv_w, 'rg_conv_b': rg_conv_b,
        'rg_w_ax': jnp.concatenate([rg_w_a, rg_w_x], axis=-1).astype(BF16),
        'rg_b_a': rg_b_a, 'rg_b_x': rg_b_x, 'rg_lambda': rg_lambda,
        'rg_w_out': rg_w_out.astype(BF16), 'kv_norm': kv_norm, 'w_kv': w_kv.astype(BF16),
        'attn_w_q': attn_w_q.astype(BF16), 'attn_w_o': attn_w_o.astype(BF16),
        'ffn_w_up': ffn_w_up.astype(BF16), 'ffn_conv_w': ffn_conv_w, 'ffn_conv_b': ffn_conv_b,
        'ffn_w_down': ffn_w_down.astype(BF16),
    }
    lam_vecs = lambda j: (lam_q1[j][None], lam_k1[j][None], lam_q2[j][None], lam_k2[j][None])

    tq_p = _pick_tile(t_p, 256)
    tk_p = _pick_tile(t_p, 512)
    bias_p = prompt_bias(rel_bias, tq_p, tk_p)

    vt_cache = {}

    def attn_prompt(j, l, q, kb, vb):
        if 'vt' not in vt_cache:
            vt = vb.reshape(bp, t_p, n_heads, 2 * hd).transpose(0, 2, 3, 1)
            extra = jnp.zeros((bp, n_heads, 16, t_p), BF16).at[:, :, 0, :].set(1.0)
            vt_cache['vt'] = jnp.concatenate([vt, extra], axis=2)
        o = prompt_attention(q.reshape(bp, t_p, dk), kb.reshape(bp, t_p, dk), vt_cache['vt'],
                             bias_p, *lam_vecs(j), attn_subln[j][:, None], tq=tq_p, tk=tk_p,
                             lam_init=_lam_init(l))
        return o.reshape(bp * t_p, dk)
    attn_prompt.q_dtype = BF16

    cw_rg = rg_conv_w.shape[1]
    cw_ffn = ffn_conv_w.shape[1]
    yp, hp, cbp, fbp, kp, vp = _run_trunk(
        x_prompt, 0, 1,
        jnp.zeros((n_a, bp, 1, d), F32), jnp.zeros((n_a, bp, cw_rg - 1, d), F32),
        jnp.zeros((depth, bp, cw_ffn - 1, f2), F32), w, attn_prompt)

    bias_s = sample_bias(rel_bias, tq, page)
    ck = cache_k.reshape(n_pool, page, dk)
    cv = cache_v.reshape(n_pool, page, dk)
    n_new = 16

    def to_seq_major(a2):
        return a2.reshape(tq, n_seq, -1).transpose(1, 0, 2)

    def attn_sample(j, l, q, kb, vb):
        q4 = to_seq_major(q)
        q8 = jnp.concatenate([q4, q4], axis=1)
        pad = ((0, 0), (0, n_new - tq), (0, 0))
        o = sample_attention(page_table, ck, cv, q8, jnp.pad(to_seq_major(kb), pad),
                             jnp.pad(to_seq_major(vb), pad), bias_s, *lam_vecs(j), attn_subln[j][None],
                             tq=tq, lam_init=_lam_init(l))
        return o.transpose(1, 0, 2).reshape(tq * n_seq, dk)
    attn_sample.q_dtype = F32

    def tmajor(a):
        return a.transpose(0, 2, 1, 3).reshape(a.shape[0], 1, a.shape[2] * n_seq, a.shape[3])

    def smajor(a2, steps):
        return a2.reshape(steps, n_seq, -1).transpose(1, 0, 2)

    xs = x_sample.transpose(1, 0, 2).reshape(1, tq * n_seq, d)
    ys, hs, cbs, fbs, ks, vs = _run_trunk(
        xs, past_len, n_seq, state_rglru_h[:, None], tmajor(state_rglru_conv), tmajor(state_ffn_conv),
        w, attn_sample)

    return (yp,
            smajor(ys, tq),
            jnp.stack([h[:, 0] for h in hp]),
            jnp.stack(cbp),
            jnp.stack(fbp),
            kp.reshape(bp, t_p, n_heads, 2, hd),
            vp.reshape(bp, t_p, n_heads, 2 * hd),
            jnp.stack([h[0] for h in hs]),
            jnp.stack([smajor(c, cw_rg - 1) for c in cbs]),
            jnp.stack([smajor(fb, cw_ffn - 1) for fb in fbs]),
            smajor(ks[None], tq).reshape(n_seq, tq, n_heads, 2, hd),
            smajor(vs[None], tq).reshape(n_seq, tq, n_heads, 2 * hd))
```

```python
import functools
import math

import jax
import jax.numpy as jnp
from jax import lax
from jax.experimental import pallas as pl
from jax.experimental.pallas import tpu as pltpu

EPS = 1e-6
RG_C = 8.0
HEAD_DIM = 64
N_BUCKETS = 32
MAX_DISTANCE = 128
NEG_INF = -1e30
ATTN_SCALE = HEAD_DIM ** -0.5
LOG2E = math.log2(math.e)

V7X_LANES = 128
V7X_SUBLANES = 8
V7X_VMEM_BYTES = 64 * 2 ** 20
VMEM_LIMIT = V7X_VMEM_BYTES - 8 * 2 ** 20

BF16 = jnp.bfloat16
F32 = jnp.float32


def _bucket_starts():
    max_exact = N_BUCKETS // 2
    starts = list(range(max_exact + 1))
    for b in range(max_exact + 1, N_BUCKETS):
        n = starts[-1]
        while True:
            n += 1
            large = max_exact + int(math.log(n / max_exact) / math.log(MAX_DISTANCE / max_exact)
                                    * (N_BUCKETS - max_exact))
            if min(large, N_BUCKETS - 1) >= b:
                break
        starts.append(n)
    return tuple(starts)


BUCKET_STARTS = _bucket_starts()
FAR_DISTANCE = BUCKET_STARTS[-1]


def _cparams(sem, vmem=VMEM_LIMIT):
    return pltpu.CompilerParams(dimension_semantics=sem, vmem_limit_bytes=vmem)


def _rms(x, g):
    return x * lax.rsqrt(jnp.mean(x * x, axis=-1, keepdims=True) + EPS) * g


def _dot(a, b):
    return jnp.dot(a, b, preferred_element_type=F32)


def _dot_nt(a, b):
    return lax.dot_general(a, b, (((1,), (1,)), ((), ())), preferred_element_type=F32)


def _norm_mm_kernel(x_ref, g_ref, w_ref, *out_refs, splits):
    xn = _rms(x_ref[...], g_ref[...]).astype(BF16)
    y = _dot(xn, w_ref[...])
    for o_ref, (lo, hi, scale, transposed) in zip(out_refs, splits):
        part = y[:, lo:hi]
        if scale != 1.0:
            part = part * scale
        if transposed:
            o_ref[0] = part.T.astype(o_ref.dtype)
        else:
            o_ref[...] = part.astype(o_ref.dtype)


def norm_matmul(x, g, w, splits, dtypes, tm, rows_per_seq=None):
    r, d = x.shape
    n = w.shape[1]
    out_shape, out_specs = [], []
    for (lo, hi, _, transposed), dt in zip(splits, dtypes):
        if transposed:
            nt = rows_per_seq // tm
            out_shape.append(jax.ShapeDtypeStruct((r // rows_per_seq, hi - lo, rows_per_seq), dt))
            out_specs.append(pl.BlockSpec((1, hi - lo, tm), lambda i, nt=nt: (i // nt, 0, i % nt)))
        else:
            out_shape.append(jax.ShapeDtypeStruct((r, hi - lo), dt))
            out_specs.append(pl.BlockSpec((tm, hi - lo), lambda i: (i, 0)))
    return pl.pallas_call(
        functools.partial(_norm_mm_kernel, splits=tuple(splits)),
        grid=(r // tm,),
        in_specs=[pl.BlockSpec((tm, d), lambda i: (i, 0)),
                  pl.BlockSpec((1, d), lambda i: (0, 0)),
                  pl.BlockSpec((d, n), lambda i: (0, 0))],
        out_specs=out_specs,
        out_shape=out_shape,
        name="norm_matmul",
        compiler_params=_cparams(("parallel",)),
    )(x, g, w)


def _res_mm_kernel(x_ref, a_ref, w_ref, o_ref):
    o_ref[...] = x_ref[...] + _dot(a_ref[...].astype(BF16), w_ref[...])


def residual_matmul(x, a, w, tm):
    r, d = x.shape
    k = a.shape[1]
    return pl.pallas_call(
        _res_mm_kernel,
        grid=(r // tm,),
        in_specs=[pl.BlockSpec((tm, d), lambda i: (i, 0)),
                  pl.BlockSpec((tm, k), lambda i: (i, 0)),
                  pl.BlockSpec((k, d), lambda i: (0, 0))],
        out_specs=pl.BlockSpec((tm, d), lambda i: (i, 0)),
        out_shape=jax.ShapeDtypeStruct((r, d), F32),
        name="residual_matmul",
        compiler_params=_cparams(("parallel",)),
    )(x, a, w)


def _rg_kernel(x_ref, h0_ref, cb0_ref, g_ref, win_ref, cw_ref, cbias_ref, wax_ref, ba_ref, bx_ref,
               lam_ref, wout_ref, out_ref, hlast_ref, cbout_ref,
               rbuf, a_scr, b_scr, h_scr, *, tm, ts, pos0, conv_w, n_blocks):
    ti = pl.program_id(1)
    d = a_scr.shape[1]
    hdr = rbuf.shape[0] - tm
    nprev = (conv_w - 1) * ts

    @pl.when(ti == 0)
    def _():
        rbuf[hdr - nprev:hdr, :] = cb0_ref[0]
        h_scr[...] = h0_ref[0]

    x = x_ref[0]
    xn = _rms(x, g_ref[...]).astype(BF16)
    u = _dot(xn, win_ref[...])
    gate = u[:, :d]
    rbuf[hdr:hdr + tm, :] = u[:, d:]
    xc = cbias_ref[...] + rbuf[hdr:hdr + tm, :] * cw_ref[conv_w - 1:conv_w, :]
    for j in range(conv_w - 1):
        off = hdr - (conv_w - 1 - j) * ts
        xc = xc + rbuf[off:off + tm, :] * cw_ref[j:j + 1, :]
    new_prev = rbuf[hdr + tm - nprev:hdr + tm, :]
    cbout_ref[0] = new_prev
    rbuf[hdr - nprev:hdr, :] = new_prev

    xcb = xc.astype(BF16)
    blk = d // n_blocks
    ras, rxs = [], []
    for n in range(n_blocks):
        rr = _dot(xcb[:, n * blk:(n + 1) * blk], wax_ref[n])
        ras.append(rr[:, :blk])
        rxs.append(rr[:, blk:])
    r = jax.nn.sigmoid(jnp.concatenate(ras, axis=1) + ba_ref[...])
    i = jax.nn.sigmoid(jnp.concatenate(rxs, axis=1) + bx_ref[...])
    z = -lam_ref[...]
    softplus = jnp.maximum(z, 0.0) + jnp.log(1.0 + jnp.exp(-jnp.abs(z)))
    log_a = (-RG_C) * r * softplus
    a = jnp.exp(log_a)
    mult = jnp.sqrt(1.0 - a * a)
    if pos0 == 0:
        row = lax.broadcasted_iota(jnp.int32, (tm, 1), 0) + ti * tm
        reset = row < ts
        a = jnp.where(reset, 0.0, a)
        mult = jnp.where(reset, 1.0, mult)
    a_scr[...] = a
    b_scr[...] = mult * i * xc

    if ts == 1:
        def group(gi, h):
            base = pl.multiple_of(gi * V7X_SUBLANES, V7X_SUBLANES)
            for rr_ in range(V7X_SUBLANES):
                h = a_scr[pl.ds(base + rr_, 1), :] * h + b_scr[pl.ds(base + rr_, 1), :]
                b_scr[pl.ds(base + rr_, 1), :] = h
            return h
        h = lax.fori_loop(0, tm // V7X_SUBLANES, group, h_scr[...])
    else:
        h = h_scr[...]
        for t in range(tm // ts):
            h = a_scr[t * ts:(t + 1) * ts, :] * h + b_scr[t * ts:(t + 1) * ts, :]
            b_scr[t * ts:(t + 1) * ts, :] = h
    h_scr[...] = h
    hlast_ref[0] = h

    hg = (b_scr[...] * jax.nn.gelu(gate)).astype(BF16)
    out_ref[0] = x + _dot(hg, wout_ref[...])


def rglru_layer(x, h0, cb0, g, w_in, conv_w, conv_b, w_ax, b_a, b_x, lam, w_out, *, tm, ts, pos0):
    bk, r, d = x.shape
    cw = conv_w.shape[0]
    nprev = (cw - 1) * ts
    hdr = max(V7X_SUBLANES, nprev)
    n_blocks = w_ax.shape[0]
    const2 = lambda b, t: (0, 0)
    kern = functools.partial(_rg_kernel, tm=tm, ts=ts, pos0=pos0, conv_w=cw, n_blocks=n_blocks)
    return pl.pallas_call(
        kern,
        grid=(bk, r // tm),
        in_specs=[pl.BlockSpec((1, tm, d), lambda b, t: (b, t, 0)),
                  pl.BlockSpec((1, ts, d), lambda b, t: (b, 0, 0)),
                  pl.BlockSpec((1, nprev, d), lambda b, t: (b, 0, 0)),
                  pl.BlockSpec((1, d), const2),
                  pl.BlockSpec((d, 2 * d), const2),
                  pl.BlockSpec((cw, d), const2),
                  pl.BlockSpec((1, d), const2),
                  pl.BlockSpec(w_ax.shape, lambda b, t: (0, 0, 0)),
                  pl.BlockSpec((1, d), const2),
                  pl.BlockSpec((1, d), const2),
                  pl.BlockSpec((1, d), const2),
                  pl.BlockSpec((d, d), const2)],
        out_specs=[pl.BlockSpec((1, tm, d), lambda b, t: (b, t, 0)),
                   pl.BlockSpec((1, ts, d), lambda b, t: (b, 0, 0)),
                   pl.BlockSpec((1, nprev, d), lambda b, t: (b, 0, 0))],
        out_shape=[jax.ShapeDtypeStruct((bk, r, d), F32),
                   jax.ShapeDtypeStruct((bk, ts, d), F32),
                   jax.ShapeDtypeStruct((bk, nprev, d), F32)],
        scratch_shapes=[pltpu.VMEM((hdr + tm, d), F32),
                        pltpu.VMEM((tm, d), F32),
                        pltpu.VMEM((tm, d), F32),
                        pltpu.VMEM((ts, d), F32)],
        name="rglru_layer",
        compiler_params=_cparams(("parallel", "arbitrary")),
    )(x, h0, cb0, g, w_in, conv_w, conv_b, w_ax, b_a, b_x, lam, w_out)


def _ffn_kernel(x_ref, fbv0_ref, fbg0_ref, g_ref, wv_ref, wg_ref, cwv_ref, cwg_ref, cbv_ref, cbg_ref,
                wd_ref, gfin_ref, out_ref, fbv_ref, fbg_ref,
                xn_scr, acc_scr, ubv, ubg, carry_v, carry_g, *, tm, ts, conv_w, final_norm):
    ti = pl.program_id(1)
    j = pl.program_id(2)
    nj = pl.num_programs(2)
    hdr = ubv.shape[0] - tm
    nprev = (conv_w - 1) * ts

    @pl.when(j == 0)
    def _():
        xn_scr[...] = _rms(x_ref[0], g_ref[...]).astype(BF16)
        acc_scr[...] = jnp.zeros_like(acc_scr)

    xn = xn_scr[...]

    def conv(w_ref, ub, carry, fb0_ref, cw_ref, cb_ref, fb_ref):
        @pl.when(ti == 0)
        def _():
            ub[hdr - nprev:hdr, :] = fb0_ref[0]

        @pl.when(ti > 0)
        def _():
            ub[hdr - nprev:hdr, :] = carry[j]

        ub[hdr:hdr + tm, :] = _dot(xn, w_ref[...])
        uc = cb_ref[...] + ub[hdr:hdr + tm, :] * cw_ref[conv_w - 1:conv_w, :]
        for jj in range(conv_w - 1):
            off = hdr - (conv_w - 1 - jj) * ts
            uc = uc + ub[off:off + tm, :] * cw_ref[jj:jj + 1, :]
        new_prev = ub[hdr + tm - nprev:hdr + tm, :]
        carry[j] = new_prev
        fb_ref[0, 0] = new_prev
        return uc

    val = conv(wv_ref, ubv, carry_v, fbv0_ref, cwv_ref, cbv_ref, fbv_ref)
    gte = conv(wg_ref, ubg, carry_g, fbg0_ref, cwg_ref, cbg_ref, fbg_ref)
    hmid = (jax.nn.gelu(gte) * val).astype(BF16)
    acc_scr[...] += _dot(hmid, wd_ref[...])

    @pl.when(j == nj - 1)
    def _():
        y = x_ref[0] + acc_scr[...]
        if final_norm:
            y = _rms(y, gfin_ref[...])
        out_ref[0] = y


def ffn_layer(x, fb0, g, w_up, conv_w, conv_b, w_down, g_final, *, tm, tn, ts, final_norm):
    bk, r, d = x.shape
    f = w_down.shape[0]
    cw = conv_w.shape[0]
    nprev = (cw - 1) * ts
    hdr = max(V7X_SUBLANES, nprev)
    nc = f // tn
    const2 = lambda b, t, j: (0, 0)
    kern = functools.partial(_ffn_kernel, tm=tm, ts=ts, conv_w=cw, final_norm=final_norm)
    out, fbv, fbg = pl.pallas_call(
        kern,
        grid=(bk, r // tm, nc),
        in_specs=[pl.BlockSpec((1, tm, d), lambda b, t, j: (b, t, 0)),
                  pl.BlockSpec((1, nprev, tn), lambda b, t, j: (b, 0, j)),
                  pl.BlockSpec((1, nprev, tn), lambda b, t, j: (b, 0, nc + j)),
                  pl.BlockSpec((1, d), const2),
                  pl.BlockSpec((d, tn), lambda b, t, j: (0, j)),
                  pl.BlockSpec((d, tn), lambda b, t, j: (0, nc + j)),
                  pl.BlockSpec((cw, tn), lambda b, t, j: (0, j)),
                  pl.BlockSpec((cw, tn), lambda b, t, j: (0, nc + j)),
                  pl.BlockSpec((1, tn), lambda b, t, j: (0, j)),
                  pl.BlockSpec((1, tn), lambda b, t, j: (0, nc + j)),
                  pl.BlockSpec((tn, d), lambda b, t, j: (j, 0)),
                  pl.BlockSpec((1, d), const2)],
        out_specs=[pl.BlockSpec((1, tm, d), lambda b, t, j: (b, t, 0)),
                   pl.BlockSpec((1, 1, nprev, tn), lambda b, t, j: (b, t, 0, j)),
                   pl.BlockSpec((1, 1, nprev, tn), lambda b, t, j: (b, t, 0, j))],
        out_shape=[jax.ShapeDtypeStruct((bk, r, d), F32),
                   jax.ShapeDtypeStruct((bk, r // tm, nprev, f), F32),
                   jax.ShapeDtypeStruct((bk, r // tm, nprev, f), F32)],
        scratch_shapes=[pltpu.VMEM((tm, d), BF16),
                        pltpu.VMEM((tm, d), F32),
                        pltpu.VMEM((hdr + tm, tn), F32),
                        pltpu.VMEM((hdr + tm, tn), F32),
                        pltpu.VMEM((nc, nprev, tn), F32),
                        pltpu.VMEM((nc, nprev, tn), F32)],
        name="ffn_layer",
        compiler_params=_cparams(("parallel", "arbitrary", "arbitrary")),
    )(x, fb0, fb0, g, w_up, w_up, conv_w, conv_w, conv_b, conv_b, w_down, g_final)
    return out, jnp.concatenate([fbv[:, -1], fbg[:, -1]], axis=-1)


def _shifted_bias(rel, rb_ref, h, scale=1.0):
    last = rb_ref[N_BUCKETS - 1, h]
    val = jnp.zeros(rel.shape, F32)
    for b in range(N_BUCKETS - 2, -1, -1):
        val = jnp.where(rel < BUCKET_STARTS[b + 1], (rb_ref[b, h] - last) * scale, val)
    return jnp.where(rel >= 0, val, NEG_INF)


def _near_offsets(tq, tk):
    step = math.gcd(tq, tk)
    lo = -(tq - 1)
    hi = tk - 1 + FAR_DISTANCE - 1
    first = -((-lo) // step) * step
    if first < lo:
        first += step
    count = (hi - first) // step + 1
    return first, step, count


def _prompt_bias_kernel(rb_ref, o_ref, *, tq, tk, first, step, count):
    h = pl.program_id(0)
    o = pl.program_id(1)
    rel = (lax.broadcasted_iota(jnp.int32, (tk, tq), 1)
           - lax.broadcasted_iota(jnp.int32, (tk, tq), 0) + (first + o * step))
    o_ref[0, 0] = jnp.where(o < count, _shifted_bias(rel, rb_ref, h, LOG2E), 0.0)


def prompt_bias(rel_bias, tq, tk):
    n_heads = rel_bias.shape[1]
    first, step, count = _near_offsets(tq, tk)
    return pl.pallas_call(
        functools.partial(_prompt_bias_kernel, tq=tq, tk=tk, first=first, step=step, count=count),
        grid=(n_heads, count + 1),
        in_specs=[pl.BlockSpec(memory_space=pltpu.SMEM)],
        out_specs=pl.BlockSpec((1, 1, tk, tq), lambda h, o: (h, o, 0, 0)),
        out_shape=jax.ShapeDtypeStruct((n_heads, count + 1, tk, tq), F32),
        name="prompt_bias",
        compiler_params=_cparams(("parallel", "parallel")),
    )(rel_bias)


def _sample_bias_kernel(rb_ref, o_ref, *, tq, page, n_heads):
    rows = 2 * tq
    row = lax.broadcasted_iota(jnp.int32, (rows, 2 * page), 0)
    col = lax.broadcasted_iota(jnp.int32, (rows, 2 * page), 1)
    t = row % tq
    rel = jnp.where(col < page, page + t - col, t - (col - page))
    for h in range(n_heads):
        o_ref[h * rows:(h + 1) * rows, :] = _shifted_bias(rel, rb_ref, h)


def sample_bias(rel_bias, tq, page):
    n_heads = rel_bias.shape[1]
    return pl.pallas_call(
        functools.partial(_sample_bias_kernel, tq=tq, page=page, n_heads=n_heads),
        in_specs=[pl.BlockSpec(memory_space=pltpu.SMEM)],
        out_specs=pl.BlockSpec(memory_space=pltpu.VMEM),
        out_shape=jax.ShapeDtypeStruct((n_heads * 2 * tq, 2 * page), F32),
        name="sample_bias",
    )(rel_bias)


def _lambda(lq1_ref, lk1_ref, lq2_ref, lk2_ref, lam_init):
    s1 = jnp.sum(lq1_ref[...] * lk1_ref[...], axis=1, keepdims=True)
    s2 = jnp.sum(lq2_ref[...] * lk2_ref[...], axis=1, keepdims=True)
    return jnp.exp(s1) - jnp.exp(s2) + lam_init


def _attn_kernel(q_ref, k_ref, vt_ref, bias_ref, lq1_ref, lk1_ref, lq2_ref, lk2_ref, subln_ref, o_ref,
                 qs_scr, m_scr, acc_scr, s_scr, p_scr, al_scr, *, tq, tk, tail_steps, lam_init):
    qi = pl.program_id(2)
    hw = 2 * HEAD_DIM
    q = q_ref[0]
    lane = lax.broadcasted_iota(jnp.int32, q.shape, 1)
    qs_scr[:tq, :] = jnp.where(lane < HEAD_DIM, q, jnp.zeros_like(q))
    qs_scr[tq:, :] = jnp.where(lane >= HEAD_DIM, q, jnp.zeros_like(q))
    m_scr[...] = jnp.full_like(m_scr, NEG_INF)
    acc_scr[...] = jnp.zeros_like(acc_scr)

    def scores(kj):
        start = pl.multiple_of(kj * tk, tk)
        k = k_ref[0, pl.ds(start, tk), :]
        return _dot_nt(k, qs_scr[...])

    first, step, count = _near_offsets(tq, tk)
    hi = first + (count - 1) * step
    n_far = jnp.maximum(qi * tq - hi + tk - 1, 0) // tk
    k_end = (qi * tq + tq - 1) // tk + 1

    last = k_end - 1

    def stage_scores(kj, slot):
        s_scr[slot] = scores(jnp.minimum(kj, last))

    def stage_probs(kj, slot, known_far):
        s = s_scr[slot]
        if not known_far:
            idx = jnp.where(kj < n_far, count, (qi * tq - kj * tk - first) // step)
            bias = bias_ref[0, idx]
            s = s + jnp.concatenate([bias, bias], axis=1)
        m_prev = m_scr[...]
        m_new = jnp.maximum(m_prev, jnp.max(s, axis=0, keepdims=True))
        m_scr[...] = m_new
        p_scr[slot] = jnp.exp2(s - m_new).astype(BF16)
        al_scr[slot] = jnp.exp2(m_prev - m_new)

    def stage_values(kj, slot):
        start = pl.multiple_of(kj * tk, tk)
        vt = vt_ref[0, 0, :, pl.ds(start, tk)]
        acc_scr[...] = al_scr[slot] * acc_scr[...] + _dot(vt, p_scr[slot])

    def pipe_step(i, slot, known_far):
        stage_scores(i + 2, slot)
        stage_probs(i + 1, 1 - slot, known_far)
        stage_values(i, slot)

    stage_scores(0, 0)
    stage_probs(0, 0, False)
    stage_scores(1, 1)

    def far_pair(ii, carry):
        pipe_step(2 * ii, 0, True)
        pipe_step(2 * ii + 1, 1, True)
        return carry

    n_far_pairs = jnp.maximum(n_far - 1, 0) // 2
    lax.fori_loop(0, n_far_pairs, far_pair, 0)

    for r in range(tail_steps):
        i = 2 * n_far_pairs + r

        @pl.when(i < last)
        def _(i=i, r=r):
            pipe_step(i, r % 2, False)

    @pl.when(last % 2 == 1)
    def _():
        stage_values(last, 1)

    @pl.when(last % 2 == 0)
    def _():
        stage_values(last, 0)

    lam = _lambda(lq1_ref, lk1_ref, lq2_ref, lk2_ref, lam_init)
    on = acc_scr[:hw, :] / acc_scr[hw:hw + 1, :]
    ot = on[:, :tq] - lam * on[:, tq:]
    ot = ot * lax.rsqrt(jnp.mean(ot * ot, axis=0, keepdims=True) + EPS)
    ot = ot * subln_ref[...] * (1.0 - lam_init)
    o_ref[0] = ot.T.astype(o_ref.dtype)


def _tail_steps(tq, tk, nq):
    first, step, count = _near_offsets(tq, tk)
    hi = first + (count - 1) * step
    most = 0
    for qi in range(nq):
        n_far = max(qi * tq - hi + tk - 1, 0) // tk
        last = (qi * tq + tq - 1) // tk
        most = max(most, last - 2 * (max(n_far - 1, 0) // 2))
    return most


def prompt_attention(q, k, vt, bias, lq1, lk1, lq2, lk2, subln, *, tq, tk, lam_init):
    b, t, dk = q.shape
    hw = 2 * HEAD_DIM
    n_heads = dk // hw
    vrows = vt.shape[2]
    vec = lambda shape: pl.BlockSpec(shape, lambda bb, h, i: (0, 0))
    return pl.pallas_call(
        functools.partial(_attn_kernel, tq=tq, tk=tk, tail_steps=_tail_steps(tq, tk, t // tq),
                          lam_init=lam_init),
        grid=(b, n_heads, t // tq),
        in_specs=[pl.BlockSpec((1, tq, hw), lambda bb, h, i: (bb, i, h)),
                  pl.BlockSpec((1, t, hw), lambda bb, h, i: (bb, 0, h)),
                  pl.BlockSpec((1, 1, vrows, t), lambda bb, h, i: (bb, h, 0, 0)),
                  pl.BlockSpec((1,) + bias.shape[1:], lambda bb, h, i: (h, 0, 0, 0)),
                  vec((1, HEAD_DIM)), vec((1, HEAD_DIM)), vec((1, HEAD_DIM)), vec((1, HEAD_DIM)),
                  vec((hw, 1))],
        out_specs=pl.BlockSpec((1, tq, hw), lambda bb, h, i: (bb, i, h)),
        out_shape=jax.ShapeDtypeStruct((b, t, dk), BF16),
        scratch_shapes=[pltpu.VMEM((2 * tq, hw), BF16),
                        pltpu.VMEM((1, 2 * tq), F32),
                        pltpu.VMEM((vrows, 2 * tq), F32),
                        pltpu.VMEM((2, tk, 2 * tq), F32),
                        pltpu.VMEM((2, tk, 2 * tq), BF16),
                        pltpu.VMEM((2, 1, 2 * tq), F32)],
        name="prompt_attention",
        compiler_params=_cparams(("parallel", "parallel", "arbitrary")),
    )(q, k, vt, bias, lq1, lk1, lq2, lk2, subln)


def _sample_attn_kernel(pt_ref, *refs, n_pages, page, tq, n_heads, lam_init):
    k_refs = refs[:n_pages]
    v_refs = refs[n_pages:2 * n_pages]
    (q_ref, kn_ref, vn_ref, bias_ref, lq1_ref, lk1_ref, lq2_ref, lk2_ref, subln_ref,
     o_ref, kt_scr, kn_scr, vbf) = refs[2 * n_pages:]
    past = n_pages * page
    hw = 2 * HEAD_DIM
    rows = n_heads * 2 * tq
    dk = n_heads * hw
    nn = kn_ref.shape[1]

    @pl.when(pl.program_id(0) == 0)
    def _():
        kn_scr[...] = jnp.zeros_like(kn_scr)
        for h in range(n_heads):
            vbf[h, past:, :] = jnp.zeros((page, hw), BF16)

    for p in range(n_pages):
        kt_scr[:, p * page:(p + 1) * page] = k_refs[p][0].astype(BF16)
        for h in range(n_heads):
            vbf[h, p * page:(p + 1) * page, :] = (
                v_refs[p][0, pl.ds(h, page, stride=n_heads), :].astype(BF16))
    kn_scr[:nn, :] = kn_ref[0]
    vn = vn_ref[0]
    for h in range(n_heads):
        vbf[h, past:past + nn, :] = vn[:, h * hw:(h + 1) * hw]

    qrep = jnp.concatenate([q_ref[0]] * n_heads, axis=0)
    rr = lax.broadcasted_iota(jnp.int32, (rows, dk), 0)
    cc = lax.broadcasted_iota(jnp.int32, (rows, dk), 1)
    qbd = jnp.where(cc // HEAD_DIM == rr // tq, qrep, 0.0).astype(BF16)

    s_main = _dot(qbd, kt_scr[:, :past - page])
    s_last = _dot(qbd, kt_scr[:, past - page:]) + bias_ref[:, :page]
    s_new = _dot_nt(qbd, kn_scr[...]) + bias_ref[:, page:]
    m = jnp.maximum(jnp.max(s_main, axis=1, keepdims=True),
                    jnp.max(jnp.maximum(s_last, s_new), axis=1, keepdims=True))
    p_all = jnp.concatenate([jnp.exp(s_main - m), jnp.exp(s_last - m), jnp.exp(s_new - m)], axis=1)
    l = jnp.sum(p_all, axis=1, keepdims=True)
    p_bf = p_all.astype(BF16)
    lam = _lambda(lq1_ref, lk1_ref, lq2_ref, lk2_ref, lam_init)
    outs = []
    for h in range(n_heads):
        r0 = h * 2 * tq
        o8 = _dot(p_bf[r0:r0 + 2 * tq], vbf[h]) / l[r0:r0 + 2 * tq]
        o = o8[:tq] - lam * o8[tq:]
        outs.append(_rms(o, subln_ref[...]) * (1.0 - lam_init))
    o_ref[0] = jnp.concatenate(outs, axis=1)


def sample_attention(page_table, cache_kt, cache_v, q8, k_new, v_new, bias, lq1, lk1, lq2, lk2, subln,
                     *, tq, lam_init):
    n_seq, n_pages = page_table.shape
    _, dk, page = cache_kt.shape
    hw = 2 * HEAD_DIM
    n_heads = dk // hw
    nn = k_new.shape[1]
    past = n_pages * page

    def page_spec(p, shape):
        return pl.BlockSpec((1,) + shape, lambda b, pt: (pt[b * n_pages + p], 0, 0))

    vec = lambda shape: pl.BlockSpec(shape, lambda b, pt: (0, 0))
    in_specs = ([page_spec(p, (dk, page)) for p in range(n_pages)]
                + [page_spec(p, (page * n_heads, hw)) for p in range(n_pages)]
                + [pl.BlockSpec((1, 2 * tq, dk), lambda b, pt: (b, 0, 0)),
                   pl.BlockSpec((1, nn, dk), lambda b, pt: (b, 0, 0)),
                   pl.BlockSpec((1, nn, dk), lambda b, pt: (b, 0, 0)),
                   vec(bias.shape),
                   vec((1, HEAD_DIM)), vec((1, HEAD_DIM)), vec((1, HEAD_DIM)), vec((1, HEAD_DIM)),
                   vec((1, hw))])
    grid_spec = pltpu.PrefetchScalarGridSpec(
        num_scalar_prefetch=1,
        grid=(n_seq,),
        in_specs=in_specs,
        out_specs=pl.BlockSpec((1, tq, dk), lambda b, pt: (b, 0, 0)),
        scratch_shapes=[pltpu.VMEM((dk, past), BF16),
                        pltpu.VMEM((page, dk), BF16),
                        pltpu.VMEM((n_heads, past + page, hw), BF16)])
    kern = functools.partial(_sample_attn_kernel, n_pages=n_pages, page=page, tq=tq,
                             n_heads=n_heads, lam_init=lam_init)
    return pl.pallas_call(
        kern,
        grid_spec=grid_spec,
        out_shape=jax.ShapeDtypeStruct((n_seq, tq, dk), F32),
        name="sample_attention",
        compiler_params=_cparams(("arbitrary",)),
    )(page_table.reshape(-1), *([cache_kt] * n_pages), *([cache_v] * n_pages),
      q8, k_new, v_new, bias, lq1, lk1, lq2, lk2, subln)


def _pick_tile(n, target):
    t = min(n, target)
    while n % t:
        t //= 2
    return t


def _run_trunk(x, pos0, ts, rg_h0, rg_cb0, ffn_fb0, w, attn_fn):
    bk, r, d = x.shape
    depth = w['norm_mix'].shape[0]
    n_a = w['rg_w_in'].shape[0]
    tm_rg = _pick_tile(r, 256)
    tm_ffn = _pick_tile(r, 1024)
    tm_mm = _pick_tile(bk * r, 512)
    dk = w['attn_w_q'].shape[2]
    new_h, new_cb, new_fb = [], [], []
    k_new = v_new = kb = vb = None
    for l in range(depth):
        if l < n_a:
            x, h_last, cb = rglru_layer(
                x, rg_h0[l], rg_cb0[l], w['norm_mix'][l][None], w['rg_w_in'][l], w['rg_conv_w'][l],
                w['rg_conv_b'][l][None], w['rg_w_ax'][l], w['rg_b_a'][l][None], w['rg_b_x'][l][None],
                w['rg_lambda'][l][None], w['rg_w_out'][l], tm=tm_rg, ts=ts, pos0=pos0)
            new_h.append(h_last)
            new_cb.append(cb)
        else:
            x2 = x.reshape(bk * r, d)
            if l == n_a:
                k_new, v_new, kb, vb = norm_matmul(
                    x2, w['kv_norm'][None], w['w_kv'],
                    [(0, dk, 1.0, True), (dk, 2 * dk, 1.0, False), (0, dk, 1.0, False),
                     (dk, 2 * dk, 1.0, False)],
                    [F32, F32, BF16, BF16], tm_mm, rows_per_seq=r)
            j = l - n_a
            q_dt = attn_fn.q_dtype
            (q,) = norm_matmul(x2, w['norm_mix'][l][None], w['attn_w_q'][j],
                               [(0, dk, attn_fn.q_scale, False)], [q_dt], tm_mm)
            o = attn_fn(j, l, q, kb, vb)
            x = residual_matmul(x2, o, w['attn_w_o'][j], tm_mm).reshape(bk, r, d)
        last = l == depth - 1
        x, fb = ffn_layer(x, ffn_fb0[l], w['norm_ffn'][l][None], w['ffn_w_up'][l], w['ffn_conv_w'][l],
                          w['ffn_conv_b'][l][None], w['ffn_w_down'][l], w['final_norm'][None],
                          tm=tm_ffn, tn=512, ts=ts, final_norm=last)
        new_fb.append(fb)
    return x, new_h, new_cb, new_fb, k_new, v_new


def _lam_init(layer_idx):
    return 0.8 - 0.6 * math.exp(-0.3 * layer_idx)


def kernel(x_prompt, x_sample, state_rglru_h, state_rglru_conv, state_ffn_conv, cache_k, cache_v,
           page_table, rel_bias, norm_mix, norm_ffn, final_norm, rg_w_in, rg_conv_w, rg_conv_b,
           rg_w_a, rg_b_a, rg_w_x, rg_b_x, rg_lambda, rg_w_out, kv_norm, w_kv, attn_w_q,
           lam_q1, lam_k1, lam_q2, lam_k2, attn_subln, attn_w_o, ffn_w_up, ffn_conv_w,
           ffn_conv_b, ffn_w_down):
    bp, t_p, d = x_prompt.shape
    n_seq, tq, _ = x_sample.shape
    depth = norm_mix.shape[0]
    n_a = rg_w_in.shape[0]
    n_pool, page, n_heads, _, hd = cache_k.shape
    assert hd == HEAD_DIM
    dk = n_heads * 2 * hd
    f2 = ffn_w_up.shape[2]
    past_len = page_table.shape[1] * page
    assert page >= FAR_DISTANCE and past_len >= 2 * page and 2 * tq == V7X_SUBLANES

    w = {
        'norm_mix': norm_mix, 'norm_ffn': norm_ffn, 'final_norm': final_norm,
        'rg_w_in': rg_w_in.astype(BF16), 'rg_conv_w': rg_conv_w, 'rg_conv_b': rg_conv_b,
        'rg_w_ax': jnp.concatenate([rg_w_a, rg_w_x], axis=-1).astype(BF16),
        'rg_b_a': rg_b_a, 'rg_b_x': rg_b_x, 'rg_lambda': rg_lambda,
        'rg_w_out': rg_w_out.astype(BF16), 'kv_norm': kv_norm, 'w_kv': w_kv.astype(BF16),
        'attn_w_q': attn_w_q.astype(BF16), 'attn_w_o': attn_w_o.astype(BF16),
        'ffn_w_up': ffn_w_up.astype(BF16), 'ffn_conv_w': ffn_conv_w, 'ffn_conv_b': ffn_conv_b,
        'ffn_w_down': ffn_w_down.astype(BF16),
    }
    lam_vecs = lambda j: (lam_q1[j][None], lam_k1[j][None], lam_q2[j][None], lam_k2[j][None])

    tq_p = _pick_tile(t_p, 512)
    tk_p = _pick_tile(t_p, 512)
    bias_p = prompt_bias(rel_bias, tq_p, tk_p)

    vt_cache = {}

    def attn_prompt(j, l, q, kb, vb):
        if 'vt' not in vt_cache:
            vt = vb.reshape(bp, t_p, n_heads, 2 * hd).transpose(0, 2, 3, 1)
            extra = jnp.zeros((bp, n_heads, 16, t_p), BF16).at[:, :, 0, :].set(1.0)
            vt_cache['vt'] = jnp.concatenate([vt, extra], axis=2)
        o = prompt_attention(q.reshape(bp, t_p, dk), kb.reshape(bp, t_p, dk), vt_cache['vt'],
                             bias_p, *lam_vecs(j), attn_subln[j][:, None], tq=tq_p, tk=tk_p,
                             lam_init=_lam_init(l))
        return o.reshape(bp * t_p, dk)
    attn_prompt.q_dtype = BF16
    attn_prompt.q_scale = ATTN_SCALE * LOG2E

    cw_rg = rg_conv_w.shape[1]
    cw_ffn = ffn_conv_w.shape[1]
    yp, hp, cbp, fbp, kp, vp = _run_trunk(
        x_prompt, 0, 1,
        jnp.zeros((n_a, bp, 1, d), F32), jnp.zeros((n_a, bp, cw_rg - 1, d), F32),
        jnp.zeros((depth, bp, cw_ffn - 1, f2), F32), w, attn_prompt)

    bias_s = sample_bias(rel_bias, tq, page)
    ck = cache_k.transpose(0, 2, 3, 4, 1).reshape(n_pool, dk, page)
    cv = cache_v.reshape(n_pool, page * n_heads, 2 * hd)
    n_new = 16

    def to_seq_major(a2):
        return a2.reshape(tq, n_seq, -1).transpose(1, 0, 2)

    def attn_sample(j, l, q, kb, vb):
        q4 = to_seq_major(q)
        q8 = jnp.concatenate([q4, q4], axis=1)
        pad = ((0, 0), (0, n_new - tq), (0, 0))
        o = sample_attention(page_table, ck, cv, q8, jnp.pad(to_seq_major(kb), pad),
                             jnp.pad(to_seq_major(vb), pad), bias_s, *lam_vecs(j), attn_subln[j][None],
                             tq=tq, lam_init=_lam_init(l))
        return o.transpose(1, 0, 2).reshape(tq * n_seq, dk)
    attn_sample.q_dtype = F32
    attn_sample.q_scale = ATTN_SCALE

    def tmajor(a):
        return a.transpose(0, 2, 1, 3).reshape(a.shape[0], 1, a.shape[2] * n_seq, a.shape[3])

    def smajor(a2, steps):
        return a2.reshape(steps, n_seq, -1).transpose(1, 0, 2)

    xs = x_sample.transpose(1, 0, 2).reshape(1, tq * n_seq, d)
    ys, hs, cbs, fbs, ks, vs = _run_trunk(
        xs, past_len, n_seq, state_rglru_h[:, None], tmajor(state_rglru_conv), tmajor(state_ffn_conv),
        w, attn_sample)

    return (yp,
            smajor(ys, tq),
            jnp.stack([h[:, 0] for h in hp]),
            jnp.stack(cbp),
            jnp.stack(fbp),
            kp.reshape(bp, n_heads, 2, hd, t_p).transpose(0, 4, 1, 2, 3),
            vp.reshape(bp, t_p, n_heads, 2 * hd),
            jnp.stack([h[0] for h in hs]),
            jnp.stack([smajor(c, cw_rg - 1) for c in cbs]),
            jnp.stack([smajor(fb, cw_ffn - 1) for fb in fbs]),
            ks.reshape(n_heads, 2, hd, tq, n_seq).transpose(4, 3, 0, 1, 2),
            smajor(vs[None], tq).reshape(n_seq, tq, n_heads, 2 * hd))
```

```python
import functools
import math

import jax
import jax.numpy as jnp
from jax import lax
from jax.experimental import pallas as pl
from jax.experimental.pallas import tpu as pltpu

EPS = 1e-6
RG_C = 8.0
HEAD_DIM = 64
N_BUCKETS = 32
MAX_DISTANCE = 128
NEG_INF = -1e30
ATTN_SCALE = HEAD_DIM ** -0.5
LOG2E = math.log2(math.e)

V7X_LANES = 128
V7X_SUBLANES = 8
V7X_VMEM_BYTES = 64 * 2 ** 20
VMEM_LIMIT = V7X_VMEM_BYTES - 8 * 2 ** 20

BF16 = jnp.bfloat16
F32 = jnp.float32


def _bucket_starts():
    max_exact = N_BUCKETS // 2
    starts = list(range(max_exact + 1))
    for b in range(max_exact + 1, N_BUCKETS):
        n = starts[-1]
        while True:
            n += 1
            large = max_exact + int(math.log(n / max_exact) / math.log(MAX_DISTANCE / max_exact)
                                    * (N_BUCKETS - max_exact))
            if min(large, N_BUCKETS - 1) >= b:
                break
        starts.append(n)
    return tuple(starts)


BUCKET_STARTS = _bucket_starts()
FAR_DISTANCE = BUCKET_STARTS[-1]


def _cparams(sem, vmem=VMEM_LIMIT):
    return pltpu.CompilerParams(dimension_semantics=sem, vmem_limit_bytes=vmem)


def _rms(x, g):
    return x * lax.rsqrt(jnp.mean(x * x, axis=-1, keepdims=True) + EPS) * g


def _dot(a, b):
    return jnp.dot(a, b, preferred_element_type=F32)


def _dot_nt(a, b):
    return lax.dot_general(a, b, (((1,), (1,)), ((), ())), preferred_element_type=F32)


def _shift_rows(u, prev, k):
    n = prev.shape[0]
    rolled = pltpu.roll(u, k, 0)
    head = rolled[:V7X_SUBLANES]
    row = lax.broadcasted_iota(jnp.int32, head.shape, 0)
    for t in range(k):
        head = jnp.where(row == t, prev[n - k + t:n - k + t + 1, :], head)
    return jnp.concatenate([head, rolled[V7X_SUBLANES:]], axis=0)


def _norm_mm_kernel(x_ref, g_ref, w_ref, *out_refs, splits):
    xn = _rms(x_ref[...], g_ref[...]).astype(BF16)
    y = _dot(xn, w_ref[...])
    for o_ref, (lo, hi, scale, transposed) in zip(out_refs, splits):
        part = y[:, lo:hi]
        if scale != 1.0:
            part = part * scale
        if transposed:
            o_ref[0] = part.T.astype(o_ref.dtype)
        else:
            o_ref[...] = part.astype(o_ref.dtype)


def norm_matmul(x, g, w, splits, dtypes, tm, rows_per_seq=None):
    r, d = x.shape
    n = w.shape[1]
    out_shape, out_specs = [], []
    for (lo, hi, _, transposed), dt in zip(splits, dtypes):
        if transposed:
            nt = rows_per_seq // tm
            out_shape.append(jax.ShapeDtypeStruct((r // rows_per_seq, hi - lo, rows_per_seq), dt))
            out_specs.append(pl.BlockSpec((1, hi - lo, tm), lambda i, nt=nt: (i // nt, 0, i % nt)))
        else:
            out_shape.append(jax.ShapeDtypeStruct((r, hi - lo), dt))
            out_specs.append(pl.BlockSpec((tm, hi - lo), lambda i: (i, 0)))
    return pl.pallas_call(
        functools.partial(_norm_mm_kernel, splits=tuple(splits)),
        grid=(r // tm,),
        in_specs=[pl.BlockSpec((tm, d), lambda i: (i, 0)),
                  pl.BlockSpec((1, d), lambda i: (0, 0)),
                  pl.BlockSpec((d, n), lambda i: (0, 0))],
        out_specs=out_specs,
        out_shape=out_shape,
        name="norm_matmul",
        compiler_params=_cparams(("parallel",)),
    )(x, g, w)


def _res_mm_kernel(x_ref, a_ref, w_ref, o_ref):
    o_ref[...] = x_ref[...] + _dot(a_ref[...].astype(BF16), w_ref[...])


def residual_matmul(x, a, w, tm):
    r, d = x.shape
    k = a.shape[1]
    return pl.pallas_call(
        _res_mm_kernel,
        grid=(r // tm,),
        in_specs=[pl.BlockSpec((tm, d), lambda i: (i, 0)),
                  pl.BlockSpec((tm, k), lambda i: (i, 0)),
                  pl.BlockSpec((k, d), lambda i: (0, 0))],
        out_specs=pl.BlockSpec((tm, d), lambda i: (i, 0)),
        out_shape=jax.ShapeDtypeStruct((r, d), F32),
        name="residual_matmul",
        compiler_params=_cparams(("parallel",)),
    )(x, a, w)


def _rg_kernel(x_ref, h0_ref, cb0_ref, g_ref, win_ref, cw_ref, cbias_ref, wax_ref, ba_ref, bx_ref,
               lam_ref, wout_ref, out_ref, hlast_ref, cbout_ref,
               rbuf, a_scr, b_scr, h_scr, *, tm, ts, pos0, conv_w, n_blocks):
    ti = pl.program_id(1)
    d = a_scr.shape[1]
    hdr = rbuf.shape[0] - tm
    nprev = (conv_w - 1) * ts

    @pl.when(ti == 0)
    def _():
        rbuf[hdr - nprev:hdr, :] = cb0_ref[0]
        h_scr[...] = h0_ref[0]

    x = x_ref[0]
    xn = _rms(x, g_ref[...]).astype(BF16)
    u = _dot(xn, win_ref[...])
    gate = u[:, :d]
    rbuf[hdr:hdr + tm, :] = u[:, d:]
    xc = cbias_ref[...] + rbuf[hdr:hdr + tm, :] * cw_ref[conv_w - 1:conv_w, :]
    for j in range(conv_w - 1):
        off = hdr - (conv_w - 1 - j) * ts
        xc = xc + rbuf[off:off + tm, :] * cw_ref[j:j + 1, :]
    new_prev = rbuf[hdr + tm - nprev:hdr + tm, :]
    cbout_ref[0] = new_prev
    rbuf[hdr - nprev:hdr, :] = new_prev

    xcb = xc.astype(BF16)
    blk = d // n_blocks
    ras, rxs = [], []
    for n in range(n_blocks):
        rr = _dot(xcb[:, n * blk:(n + 1) * blk], wax_ref[n])
        ras.append(rr[:, :blk])
        rxs.append(rr[:, blk:])
    r = jax.nn.sigmoid(jnp.concatenate(ras, axis=1) + ba_ref[...])
    i = jax.nn.sigmoid(jnp.concatenate(rxs, axis=1) + bx_ref[...])
    z = -lam_ref[...]
    softplus = jnp.maximum(z, 0.0) + jnp.log(1.0 + jnp.exp(-jnp.abs(z)))
    log_a = (-RG_C) * r * softplus
    a = jnp.exp(log_a)
    mult = jnp.sqrt(1.0 - a * a)
    if pos0 == 0:
        row = lax.broadcasted_iota(jnp.int32, (tm, 1), 0) + ti * tm
        reset = row < ts
        a = jnp.where(reset, 0.0, a)
        mult = jnp.where(reset, 1.0, mult)
    a_scr[...] = a
    b_scr[...] = mult * i * xc

    if ts == 1:
        def group(gi, h):
            base = pl.multiple_of(gi * V7X_SUBLANES, V7X_SUBLANES)
            for rr_ in range(V7X_SUBLANES):
                h = a_scr[pl.ds(base + rr_, 1), :] * h + b_scr[pl.ds(base + rr_, 1), :]
                b_scr[pl.ds(base + rr_, 1), :] = h
            return h
        h = lax.fori_loop(0, tm // V7X_SUBLANES, group, h_scr[...])
    else:
        h = h_scr[...]
        for t in range(tm // ts):
            h = a_scr[t * ts:(t + 1) * ts, :] * h + b_scr[t * ts:(t + 1) * ts, :]
            b_scr[t * ts:(t + 1) * ts, :] = h
    h_scr[...] = h
    hlast_ref[0] = h

    hg = (b_scr[...] * jax.nn.gelu(gate)).astype(BF16)
    out_ref[0] = x + _dot(hg, wout_ref[...])


def rglru_layer(x, h0, cb0, g, w_in, conv_w, conv_b, w_ax, b_a, b_x, lam, w_out, *, tm, ts, pos0):
    bk, r, d = x.shape
    cw = conv_w.shape[0]
    nprev = (cw - 1) * ts
    hdr = max(V7X_SUBLANES, nprev)
    n_blocks = w_ax.shape[0]
    const2 = lambda b, t: (0, 0)
    kern = functools.partial(_rg_kernel, tm=tm, ts=ts, pos0=pos0, conv_w=cw, n_blocks=n_blocks)
    return pl.pallas_call(
        kern,
        grid=(bk, r // tm),
        in_specs=[pl.BlockSpec((1, tm, d), lambda b, t: (b, t, 0)),
                  pl.BlockSpec((1, ts, d), lambda b, t: (b, 0, 0)),
                  pl.BlockSpec((1, nprev, d), lambda b, t: (b, 0, 0)),
                  pl.BlockSpec((1, d), const2),
                  pl.BlockSpec((d, 2 * d), const2),
                  pl.BlockSpec((cw, d), const2),
                  pl.BlockSpec((1, d), const2),
                  pl.BlockSpec(w_ax.shape, lambda b, t: (0, 0, 0)),
                  pl.BlockSpec((1, d), const2),
                  pl.BlockSpec((1, d), const2),
                  pl.BlockSpec((1, d), const2),
                  pl.BlockSpec((d, d), const2)],
        out_specs=[pl.BlockSpec((1, tm, d), lambda b, t: (b, t, 0)),
                   pl.BlockSpec((1, ts, d), lambda b, t: (b, 0, 0)),
                   pl.BlockSpec((1, nprev, d), lambda b, t: (b, 0, 0))],
        out_shape=[jax.ShapeDtypeStruct((bk, r, d), F32),
                   jax.ShapeDtypeStruct((bk, ts, d), F32),
                   jax.ShapeDtypeStruct((bk, nprev, d), F32)],
        scratch_shapes=[pltpu.VMEM((hdr + tm, d), F32),
                        pltpu.VMEM((tm, d), F32),
                        pltpu.VMEM((tm, d), F32),
                        pltpu.VMEM((ts, d), F32)],
        name="rglru_layer",
        compiler_params=_cparams(("parallel", "arbitrary")),
    )(x, h0, cb0, g, w_in, conv_w, conv_b, w_ax, b_a, b_x, lam, w_out)


def _ffn_kernel(x_ref, fbv0_ref, fbg0_ref, g_ref, wv_ref, wg_ref, cwv_ref, cwg_ref, cbv_ref, cbg_ref,
                wd_ref, gfin_ref, out_ref, fbv_ref, fbg_ref,
                xn_scr, acc_scr, ubv, ubg, carry_v, carry_g, *, tm, ts, conv_w, final_norm):
    ti = pl.program_id(1)
    j = pl.program_id(2)
    nj = pl.num_programs(2)
    hdr = ubv.shape[0] - tm
    nprev = (conv_w - 1) * ts

    @pl.when(j == 0)
    def _():
        xn_scr[...] = _rms(x_ref[0], g_ref[...]).astype(BF16)
        acc_scr[...] = jnp.zeros_like(acc_scr)

    xn = xn_scr[...]

    def conv(w_ref, ub, carry, fb0_ref, cw_ref, cb_ref, fb_ref):
        prev = jnp.where(ti == 0, fb0_ref[0], carry[j])
        if ts == 1:
            u = _dot(xn, w_ref[...])
            uc = cb_ref[...] + u * cw_ref[conv_w - 1:conv_w, :]
            for jj in range(conv_w - 1):
                uc = uc + _shift_rows(u, prev, conv_w - 1 - jj) * cw_ref[jj:jj + 1, :]
            new_prev = u[tm - nprev:, :]
        else:
            ub[hdr - nprev:hdr, :] = prev
            ub[hdr:hdr + tm, :] = _dot(xn, w_ref[...])
            uc = cb_ref[...] + ub[hdr:hdr + tm, :] * cw_ref[conv_w - 1:conv_w, :]
            for jj in range(conv_w - 1):
                off = hdr - (conv_w - 1 - jj) * ts
                uc = uc + ub[off:off + tm, :] * cw_ref[jj:jj + 1, :]
            new_prev = ub[hdr + tm - nprev:hdr + tm, :]
        carry[j] = new_prev
        fb_ref[0, 0] = new_prev
        return uc

    val = conv(wv_ref, ubv, carry_v, fbv0_ref, cwv_ref, cbv_ref, fbv_ref)
    gte = conv(wg_ref, ubg, carry_g, fbg0_ref, cwg_ref, cbg_ref, fbg_ref)
    hmid = (jax.nn.gelu(gte) * val).astype(BF16)
    acc_scr[...] += _dot(hmid, wd_ref[...])

    @pl.when(j == nj - 1)
    def _():
        y = x_ref[0] + acc_scr[...]
        if final_norm:
            y = _rms(y, gfin_ref[...])
        out_ref[0] = y


def ffn_layer(x, fb0, g, w_up, conv_w, conv_b, w_down, g_final, *, tm, tn, ts, final_norm):
    bk, r, d = x.shape
    f = w_down.shape[0]
    cw = conv_w.shape[0]
    nprev = (cw - 1) * ts
    hdr = max(V7X_SUBLANES, nprev)
    nc = f // tn
    const2 = lambda b, t, j: (0, 0)
    kern = functools.partial(_ffn_kernel, tm=tm, ts=ts, conv_w=cw, final_norm=final_norm)
    out, fbv, fbg = pl.pallas_call(
        kern,
        grid=(bk, r // tm, nc),
        in_specs=[pl.BlockSpec((1, tm, d), lambda b, t, j: (b, t, 0)),
                  pl.BlockSpec((1, nprev, tn), lambda b, t, j: (b, 0, j)),
                  pl.BlockSpec((1, nprev, tn), lambda b, t, j: (b, 0, nc + j)),
                  pl.BlockSpec((1, d), const2),
                  pl.BlockSpec((d, tn), lambda b, t, j: (0, j)),
                  pl.BlockSpec((d, tn), lambda b, t, j: (0, nc + j)),
                  pl.BlockSpec((cw, tn), lambda b, t, j: (0, j)),
                  pl.BlockSpec((cw, tn), lambda b, t, j: (0, nc + j)),
                  pl.BlockSpec((1, tn), lambda b, t, j: (0, j)),
                  pl.BlockSpec((1, tn), lambda b, t, j: (0, nc + j)),
                  pl.BlockSpec((tn, d), lambda b, t, j: (j, 0)),
                  pl.BlockSpec((1, d), const2)],
        out_specs=[pl.BlockSpec((1, tm, d), lambda b, t, j: (b, t, 0)),
                   pl.BlockSpec((1, 1, nprev, tn), lambda b, t, j: (b, t, 0, j)),
                   pl.BlockSpec((1, 1, nprev, tn), lambda b, t, j: (b, t, 0, j))],
        out_shape=[jax.ShapeDtypeStruct((bk, r, d), F32),
                   jax.ShapeDtypeStruct((bk, r // tm, nprev, f), F32),
                   jax.ShapeDtypeStruct((bk, r // tm, nprev, f), F32)],
        scratch_shapes=[pltpu.VMEM((tm, d), BF16),
                        pltpu.VMEM((tm, d), F32),
                        pltpu.VMEM((hdr + tm, tn), F32),
                        pltpu.VMEM((hdr + tm, tn), F32),
                        pltpu.VMEM((nc, nprev, tn), F32),
                        pltpu.VMEM((nc, nprev, tn), F32)],
        name="ffn_layer",
        compiler_params=_cparams(("parallel", "arbitrary", "arbitrary")),
    )(x, fb0, fb0, g, w_up, w_up, conv_w, conv_w, conv_b, conv_b, w_down, g_final)
    return out, jnp.concatenate([fbv[:, -1], fbg[:, -1]], axis=-1)


def _shifted_bias(rel, rb_ref, h, scale=1.0):
    last = rb_ref[N_BUCKETS - 1, h]
    val = jnp.zeros(rel.shape, F32)
    for b in range(N_BUCKETS - 2, -1, -1):
        val = jnp.where(rel < BUCKET_STARTS[b + 1], (rb_ref[b, h] - last) * scale, val)
    return jnp.where(rel >= 0, val, NEG_INF)


def _near_offsets(tq, tk):
    step = math.gcd(tq, tk)
    lo = -(tq - 1)
    hi = tk - 1 + FAR_DISTANCE - 1
    first = -((-lo) // step) * step
    if first < lo:
        first += step
    count = (hi - first) // step + 1
    return first, step, count


def _prompt_bias_kernel(rb_ref, o_ref, *, tq, tk, first, step, count):
    h = pl.program_id(0)
    o = pl.program_id(1)
    rel = (lax.broadcasted_iota(jnp.int32, (tk, tq), 1)
           - lax.broadcasted_iota(jnp.int32, (tk, tq), 0) + (first + o * step))
    o_ref[0, 0] = jnp.where(o < count, _shifted_bias(rel, rb_ref, h, LOG2E), 0.0)


def prompt_bias(rel_bias, tq, tk):
    n_heads = rel_bias.shape[1]
    first, step, count = _near_offsets(tq, tk)
    return pl.pallas_call(
        functools.partial(_prompt_bias_kernel, tq=tq, tk=tk, first=first, step=step, count=count),
        grid=(n_heads, count + 1),
        in_specs=[pl.BlockSpec(memory_space=pltpu.SMEM)],
        out_specs=pl.BlockSpec((1, 1, tk, tq), lambda h, o: (h, o, 0, 0)),
        out_shape=jax.ShapeDtypeStruct((n_heads, count + 1, tk, tq), F32),
        name="prompt_bias",
        compiler_params=_cparams(("parallel", "parallel")),
    )(rel_bias)


def _sample_bias_kernel(rb_ref, o_ref, *, tq, page, n_heads):
    rows = 2 * tq
    row = lax.broadcasted_iota(jnp.int32, (rows, 2 * page), 0)
    col = lax.broadcasted_iota(jnp.int32, (rows, 2 * page), 1)
    t = row % tq
    rel = jnp.where(col < page, page + t - col, t - (col - page))
    for h in range(n_heads):
        o_ref[h * rows:(h + 1) * rows, :] = _shifted_bias(rel, rb_ref, h)


def sample_bias(rel_bias, tq, page):
    n_heads = rel_bias.shape[1]
    return pl.pallas_call(
        functools.partial(_sample_bias_kernel, tq=tq, page=page, n_heads=n_heads),
        in_specs=[pl.BlockSpec(memory_space=pltpu.SMEM)],
        out_specs=pl.BlockSpec(memory_space=pltpu.VMEM),
        out_shape=jax.ShapeDtypeStruct((n_heads * 2 * tq, 2 * page), F32),
        name="sample_bias",
    )(rel_bias)


def _lambda(lq1_ref, lk1_ref, lq2_ref, lk2_ref, lam_init):
    s1 = jnp.sum(lq1_ref[...] * lk1_ref[...], axis=1, keepdims=True)
    s2 = jnp.sum(lq2_ref[...] * lk2_ref[...], axis=1, keepdims=True)
    return jnp.exp(s1) - jnp.exp(s2) + lam_init


def _block_schedule(tq, tk, nq):
    first, step, count = _near_offsets(tq, tk)
    hi = first + (count - 1) * step
    far, near = [], []
    for qi in range(nq):
        n_far = max(qi * tq - hi + tk - 1, 0) // tk
        k_end = (qi * tq + tq - 1) // tk + 1
        far += [(kj, qi, count) for kj in range(n_far)]
        near += [(kj, qi, (qi * tq - kj * tk - first) // step) for kj in range(n_far, k_end)]
    order = sorted(far) + sorted(near)
    order += [order[-1]] * 2
    return ([o[1] for o in order], [o[0] for o in order], [o[2] for o in order], len(far))


def _attn_kernel(sq_ref, sk_ref, sb_ref, q_ref, k_ref, vt_ref, bias_ref, lq1_ref, lk1_ref, lq2_ref,
                 lk2_ref, subln_ref, o_ref, qs_scr, m_scr, acc_scr, s_scr, p_scr, al_scr,
                 *, tq, tk, n_blocks, n_far, lam_init):
    hw = 2 * HEAD_DIM
    nq = qs_scr.shape[0]

    def prepare(qi, carry):
        q = q_ref[0, pl.ds(pl.multiple_of(qi * tq, tq), tq), :]
        lane = lax.broadcasted_iota(jnp.int32, q.shape, 1)
        qs_scr[qi, :tq, :] = jnp.where(lane < HEAD_DIM, q, jnp.zeros_like(q))
        qs_scr[qi, tq:, :] = jnp.where(lane >= HEAD_DIM, q, jnp.zeros_like(q))
        m_scr[qi] = jnp.full(m_scr.shape[1:], NEG_INF, F32)
        acc_scr[qi] = jnp.zeros(acc_scr.shape[1:], F32)
        return carry

    lax.fori_loop(0, nq, prepare, 0)

    def stage_scores(i, slot):
        start = pl.multiple_of(sk_ref[i] * tk, tk)
        k = k_ref[0, pl.ds(start, tk), :]
        s_scr[slot] = _dot_nt(k, qs_scr[sq_ref[i]])

    def stage_probs(i, slot, far):
        qi = sq_ref[i]
        s = s_scr[slot]
        if not far:
            bias = bias_ref[0, sb_ref[i]]
            s = s + jnp.concatenate([bias, bias], axis=1)
        m_prev = m_scr[qi]
        m_new = jnp.maximum(m_prev, jnp.max(s, axis=0, keepdims=True))
        m_scr[qi] = m_new
        p_scr[slot] = jnp.exp2(s - m_new).astype(BF16)
        al_scr[slot] = jnp.exp2(m_prev - m_new)

    def stage_values(i, slot):
        qi = sq_ref[i]
        start = pl.multiple_of(sk_ref[i] * tk, tk)
        vt = vt_ref[0, 0, :, pl.ds(start, tk)]
        acc_scr[qi] = al_scr[slot] * acc_scr[qi] + _dot(vt, p_scr[slot])

    def pipe_step(i, slot, far):
        stage_scores(i + 2, slot)
        stage_probs(i + 1, 1 - slot, far)
        stage_values(i, slot)

    def run_steps(lo, hi, far):
        if lo < hi and lo % 2 == 1:
            pipe_step(lo, 1, far)
            lo += 1
        n_pairs = (hi - lo) // 2

        def pair(ii, carry):
            pipe_step(lo + 2 * ii, 0, far)
            pipe_step(lo + 2 * ii + 1, 1, far)
            return carry

        if n_pairs > 0:
            lax.fori_loop(0, n_pairs, pair, 0)
        if lo + 2 * n_pairs < hi:
            pipe_step(hi - 1, 0, far)

    stage_scores(0, 0)
    stage_probs(0, 0, n_far > 0)
    stage_scores(1, 1)
    far_steps = max(n_far - 1, 0)
    run_steps(0, far_steps, True)
    run_steps(far_steps, n_blocks - 1, False)
    stage_values(n_blocks - 1, (n_blocks - 1) % 2)

    lam = _lambda(lq1_ref, lk1_ref, lq2_ref, lk2_ref, lam_init)

    def finish(qi, carry):
        acc = acc_scr[qi]
        on = acc[:hw, :] / acc[hw:hw + 1, :]
        ot = on[:, :tq] - lam * on[:, tq:]
        ot = ot * lax.rsqrt(jnp.mean(ot * ot, axis=0, keepdims=True) + EPS)
        ot = ot * subln_ref[...] * (1.0 - lam_init)
        o_ref[0, pl.ds(pl.multiple_of(qi * tq, tq), tq), :] = ot.T.astype(o_ref.dtype)
        return carry

    lax.fori_loop(0, nq, finish, 0)


def prompt_attention(q, k, vt, bias, lq1, lk1, lq2, lk2, subln, *, tq, tk, lam_init):
    b, t, dk = q.shape
    hw = 2 * HEAD_DIM
    n_heads = dk // hw
    vrows = vt.shape[2]
    nq = t // tq
    sched_q, sched_k, sched_b, n_far = _block_schedule(tq, tk, nq)
    n_blocks = len(sched_q) - 2
    vec = lambda shape: pl.BlockSpec(shape, lambda bb, h, *_: (0, 0))
    grid_spec = pltpu.PrefetchScalarGridSpec(
        num_scalar_prefetch=3,
        grid=(b, n_heads),
        in_specs=[pl.BlockSpec((1, t, hw), lambda bb, h, *_: (bb, 0, h)),
                  pl.BlockSpec((1, t, hw), lambda bb, h, *_: (bb, 0, h)),
                  pl.BlockSpec((1, 1, vrows, t), lambda bb, h, *_: (bb, h, 0, 0)),
                  pl.BlockSpec((1,) + bias.shape[1:], lambda bb, h, *_: (h, 0, 0, 0)),
                  vec((1, HEAD_DIM)), vec((1, HEAD_DIM)), vec((1, HEAD_DIM)), vec((1, HEAD_DIM)),
                  vec((hw, 1))],
        out_specs=pl.BlockSpec((1, t, hw), lambda bb, h, *_: (bb, 0, h)),
        scratch_shapes=[pltpu.VMEM((nq, 2 * tq, hw), BF16),
                        pltpu.VMEM((nq, 1, 2 * tq), F32),
                        pltpu.VMEM((nq, vrows, 2 * tq), F32),
                        pltpu.VMEM((2, tk, 2 * tq), F32),
                        pltpu.VMEM((2, tk, 2 * tq), BF16),
                        pltpu.VMEM((2, 1, 2 * tq), F32)])
    return pl.pallas_call(
        functools.partial(_attn_kernel, tq=tq, tk=tk, n_blocks=n_blocks, n_far=n_far, lam_init=lam_init),
        grid_spec=grid_spec,
        out_shape=jax.ShapeDtypeStruct((b, t, dk), BF16),
        name="prompt_attention",
        compiler_params=_cparams(("parallel", "parallel")),
    )(jnp.asarray(sched_q, jnp.int32), jnp.asarray(sched_k, jnp.int32), jnp.asarray(sched_b, jnp.int32),
      q, k, vt, bias, lq1, lk1, lq2, lk2, subln)


def _sample_attn_kernel(pt_ref, *refs, n_pages, page, tq, n_heads, lam_init):
    k_refs = refs[:n_pages]
    v_refs = refs[n_pages:2 * n_pages]
    (q_ref, kn_ref, vn_ref, bias_ref, lq1_ref, lk1_ref, lq2_ref, lk2_ref, subln_ref,
     o_ref, kt_scr, kn_scr, vbf) = refs[2 * n_pages:]
    past = n_pages * page
    hw = 2 * HEAD_DIM
    rows = n_heads * 2 * tq
    dk = n_heads * hw
    nn = kn_ref.shape[1]

    @pl.when(pl.program_id(0) == 0)
    def _():
        kn_scr[...] = jnp.zeros_like(kn_scr)
        for h in range(n_heads):
            vbf[h, past:, :] = jnp.zeros((page, hw), BF16)

    for p in range(n_pages):
        kt_scr[:, p * page:(p + 1) * page] = k_refs[p][0].astype(BF16)
        for h in range(n_heads):
            vbf[h, p * page:(p + 1) * page, :] = (
                v_refs[p][0, pl.ds(h, page, stride=n_heads), :].astype(BF16))
    kn_scr[:nn, :] = kn_ref[0]
    vn = vn_ref[0]
    for h in range(n_heads):
        vbf[h, past:past + nn, :] = vn[:, h * hw:(h + 1) * hw]

    qrep = jnp.concatenate([q_ref[0]] * n_heads, axis=0)
    rr = lax.broadcasted_iota(jnp.int32, (rows, dk), 0)
    cc = lax.broadcasted_iota(jnp.int32, (rows, dk), 1)
    qbd = jnp.where(cc // HEAD_DIM == rr // tq, qrep, 0.0).astype(BF16)

    s_main = _dot(qbd, kt_scr[:, :past - page])
    s_last = _dot(qbd, kt_scr[:, past - page:]) + bias_ref[:, :page]
    s_new = _dot_nt(qbd, kn_scr[...]) + bias_ref[:, page:]
    m = jnp.maximum(jnp.max(s_main, axis=1, keepdims=True),
                    jnp.max(jnp.maximum(s_last, s_new), axis=1, keepdims=True))
    p_all = jnp.concatenate([jnp.exp(s_main - m), jnp.exp(s_last - m), jnp.exp(s_new - m)], axis=1)
    l = jnp.sum(p_all, axis=1, keepdims=True)
    p_bf = p_all.astype(BF16)
    lam = _lambda(lq1_ref, lk1_ref, lq2_ref, lk2_ref, lam_init)
    outs = []
    for h in range(n_heads):
        r0 = h * 2 * tq
        o8 = _dot(p_bf[r0:r0 + 2 * tq], vbf[h]) / l[r0:r0 + 2 * tq]
        o = o8[:tq] - lam * o8[tq:]
        outs.append(_rms(o, subln_ref[...]) * (1.0 - lam_init))
    o_ref[0] = jnp.concatenate(outs, axis=1)


def sample_attention(page_table, cache_kt, cache_v, q8, k_new, v_new, bias, lq1, lk1, lq2, lk2, subln,
                     *, tq, lam_init):
    n_seq, n_pages = page_table.shape
    _, dk, page = cache_kt.shape
    hw = 2 * HEAD_DIM
    n_heads = dk // hw
    nn = k_new.shape[1]
    past = n_pages * page

    def page_spec(p, shape):
        return pl.BlockSpec((1,) + shape, lambda b, pt: (pt[b * n_pages + p], 0, 0))

    vec = lambda shape: pl.BlockSpec(shape, lambda b, pt: (0, 0))
    in_specs = ([page_spec(p, (dk, page)) for p in range(n_pages)]
                + [page_spec(p, (page * n_heads, hw)) for p in range(n_pages)]
                + [pl.BlockSpec((1, 2 * tq, dk), lambda b, pt: (b, 0, 0)),
                   pl.BlockSpec((1, nn, dk), lambda b, pt: (b, 0, 0)),
                   pl.BlockSpec((1, nn, dk), lambda b, pt: (b, 0, 0)),
                   vec(bias.shape),
                   vec((1, HEAD_DIM)), vec((1, HEAD_DIM)), vec((1, HEAD_DIM)), vec((1, HEAD_DIM)),
                   vec((1, hw))])
    grid_spec = pltpu.PrefetchScalarGridSpec(
        num_scalar_prefetch=1,
        grid=(n_seq,),
        in_specs=in_specs,
        out_specs=pl.BlockSpec((1, tq, dk), lambda b, pt: (b, 0, 0)),
        scratch_shapes=[pltpu.VMEM((dk, past), BF16),
                        pltpu.VMEM((page, dk), BF16),
                        pltpu.VMEM((n_heads, past + page, hw), BF16)])
    kern = functools.partial(_sample_attn_kernel, n_pages=n_pages, page=page, tq=tq,
                             n_heads=n_heads, lam_init=lam_init)
    return pl.pallas_call(
        kern,
        grid_spec=grid_spec,
        out_shape=jax.ShapeDtypeStruct((n_seq, tq, dk), F32),
        name="sample_attention",
        compiler_params=_cparams(("arbitrary",)),
    )(page_table.reshape(-1), *([cache_kt] * n_pages), *([cache_v] * n_pages),
      q8, k_new, v_new, bias, lq1, lk1, lq2, lk2, subln)


def _pick_tile(n, target):
    t = min(n, target)
    while n % t:
        t //= 2
    return t


def _run_trunk(x, pos0, ts, rg_h0, rg_cb0, ffn_fb0, w, attn_fn):
    bk, r, d = x.shape
    depth = w['norm_mix'].shape[0]
    n_a = w['rg_w_in'].shape[0]
    tm_rg = _pick_tile(r, 256)
    tm_ffn = _pick_tile(r, 1024)
    tm_mm = _pick_tile(bk * r, 512)
    dk = w['attn_w_q'].shape[2]
    new_h, new_cb, new_fb = [], [], []
    k_new = v_new = kb = vb = None
    for l in range(depth):
        if l < n_a:
            x, h_last, cb = rglru_layer(
                x, rg_h0[l], rg_cb0[l], w['norm_mix'][l][None], w['rg_w_in'][l], w['rg_conv_w'][l],
                w['rg_conv_b'][l][None], w['rg_w_ax'][l], w['rg_b_a'][l][None], w['rg_b_x'][l][None],
                w['rg_lambda'][l][None], w['rg_w_out'][l], tm=tm_rg, ts=ts, pos0=pos0)
            new_h.append(h_last)
            new_cb.append(cb)
        else:
            x2 = x.reshape(bk * r, d)
            if l == n_a:
                k_new, v_new, kb, vb = norm_matmul(
                    x2, w['kv_norm'][None], w['w_kv'],
                    [(0, dk, 1.0, True), (dk, 2 * dk, 1.0, False), (0, dk, 1.0, False),
                     (dk, 2 * dk, 1.0, False)],
                    [F32, F32, BF16, BF16], tm_mm, rows_per_seq=r)
            j = l - n_a
            q_dt = attn_fn.q_dtype
            (q,) = norm_matmul(x2, w['norm_mix'][l][None], w['attn_w_q'][j],
                               [(0, dk, attn_fn.q_scale, False)], [q_dt], tm_mm)
            o = attn_fn(j, l, q, kb, vb)
            x = residual_matmul(x2, o, w['attn_w_o'][j], tm_mm).reshape(bk, r, d)
        last = l == depth - 1
        x, fb = ffn_layer(x, ffn_fb0[l], w['norm_ffn'][l][None], w['ffn_w_up'][l], w['ffn_conv_w'][l],
                          w['ffn_conv_b'][l][None], w['ffn_w_down'][l], w['final_norm'][None],
                          tm=tm_ffn, tn=512, ts=ts, final_norm=last)
        new_fb.append(fb)
    return x, new_h, new_cb, new_fb, k_new, v_new


def _lam_init(layer_idx):
    return 0.8 - 0.6 * math.exp(-0.3 * layer_idx)


def kernel(x_prompt, x_sample, state_rglru_h, state_rglru_conv, state_ffn_conv, cache_k, cache_v,
           page_table, rel_bias, norm_mix, norm_ffn, final_norm, rg_w_in, rg_conv_w, rg_conv_b,
           rg_w_a, rg_b_a, rg_w_x, rg_b_x, rg_lambda, rg_w_out, kv_norm, w_kv, attn_w_q,
           lam_q1, lam_k1, lam_q2, lam_k2, attn_subln, attn_w_o, ffn_w_up, ffn_conv_w,
           ffn_conv_b, ffn_w_down):
    bp, t_p, d = x_prompt.shape
    n_seq, tq, _ = x_sample.shape
    depth = norm_mix.shape[0]
    n_a = rg_w_in.shape[0]
    n_pool, page, n_heads, _, hd = cache_k.shape
    assert hd == HEAD_DIM
    dk = n_heads * 2 * hd
    f2 = ffn_w_up.shape[2]
    past_len = page_table.shape[1] * page
    assert page >= FAR_DISTANCE and past_len >= 2 * page and 2 * tq == V7X_SUBLANES

    w = {
        'norm_mix': norm_mix, 'norm_ffn': norm_ffn, 'final_norm': final_norm,
        'rg_w_in': rg_w_in.astype(BF16), 'rg_conv_w': rg_conv_w, 'rg_conv_b': rg_conv_b,
        'rg_w_ax': jnp.concatenate([rg_w_a, rg_w_x], axis=-1).astype(BF16),
        'rg_b_a': rg_b_a, 'rg_b_x': rg_b_x, 'rg_lambda': rg_lambda,
        'rg_w_out': rg_w_out.astype(BF16), 'kv_norm': kv_norm, 'w_kv': w_kv.astype(BF16),
        'attn_w_q': attn_w_q.astype(BF16), 'attn_w_o': attn_w_o.astype(BF16),
        'ffn_w_up': ffn_w_up.astype(BF16), 'ffn_conv_w': ffn_conv_w, 'ffn_conv_b': ffn_conv_b,
        'ffn_w_down': ffn_w_down.astype(BF16),
    }
    lam_vecs = lambda j: (lam_q1[j][None], lam_k1[j][None], lam_q2[j][None], lam_k2[j][None])

    tq_p = _pick_tile(t_p, 512)
    tk_p = _pick_tile(t_p, 512)
    bias_p = prompt_bias(rel_bias, tq_p, tk_p)

    vt_cache = {}

    def attn_prompt(j, l, q, kb, vb):
        if 'vt' not in vt_cache:
            vt = vb.reshape(bp, t_p, n_heads, 2 * hd).transpose(0, 2, 3, 1)
            extra = jnp.zeros((bp, n_heads, 16, t_p), BF16).at[:, :, 0, :].set(1.0)
            vt_cache['vt'] = jnp.concatenate([vt, extra], axis=2)
        o = prompt_attention(q.reshape(bp, t_p, dk), kb.reshape(bp, t_p, dk), vt_cache['vt'],
                             bias_p, *lam_vecs(j), attn_subln[j][:, None], tq=tq_p, tk=tk_p,
                             lam_init=_lam_init(l))
        return o.reshape(bp * t_p, dk)
    attn_prompt.q_dtype = BF16
    attn_prompt.q_scale = ATTN_SCALE * LOG2E

    cw_rg = rg_conv_w.shape[1]
    cw_ffn = ffn_conv_w.shape[1]
    yp, hp, cbp, fbp, kp, vp = _run_trunk(
        x_prompt, 0, 1,
        jnp.zeros((n_a, bp, 1, d), F32), jnp.zeros((n_a, bp, cw_rg - 1, d), F32),
        jnp.zeros((depth, bp, cw_ffn - 1, f2), F32), w, attn_prompt)

    bias_s = sample_bias(rel_bias, tq, page)
    ck = cache_k.transpose(0, 2, 3, 4, 1).reshape(n_pool, dk, page)
    cv = cache_v.reshape(n_pool, page * n_heads, 2 * hd)
    n_new = 16

    def to_seq_major(a2):
        return a2.reshape(tq, n_seq, -1).transpose(1, 0, 2)

    def attn_sample(j, l, q, kb, vb):
        q4 = to_seq_major(q)
        q8 = jnp.concatenate([q4, q4], axis=1)
        pad = ((0, 0), (0, n_new - tq), (0, 0))
        o = sample_attention(page_table, ck, cv, q8, jnp.pad(to_seq_major(kb), pad),
                             jnp.pad(to_seq_major(vb), pad), bias_s, *lam_vecs(j), attn_subln[j][None],
                             tq=tq, lam_init=_lam_init(l))
        return o.transpose(1, 0, 2).reshape(tq * n_seq, dk)
    attn_sample.q_dtype = F32
    attn_sample.q_scale = ATTN_SCALE

    def tmajor(a):
        return a.transpose(0, 2, 1, 3).reshape(a.shape[0], 1, a.shape[2] * n_seq, a.shape[3])

    def smajor(a2, steps):
        return a2.reshape(steps, n_seq, -1).transpose(1, 0, 2)

    xs = x_sample.transpose(1, 0, 2).reshape(1, tq * n_seq, d)
    ys, hs, cbs, fbs, ks, vs = _run_trunk(
        xs, past_len, n_seq, state_rglru_h[:, None], tmajor(state_rglru_conv), tmajor(state_ffn_conv),
        w, attn_sample)

    return (yp,
            smajor(ys, tq),
            jnp.stack([h[:, 0] for h in hp]),
            jnp.stack(cbp),
            jnp.stack(fbp),
            kp.reshape(bp, n_heads, 2, hd, t_p).transpose(0, 4, 1, 2, 3),
            vp.reshape(bp, t_p, n_heads, 2 * hd),
            jnp.stack([h[0] for h in hs]),
            jnp.stack([smajor(c, cw_rg - 1) for c in cbs]),
            jnp.stack([smajor(fb, cw_ffn - 1) for fb in fbs]),
            ks.reshape(n_heads, 2, hd, tq, n_seq).transpose(4, 3, 0, 1, 2),
            smajor(vs[None], tq).reshape(n_seq, tq, n_heads, 2 * hd))
```

```python
import functools
import math

import jax
import jax.numpy as jnp
from jax import lax
from jax.experimental import pallas as pl
from jax.experimental.pallas import tpu as pltpu

EPS = 1e-6
RG_C = 8.0
HEAD_DIM = 64
N_BUCKETS = 32
MAX_DISTANCE = 128
NEG_INF = -1e30
ATTN_SCALE = HEAD_DIM ** -0.5
LOG2E = math.log2(math.e)

V7X_LANES = 128
V7X_SUBLANES = 8
V7X_VMEM_BYTES = 64 * 2 ** 20
VMEM_LIMIT = V7X_VMEM_BYTES - 8 * 2 ** 20

BF16 = jnp.bfloat16
F32 = jnp.float32


def _bucket_starts():
    max_exact = N_BUCKETS // 2
    starts = list(range(max_exact + 1))
    for b in range(max_exact + 1, N_BUCKETS):
        n = starts[-1]
        while True:
            n += 1
            large = max_exact + int(math.log(n / max_exact) / math.log(MAX_DISTANCE / max_exact)
                                    * (N_BUCKETS - max_exact))
            if min(large, N_BUCKETS - 1) >= b:
                break
        starts.append(n)
    return tuple(starts)


BUCKET_STARTS = _bucket_starts()
FAR_DISTANCE = BUCKET_STARTS[-1]


def _cparams(sem, vmem=VMEM_LIMIT):
    return pltpu.CompilerParams(dimension_semantics=sem, vmem_limit_bytes=vmem)


def _rms(x, g):
    return x * lax.rsqrt(jnp.mean(x * x, axis=-1, keepdims=True) + EPS) * g


def _dot(a, b):
    return jnp.dot(a, b, preferred_element_type=F32)


def _dot_nt(a, b):
    return lax.dot_general(a, b, (((1,), (1,)), ((), ())), preferred_element_type=F32)


def _shift_rows(u, prev, k):
    n = prev.shape[0]
    rolled = pltpu.roll(u, k, 0)
    head = rolled[:V7X_SUBLANES]
    row = lax.broadcasted_iota(jnp.int32, head.shape, 0)
    for t in range(k):
        head = jnp.where(row == t, prev[n - k + t:n - k + t + 1, :], head)
    return jnp.concatenate([head, rolled[V7X_SUBLANES:]], axis=0)


def _norm_mm_kernel(x_ref, g_ref, w_ref, *out_refs, splits):
    xn = _rms(x_ref[...], g_ref[...]).astype(BF16)
    y = _dot(xn, w_ref[...])
    for o_ref, (lo, hi, scale, transposed) in zip(out_refs, splits):
        part = y[:, lo:hi]
        if scale != 1.0:
            part = part * scale
        if transposed:
            o_ref[0] = part.T.astype(o_ref.dtype)
        else:
            o_ref[...] = part.astype(o_ref.dtype)


def norm_matmul(x, g, w, splits, dtypes, tm, rows_per_seq=None):
    r, d = x.shape
    n = w.shape[1]
    out_shape, out_specs = [], []
    for (lo, hi, _, transposed), dt in zip(splits, dtypes):
        if transposed:
            nt = rows_per_seq // tm
            out_shape.append(jax.ShapeDtypeStruct((r // rows_per_seq, hi - lo, rows_per_seq), dt))
            out_specs.append(pl.BlockSpec((1, hi - lo, tm), lambda i, nt=nt: (i // nt, 0, i % nt)))
        else:
            out_shape.append(jax.ShapeDtypeStruct((r, hi - lo), dt))
            out_specs.append(pl.BlockSpec((tm, hi - lo), lambda i: (i, 0)))
    return pl.pallas_call(
        functools.partial(_norm_mm_kernel, splits=tuple(splits)),
        grid=(r // tm,),
        in_specs=[pl.BlockSpec((tm, d), lambda i: (i, 0)),
                  pl.BlockSpec((1, d), lambda i: (0, 0)),
                  pl.BlockSpec((d, n), lambda i: (0, 0))],
        out_specs=out_specs,
        out_shape=out_shape,
        name="norm_matmul",
        compiler_params=_cparams(("parallel",)),
    )(x, g, w)


def _res_mm_kernel(x_ref, a_ref, w_ref, o_ref):
    o_ref[...] = x_ref[...] + _dot(a_ref[...].astype(BF16), w_ref[...])


def residual_matmul(x, a, w, tm):
    r, d = x.shape
    k = a.shape[1]
    return pl.pallas_call(
        _res_mm_kernel,
        grid=(r // tm,),
        in_specs=[pl.BlockSpec((tm, d), lambda i: (i, 0)),
                  pl.BlockSpec((tm, k), lambda i: (i, 0)),
                  pl.BlockSpec((k, d), lambda i: (0, 0))],
        out_specs=pl.BlockSpec((tm, d), lambda i: (i, 0)),
        out_shape=jax.ShapeDtypeStruct((r, d), F32),
        name="residual_matmul",
        compiler_params=_cparams(("parallel",)),
    )(x, a, w)


def _rg_kernel(x_ref, h0_ref, cb0_ref, g_ref, win_ref, cw_ref, cbias_ref, wax_ref, ba_ref, bx_ref,
               lam_ref, wout_ref, out_ref, hlast_ref, cbout_ref,
               rbuf, a_scr, b_scr, h_scr, *, tm, ts, pos0, conv_w, n_blocks):
    ti = pl.program_id(1)
    d = a_scr.shape[1]
    hdr = rbuf.shape[0] - tm
    nprev = (conv_w - 1) * ts

    @pl.when(ti == 0)
    def _():
        rbuf[hdr - nprev:hdr, :] = cb0_ref[0]
        h_scr[...] = h0_ref[0]

    x = x_ref[0]
    xn = _rms(x, g_ref[...]).astype(BF16)
    u = _dot(xn, win_ref[...])
    gate = u[:, :d]
    rbuf[hdr:hdr + tm, :] = u[:, d:]
    xc = cbias_ref[...] + rbuf[hdr:hdr + tm, :] * cw_ref[conv_w - 1:conv_w, :]
    for j in range(conv_w - 1):
        off = hdr - (conv_w - 1 - j) * ts
        xc = xc + rbuf[off:off + tm, :] * cw_ref[j:j + 1, :]
    new_prev = rbuf[hdr + tm - nprev:hdr + tm, :]
    cbout_ref[0] = new_prev
    rbuf[hdr - nprev:hdr, :] = new_prev

    xcb = xc.astype(BF16)
    blk = d // n_blocks
    ras, rxs = [], []
    for n in range(n_blocks):
        rr = _dot(xcb[:, n * blk:(n + 1) * blk], wax_ref[n])
        ras.append(rr[:, :blk])
        rxs.append(rr[:, blk:])
    r = jax.nn.sigmoid(jnp.concatenate(ras, axis=1) + ba_ref[...])
    i = jax.nn.sigmoid(jnp.concatenate(rxs, axis=1) + bx_ref[...])
    z = -lam_ref[...]
    softplus = jnp.maximum(z, 0.0) + jnp.log(1.0 + jnp.exp(-jnp.abs(z)))
    log_a = (-RG_C) * r * softplus
    a = jnp.exp(log_a)
    mult = jnp.sqrt(1.0 - a * a)
    if pos0 == 0:
        row = lax.broadcasted_iota(jnp.int32, (tm, 1), 0) + ti * tm
        reset = row < ts
        a = jnp.where(reset, 0.0, a)
        mult = jnp.where(reset, 1.0, mult)
    a_scr[...] = a
    b_scr[...] = mult * i * xc

    if ts == 1:
        def group(gi, h):
            base = pl.multiple_of(gi * V7X_SUBLANES, V7X_SUBLANES)
            for rr_ in range(V7X_SUBLANES):
                h = a_scr[pl.ds(base + rr_, 1), :] * h + b_scr[pl.ds(base + rr_, 1), :]
                b_scr[pl.ds(base + rr_, 1), :] = h
            return h
        h = lax.fori_loop(0, tm // V7X_SUBLANES, group, h_scr[...])
    else:
        h = h_scr[...]
        for t in range(tm // ts):
            h = a_scr[t * ts:(t + 1) * ts, :] * h + b_scr[t * ts:(t + 1) * ts, :]
            b_scr[t * ts:(t + 1) * ts, :] = h
    h_scr[...] = h
    hlast_ref[0] = h

    hg = (b_scr[...] * jax.nn.gelu(gate)).astype(BF16)
    out_ref[0] = x + _dot(hg, wout_ref[...])


def rglru_layer(x, h0, cb0, g, w_in, conv_w, conv_b, w_ax, b_a, b_x, lam, w_out, *, tm, ts, pos0):
    bk, r, d = x.shape
    cw = conv_w.shape[0]
    nprev = (cw - 1) * ts
    hdr = max(V7X_SUBLANES, nprev)
    n_blocks = w_ax.shape[0]
    const2 = lambda b, t: (0, 0)
    kern = functools.partial(_rg_kernel, tm=tm, ts=ts, pos0=pos0, conv_w=cw, n_blocks=n_blocks)
    return pl.pallas_call(
        kern,
        grid=(bk, r // tm),
        in_specs=[pl.BlockSpec((1, tm, d), lambda b, t: (b, t, 0)),
                  pl.BlockSpec((1, ts, d), lambda b, t: (b, 0, 0)),
                  pl.BlockSpec((1, nprev, d), lambda b, t: (b, 0, 0)),
                  pl.BlockSpec((1, d), const2),
                  pl.BlockSpec((d, 2 * d), const2),
                  pl.BlockSpec((cw, d), const2),
                  pl.BlockSpec((1, d), const2),
                  pl.BlockSpec(w_ax.shape, lambda b, t: (0, 0, 0)),
                  pl.BlockSpec((1, d), const2),
                  pl.BlockSpec((1, d), const2),
                  pl.BlockSpec((1, d), const2),
                  pl.BlockSpec((d, d), const2)],
        out_specs=[pl.BlockSpec((1, tm, d), lambda b, t: (b, t, 0)),
                   pl.BlockSpec((1, ts, d), lambda b, t: (b, 0, 0)),
                   pl.BlockSpec((1, nprev, d), lambda b, t: (b, 0, 0))],
        out_shape=[jax.ShapeDtypeStruct((bk, r, d), F32),
                   jax.ShapeDtypeStruct((bk, ts, d), F32),
                   jax.ShapeDtypeStruct((bk, nprev, d), F32)],
        scratch_shapes=[pltpu.VMEM((hdr + tm, d), F32),
                        pltpu.VMEM((tm, d), F32),
                        pltpu.VMEM((tm, d), F32),
                        pltpu.VMEM((ts, d), F32)],
        name="rglru_layer",
        compiler_params=_cparams(("parallel", "arbitrary")),
    )(x, h0, cb0, g, w_in, conv_w, conv_b, w_ax, b_a, b_x, lam, w_out)


def _ffn_kernel(x_ref, fb0_ref, g_ref, wup_ref, cw_ref, cb_ref, wd_ref, gfin_ref, out_ref, fb_ref,
                xn_scr, acc_scr, u_scr, h_scr, carry, *, tm, tn, ts, conv_w, final_norm):
    ti = pl.program_id(1)
    f = wd_ref.shape[0]
    nc = f // tn
    hdr = u_scr.shape[2] - tm
    nprev = (conv_w - 1) * ts

    xn_scr[...] = _rms(x_ref[0], g_ref[...]).astype(BF16)

    def stage_up(c, slot):
        xn = xn_scr[...]
        for half in range(2):
            col = half * f + c * tn
            u_scr[slot, half, hdr:hdr + tm, :] = _dot(xn, wup_ref[:, col:col + tn])

    def conv(c, slot, half):
        col = half * f + c * tn
        prev = jnp.where(ti == 0, fb0_ref[0, :, col:col + tn], carry[half, c])
        cw = cw_ref[:, col:col + tn]
        if ts == 1:
            u = u_scr[slot, half]
            uc = cb_ref[:, col:col + tn] + u * cw[conv_w - 1:conv_w, :]
            for jj in range(conv_w - 1):
                uc = uc + _shift_rows(u, prev, conv_w - 1 - jj) * cw[jj:jj + 1, :]
            new_prev = u[tm - nprev:, :]
        else:
            u_scr[slot, half, :hdr, :] = prev
            uc = cb_ref[:, col:col + tn] + u_scr[slot, half, hdr:hdr + tm, :] * cw[conv_w - 1:conv_w, :]
            for jj in range(conv_w - 1):
                off = hdr - (conv_w - 1 - jj) * ts
                uc = uc + u_scr[slot, half, off:off + tm, :] * cw[jj:jj + 1, :]
            new_prev = u_scr[slot, half, tm:hdr + tm, :]
        carry[half, c] = new_prev
        fb_ref[0, 0, :, col:col + tn] = new_prev
        return uc

    def stage_gate(c, slot):
        val = conv(c, slot, 0)
        gte = conv(c, slot, 1)
        h_scr[slot] = (jax.nn.gelu(gte) * val).astype(BF16)

    def stage_down(c, slot):
        y = _dot(h_scr[slot], wd_ref[c * tn:(c + 1) * tn, :])
        if c == 0:
            acc_scr[...] = y
        else:
            acc_scr[...] += y

    stage_up(0, 0)
    stage_gate(0, 0)
    if nc > 1:
        stage_up(1, 1)
    for c in range(nc - 1):
        if c + 2 < nc:
            stage_up(c + 2, c % 2)
        stage_gate(c + 1, (c + 1) % 2)
        stage_down(c, c % 2)
    stage_down(nc - 1, (nc - 1) % 2)

    y = x_ref[0] + acc_scr[...]
    if final_norm:
        y = _rms(y, gfin_ref[...])
    out_ref[0] = y


def ffn_layer(x, fb0, g, w_up, conv_w, conv_b, w_down, g_final, *, tm, tn, ts, final_norm):
    bk, r, d = x.shape
    f = w_down.shape[0]
    cw = conv_w.shape[0]
    nprev = (cw - 1) * ts
    hdr = 0 if ts == 1 else nprev
    nc = f // tn
    const2 = lambda b, t: (0, 0)
    resident = pl.Buffered(1)
    kern = functools.partial(_ffn_kernel, tm=tm, tn=tn, ts=ts, conv_w=cw, final_norm=final_norm)
    out, fb = pl.pallas_call(
        kern,
        grid=(bk, r // tm),
        in_specs=[pl.BlockSpec((1, tm, d), lambda b, t: (b, t, 0)),
                  pl.BlockSpec((1, nprev, 2 * f), lambda b, t: (b, 0, 0)),
                  pl.BlockSpec((1, d), const2),
                  pl.BlockSpec((d, 2 * f), const2, pipeline_mode=resident),
                  pl.BlockSpec((cw, 2 * f), const2),
                  pl.BlockSpec((1, 2 * f), const2),
                  pl.BlockSpec((f, d), const2, pipeline_mode=resident),
                  pl.BlockSpec((1, d), const2)],
        out_specs=[pl.BlockSpec((1, tm, d), lambda b, t: (b, t, 0)),
                   pl.BlockSpec((1, 1, nprev, 2 * f), lambda b, t: (b, t, 0, 0))],
        out_shape=[jax.ShapeDtypeStruct((bk, r, d), F32),
                   jax.ShapeDtypeStruct((bk, r // tm, nprev, 2 * f), F32)],
        scratch_shapes=[pltpu.VMEM((tm, d), BF16),
                        pltpu.VMEM((tm, d), F32),
                        pltpu.VMEM((2, 2, hdr + tm, tn), F32),
                        pltpu.VMEM((2, tm, tn), BF16),
                        pltpu.VMEM((2, nc, nprev, tn), F32)],
        name="ffn_layer",
        compiler_params=_cparams(("parallel", "arbitrary")),
    )(x, fb0, g, w_up, conv_w, conv_b, w_down, g_final)
    return out, fb[:, -1]


def _shifted_bias(rel, rb_ref, h, scale=1.0):
    last = rb_ref[N_BUCKETS - 1, h]
    val = jnp.zeros(rel.shape, F32)
    for b in range(N_BUCKETS - 2, -1, -1):
        val = jnp.where(rel < BUCKET_STARTS[b + 1], (rb_ref[b, h] - last) * scale, val)
    return jnp.where(rel >= 0, val, NEG_INF)


def _near_offsets(tq, tk):
    step = math.gcd(tq, tk)
    lo = -(tq - 1)
    hi = tk - 1 + FAR_DISTANCE - 1
    first = -((-lo) // step) * step
    if first < lo:
        first += step
    count = (hi - first) // step + 1
    return first, step, count


def _prompt_bias_kernel(rb_ref, o_ref, *, tq, tk, first, step, count):
    h = pl.program_id(0)
    o = pl.program_id(1)
    rel = (lax.broadcasted_iota(jnp.int32, (tk, tq), 1)
           - lax.broadcasted_iota(jnp.int32, (tk, tq), 0) + (first + o * step))
    o_ref[0, 0] = jnp.where(o < count, _shifted_bias(rel, rb_ref, h, LOG2E), 0.0)


def prompt_bias(rel_bias, tq, tk):
    n_heads = rel_bias.shape[1]
    first, step, count = _near_offsets(tq, tk)
    return pl.pallas_call(
        functools.partial(_prompt_bias_kernel, tq=tq, tk=tk, first=first, step=step, count=count),
        grid=(n_heads, count + 1),
        in_specs=[pl.BlockSpec(memory_space=pltpu.SMEM)],
        out_specs=pl.BlockSpec((1, 1, tk, tq), lambda h, o: (h, o, 0, 0)),
        out_shape=jax.ShapeDtypeStruct((n_heads, count + 1, tk, tq), F32),
        name="prompt_bias",
        compiler_params=_cparams(("parallel", "parallel")),
    )(rel_bias)


def _sample_bias_kernel(rb_ref, o_ref, *, tq, page, n_heads):
    rows = 2 * tq
    row = lax.broadcasted_iota(jnp.int32, (rows, 2 * page), 0)
    col = lax.broadcasted_iota(jnp.int32, (rows, 2 * page), 1)
    t = row % tq
    rel = jnp.where(col < page, page + t - col, t - (col - page))
    for h in range(n_heads):
        o_ref[h * rows:(h + 1) * rows, :] = _shifted_bias(rel, rb_ref, h)


def sample_bias(rel_bias, tq, page):
    n_heads = rel_bias.shape[1]
    return pl.pallas_call(
        functools.partial(_sample_bias_kernel, tq=tq, page=page, n_heads=n_heads),
        in_specs=[pl.BlockSpec(memory_space=pltpu.SMEM)],
        out_specs=pl.BlockSpec(memory_space=pltpu.VMEM),
        out_shape=jax.ShapeDtypeStruct((n_heads * 2 * tq, 2 * page), F32),
        name="sample_bias",
    )(rel_bias)


def _lambda(lq1_ref, lk1_ref, lq2_ref, lk2_ref, lam_init):
    s1 = jnp.sum(lq1_ref[...] * lk1_ref[...], axis=1, keepdims=True)
    s2 = jnp.sum(lq2_ref[...] * lk2_ref[...], axis=1, keepdims=True)
    return jnp.exp(s1) - jnp.exp(s2) + lam_init


def _block_schedule(tq, tk, nq):
    first, step, count = _near_offsets(tq, tk)
    hi = first + (count - 1) * step
    far, near = [], []
    for qi in range(nq):
        n_far = max(qi * tq - hi + tk - 1, 0) // tk
        k_end = (qi * tq + tq - 1) // tk + 1
        far += [(kj, qi, count) for kj in range(n_far)]
        near += [(kj, qi, (qi * tq - kj * tk - first) // step) for kj in range(n_far, k_end)]
    order = sorted(far) + sorted(near)
    order += [order[-1]] * 2
    return ([o[1] for o in order], [o[0] for o in order], [o[2] for o in order], len(far))


def _attn_kernel(sq_ref, sk_ref, sb_ref, q_ref, k_ref, vt_ref, bias_ref, lq1_ref, lk1_ref, lq2_ref,
                 lk2_ref, subln_ref, o_ref, qs_scr, m_scr, acc_scr, s_scr, p_scr, al_scr,
                 *, tq, tk, n_blocks, n_far, lam_init):
    hw = 2 * HEAD_DIM
    nq = qs_scr.shape[0]

    def prepare(qi, carry):
        q = q_ref[0, pl.ds(pl.multiple_of(qi * tq, tq), tq), :]
        lane = lax.broadcasted_iota(jnp.int32, q.shape, 1)
        qs_scr[qi, :tq, :] = jnp.where(lane < HEAD_DIM, q, jnp.zeros_like(q))
        qs_scr[qi, tq:, :] = jnp.where(lane >= HEAD_DIM, q, jnp.zeros_like(q))
        m_scr[qi] = jnp.full(m_scr.shape[1:], NEG_INF, F32)
        acc_scr[qi] = jnp.zeros(acc_scr.shape[1:], F32)
        return carry

    lax.fori_loop(0, nq, prepare, 0)

    def stage_scores(i, slot):
        start = pl.multiple_of(sk_ref[i] * tk, tk)
        k = k_ref[0, pl.ds(start, tk), :]
        s_scr[slot] = _dot_nt(k, qs_scr[sq_ref[i]])

    def stage_probs(i, slot, far):
        qi = sq_ref[i]
        s = s_scr[slot]
        if not far:
            bias = bias_ref[0, sb_ref[i]]
            s = s + jnp.concatenate([bias, bias], axis=1)
        m_prev = m_scr[qi]
        m_new = jnp.maximum(m_prev, jnp.max(s, axis=0, keepdims=True))
        m_scr[qi] = m_new
        p_scr[slot] = jnp.exp2(s - m_new).astype(BF16)
        al_scr[slot] = jnp.exp2(m_prev - m_new)

    def stage_values(i, slot):
        qi = sq_ref[i]
        start = pl.multiple_of(sk_ref[i] * tk, tk)
        vt = vt_ref[0, 0, :, pl.ds(start, tk)]
        acc_scr[qi] = al_scr[slot] * acc_scr[qi] + _dot(vt, p_scr[slot])

    def pipe_step(i, slot, far):
        stage_scores(i + 2, slot)
        stage_probs(i + 1, 1 - slot, far)
        stage_values(i, slot)

    def run_steps(lo, hi, far):
        if lo < hi and lo % 2 == 1:
            pipe_step(lo, 1, far)
            lo += 1
        n_pairs = (hi - lo) // 2

        def pair(ii, carry):
            pipe_step(lo + 2 * ii, 0, far)
            pipe_step(lo + 2 * ii + 1, 1, far)
            return carry

        if n_pairs > 0:
            lax.fori_loop(0, n_pairs, pair, 0)
        if lo + 2 * n_pairs < hi:
            pipe_step(hi - 1, 0, far)

    stage_scores(0, 0)
    stage_probs(0, 0, n_far > 0)
    stage_scores(1, 1)
    far_steps = max(n_far - 1, 0)
    run_steps(0, far_steps, True)
    run_steps(far_steps, n_blocks - 1, False)
    stage_values(n_blocks - 1, (n_blocks - 1) % 2)

    lam = _lambda(lq1_ref, lk1_ref, lq2_ref, lk2_ref, lam_init)

    def finish(qi, carry):
        acc = acc_scr[qi]
        on = acc[:hw, :] / acc[hw:hw + 1, :]
        ot = on[:, :tq] - lam * on[:, tq:]
        ot = ot * lax.rsqrt(jnp.mean(ot * ot, axis=0, keepdims=True) + EPS)
        ot = ot * subln_ref[...] * (1.0 - lam_init)
        o_ref[0, pl.ds(pl.multiple_of(qi * tq, tq), tq), :] = ot.T.astype(o_ref.dtype)
        return carry

    lax.fori_loop(0, nq, finish, 0)


def prompt_attention(q, k, vt, bias, lq1, lk1, lq2, lk2, subln, *, tq, tk, lam_init):
    b, t, dk = q.shape
    hw = 2 * HEAD_DIM
    n_heads = dk // hw
    vrows = vt.shape[2]
    nq = t // tq
    sched_q, sched_k, sched_b, n_far = _block_schedule(tq, tk, nq)
    n_blocks = len(sched_q) - 2
    vec = lambda shape: pl.BlockSpec(shape, lambda bb, h, *_: (0, 0))
    grid_spec = pltpu.PrefetchScalarGridSpec(
        num_scalar_prefetch=3,
        grid=(b, n_heads),
        in_specs=[pl.BlockSpec((1, t, hw), lambda bb, h, *_: (bb, 0, h)),
                  pl.BlockSpec((1, t, hw), lambda bb, h, *_: (bb, 0, h)),
                  pl.BlockSpec((1, 1, vrows, t), lambda bb, h, *_: (bb, h, 0, 0)),
                  pl.BlockSpec((1,) + bias.shape[1:], lambda bb, h, *_: (h, 0, 0, 0)),
                  vec((1, HEAD_DIM)), vec((1, HEAD_DIM)), vec((1, HEAD_DIM)), vec((1, HEAD_DIM)),
                  vec((hw, 1))],
        out_specs=pl.BlockSpec((1, t, hw), lambda bb, h, *_: (bb, 0, h)),
        scratch_shapes=[pltpu.VMEM((nq, 2 * tq, hw), BF16),
                        pltpu.VMEM((nq, 1, 2 * tq), F32),
                        pltpu.VMEM((nq, vrows, 2 * tq), F32),
                        pltpu.VMEM((2, tk, 2 * tq), F32),
                        pltpu.VMEM((2, tk, 2 * tq), BF16),
                        pltpu.VMEM((2, 1, 2 * tq), F32)])
    return pl.pallas_call(
        functools.partial(_attn_kernel, tq=tq, tk=tk, n_blocks=n_blocks, n_far=n_far, lam_init=lam_init),
        grid_spec=grid_spec,
        out_shape=jax.ShapeDtypeStruct((b, t, dk), BF16),
        name="prompt_attention",
        compiler_params=_cparams(("parallel", "parallel")),
    )(jnp.asarray(sched_q, jnp.int32), jnp.asarray(sched_k, jnp.int32), jnp.asarray(sched_b, jnp.int32),
      q, k, vt, bias, lq1, lk1, lq2, lk2, subln)


def _sample_attn_kernel(pt_ref, *refs, n_pages, page, tq, n_heads, lam_init):
    k_refs = refs[:n_pages]
    v_refs = refs[n_pages:2 * n_pages]
    (q_ref, kn_ref, vn_ref, bias_ref, lq1_ref, lk1_ref, lq2_ref, lk2_ref, subln_ref,
     o_ref, kt_scr, kn_scr, vbf) = refs[2 * n_pages:]
    past = n_pages * page
    hw = 2 * HEAD_DIM
    rows = n_heads * 2 * tq
    dk = n_heads * hw
    nn = kn_ref.shape[1]

    @pl.when(pl.program_id(0) == 0)
    def _():
        kn_scr[...] = jnp.zeros_like(kn_scr)
        for h in range(n_heads):
            vbf[h, past:, :] = jnp.zeros((page, hw), BF16)

    for p in range(n_pages):
        kt_scr[:, p * page:(p + 1) * page] = k_refs[p][0].astype(BF16)
        for h in range(n_heads):
            vbf[h, p * page:(p + 1) * page, :] = (
                v_refs[p][0, pl.ds(h, page, stride=n_heads), :].astype(BF16))
    kn_scr[:nn, :] = kn_ref[0]
    vn = vn_ref[0]
    for h in range(n_heads):
        vbf[h, past:past + nn, :] = vn[:, h * hw:(h + 1) * hw]

    qrep = jnp.concatenate([q_ref[0]] * n_heads, axis=0)
    rr = lax.broadcasted_iota(jnp.int32, (rows, dk), 0)
    cc = lax.broadcasted_iota(jnp.int32, (rows, dk), 1)
    qbd = jnp.where(cc // HEAD_DIM == rr // tq, qrep, 0.0).astype(BF16)

    s_main = _dot(qbd, kt_scr[:, :past - page])
    s_last = _dot(qbd, kt_scr[:, past - page:]) + bias_ref[:, :page]
    s_new = _dot_nt(qbd, kn_scr[...]) + bias_ref[:, page:]
    m = jnp.maximum(jnp.max(s_main, axis=1, keepdims=True),
                    jnp.max(jnp.maximum(s_last, s_new), axis=1, keepdims=True))
    p_all = jnp.concatenate([jnp.exp(s_main - m), jnp.exp(s_last - m), jnp.exp(s_new - m)], axis=1)
    l = jnp.sum(p_all, axis=1, keepdims=True)
    p_bf = p_all.astype(BF16)
    lam = _lambda(lq1_ref, lk1_ref, lq2_ref, lk2_ref, lam_init)
    outs = []
    for h in range(n_heads):
        r0 = h * 2 * tq
        o8 = _dot(p_bf[r0:r0 + 2 * tq], vbf[h]) / l[r0:r0 + 2 * tq]
        o = o8[:tq] - lam * o8[tq:]
        outs.append(_rms(o, subln_ref[...]) * (1.0 - lam_init))
    o_ref[0] = jnp.concatenate(outs, axis=1)


def sample_attention(page_table, cache_kt, cache_v, q8, k_new, v_new, bias, lq1, lk1, lq2, lk2, subln,
                     *, tq, lam_init):
    n_seq, n_pages = page_table.shape
    _, dk, page = cache_kt.shape
    hw = 2 * HEAD_DIM
    n_heads = dk // hw
    nn = k_new.shape[1]
    past = n_pages * page

    def page_spec(p, shape):
        return pl.BlockSpec((1,) + shape, lambda b, pt: (pt[b * n_pages + p], 0, 0))

    vec = lambda shape: pl.BlockSpec(shape, lambda b, pt: (0, 0))
    in_specs = ([page_spec(p, (dk, page)) for p in range(n_pages)]
                + [page_spec(p, (page * n_heads, hw)) for p in range(n_pages)]
                + [pl.BlockSpec((1, 2 * tq, dk), lambda b, pt: (b, 0, 0)),
                   pl.BlockSpec((1, nn, dk), lambda b, pt: (b, 0, 0)),
                   pl.BlockSpec((1, nn, dk), lambda b, pt: (b, 0, 0)),
                   vec(bias.shape),
                   vec((1, HEAD_DIM)), vec((1, HEAD_DIM)), vec((1, HEAD_DIM)), vec((1, HEAD_DIM)),
                   vec((1, hw))])
    grid_spec = pltpu.PrefetchScalarGridSpec(
        num_scalar_prefetch=1,
        grid=(n_seq,),
        in_specs=in_specs,
        out_specs=pl.BlockSpec((1, tq, dk), lambda b, pt: (b, 0, 0)),
        scratch_shapes=[pltpu.VMEM((dk, past), BF16),
                        pltpu.VMEM((page, dk), BF16),
                        pltpu.VMEM((n_heads, past + page, hw), BF16)])
    kern = functools.partial(_sample_attn_kernel, n_pages=n_pages, page=page, tq=tq,
                             n_heads=n_heads, lam_init=lam_init)
    return pl.pallas_call(
        kern,
        grid_spec=grid_spec,
        out_shape=jax.ShapeDtypeStruct((n_seq, tq, dk), F32),
        name="sample_attention",
        compiler_params=_cparams(("arbitrary",)),
    )(page_table.reshape(-1), *([cache_kt] * n_pages), *([cache_v] * n_pages),
      q8, k_new, v_new, bias, lq1, lk1, lq2, lk2, subln)


def _pick_tile(n, target):
    t = min(n, target)
    while n % t:
        t //= 2
    return t


def _run_trunk(x, pos0, ts, rg_h0, rg_cb0, ffn_fb0, w, attn_fn):
    bk, r, d = x.shape
    depth = w['norm_mix'].shape[0]
    n_a = w['rg_w_in'].shape[0]
    tm_rg = _pick_tile(r, 256)
    tm_ffn = _pick_tile(r, 512)
    tm_mm = _pick_tile(bk * r, 512)
    dk = w['attn_w_q'].shape[2]
    new_h, new_cb, new_fb = [], [], []
    k_new = v_new = kb = vb = None
    for l in range(depth):
        if l < n_a:
            x, h_last, cb = rglru_layer(
                x, rg_h0[l], rg_cb0[l], w['norm_mix'][l][None], w['rg_w_in'][l], w['rg_conv_w'][l],
                w['rg_conv_b'][l][None], w['rg_w_ax'][l], w['rg_b_a'][l][None], w['rg_b_x'][l][None],
                w['rg_lambda'][l][None], w['rg_w_out'][l], tm=tm_rg, ts=ts, pos0=pos0)
            new_h.append(h_last)
            new_cb.append(cb)
        else:
            x2 = x.reshape(bk * r, d)
            if l == n_a:
                k_new, v_new, kb, vb = norm_matmul(
                    x2, w['kv_norm'][None], w['w_kv'],
                    [(0, dk, 1.0, True), (dk, 2 * dk, 1.0, False), (0, dk, 1.0, False),
                     (dk, 2 * dk, 1.0, False)],
                    [F32, F32, BF16, BF16], tm_mm, rows_per_seq=r)
            j = l - n_a
            q_dt = attn_fn.q_dtype
            (q,) = norm_matmul(x2, w['norm_mix'][l][None], w['attn_w_q'][j],
                               [(0, dk, attn_fn.q_scale, False)], [q_dt], tm_mm)
            o = attn_fn(j, l, q, kb, vb)
            x = residual_matmul(x2, o, w['attn_w_o'][j], tm_mm).reshape(bk, r, d)
        last = l == depth - 1
        x, fb = ffn_layer(x, ffn_fb0[l], w['norm_ffn'][l][None], w['ffn_w_up'][l], w['ffn_conv_w'][l],
                          w['ffn_conv_b'][l][None], w['ffn_w_down'][l], w['final_norm'][None],
                          tm=tm_ffn, tn=512, ts=ts, final_norm=last)
        new_fb.append(fb)
    return x, new_h, new_cb, new_fb, k_new, v_new


def _lam_init(layer_idx):
    return 0.8 - 0.6 * math.exp(-0.3 * layer_idx)


def kernel(x_prompt, x_sample, state_rglru_h, state_rglru_conv, state_ffn_conv, cache_k, cache_v,
           page_table, rel_bias, norm_mix, norm_ffn, final_norm, rg_w_in, rg_conv_w, rg_conv_b,
           rg_w_a, rg_b_a, rg_w_x, rg_b_x, rg_lambda, rg_w_out, kv_norm, w_kv, attn_w_q,
           lam_q1, lam_k1, lam_q2, lam_k2, attn_subln, attn_w_o, ffn_w_up, ffn_conv_w,
           ffn_conv_b, ffn_w_down):
    bp, t_p, d = x_prompt.shape
    n_seq, tq, _ = x_sample.shape
    depth = norm_mix.shape[0]
    n_a = rg_w_in.shape[0]
    n_pool, page, n_heads, _, hd = cache_k.shape
    assert hd == HEAD_DIM
    dk = n_heads * 2 * hd
    f2 = ffn_w_up.shape[2]
    past_len = page_table.shape[1] * page
    assert page >= FAR_DISTANCE and past_len >= 2 * page and 2 * tq == V7X_SUBLANES

    w = {
        'norm_mix': norm_mix, 'norm_ffn': norm_ffn, 'final_norm': final_norm,
        'rg_w_in': rg_w_in.astype(BF16), 'rg_conv_w': rg_conv_w, 'rg_conv_b': rg_conv_b,
        'rg_w_ax': jnp.concatenate([rg_w_a, rg_w_x], axis=-1).astype(BF16),
        'rg_b_a': rg_b_a, 'rg_b_x': rg_b_x, 'rg_lambda': rg_lambda,
        'rg_w_out': rg_w_out.astype(BF16), 'kv_norm': kv_norm, 'w_kv': w_kv.astype(BF16),
        'attn_w_q': attn_w_q.astype(BF16), 'attn_w_o': attn_w_o.astype(BF16),
        'ffn_w_up': ffn_w_up.astype(BF16), 'ffn_conv_w': ffn_conv_w, 'ffn_conv_b': ffn_conv_b,
        'ffn_w_down': ffn_w_down.astype(BF16),
    }
    lam_vecs = lambda j: (lam_q1[j][None], lam_k1[j][None], lam_q2[j][None], lam_k2[j][None])

    tq_p = _pick_tile(t_p, 512)
    tk_p = _pick_tile(t_p, 512)
    bias_p = prompt_bias(rel_bias, tq_p, tk_p)

    vt_cache = {}

    def attn_prompt(j, l, q, kb, vb):
        if 'vt' not in vt_cache:
            vt = vb.reshape(bp, t_p, n_heads, 2 * hd).transpose(0, 2, 3, 1)
            extra = jnp.zeros((bp, n_heads, 16, t_p), BF16).at[:, :, 0, :].set(1.0)
            vt_cache['vt'] = jnp.concatenate([vt, extra], axis=2)
        o = prompt_attention(q.reshape(bp, t_p, dk), kb.reshape(bp, t_p, dk), vt_cache['vt'],
                             bias_p, *lam_vecs(j), attn_subln[j][:, None], tq=tq_p, tk=tk_p,
                             lam_init=_lam_init(l))
        return o.reshape(bp * t_p, dk)
    attn_prompt.q_dtype = BF16
    attn_prompt.q_scale = ATTN_SCALE * LOG2E

    cw_rg = rg_conv_w.shape[1]
    cw_ffn = ffn_conv_w.shape[1]
    yp, hp, cbp, fbp, kp, vp = _run_trunk(
        x_prompt, 0, 1,
        jnp.zeros((n_a, bp, 1, d), F32), jnp.zeros((n_a, bp, cw_rg - 1, d), F32),
        jnp.zeros((depth, bp, cw_ffn - 1, f2), F32), w, attn_prompt)

    bias_s = sample_bias(rel_bias, tq, page)
    ck = cache_k.transpose(0, 2, 3, 4, 1).reshape(n_pool, dk, page)
    cv = cache_v.reshape(n_pool, page * n_heads, 2 * hd)
    n_new = 16

    def to_seq_major(a2):
        return a2.reshape(tq, n_seq, -1).transpose(1, 0, 2)

    def attn_sample(j, l, q, kb, vb):
        q4 = to_seq_major(q)
        q8 = jnp.concatenate([q4, q4], axis=1)
        pad = ((0, 0), (0, n_new - tq), (0, 0))
        o = sample_attention(page_table, ck, cv, q8, jnp.pad(to_seq_major(kb), pad),
                             jnp.pad(to_seq_major(vb), pad), bias_s, *lam_vecs(j), attn_subln[j][None],
                             tq=tq, lam_init=_lam_init(l))
        return o.transpose(1, 0, 2).reshape(tq * n_seq, dk)
    attn_sample.q_dtype = F32
    attn_sample.q_scale = ATTN_SCALE

    def tmajor(a):
        return a.transpose(0, 2, 1, 3).reshape(a.shape[0], 1, a.shape[2] * n_seq, a.shape[3])

    def smajor(a2, steps):
        return a2.reshape(steps, n_seq, -1).transpose(1, 0, 2)

    xs = x_sample.transpose(1, 0, 2).reshape(1, tq * n_seq, d)
    ys, hs, cbs, fbs, ks, vs = _run_trunk(
        xs, past_len, n_seq, state_rglru_h[:, None], tmajor(state_rglru_conv), tmajor(state_ffn_conv),
        w, attn_sample)

    return (yp,
            smajor(ys, tq),
            jnp.stack([h[:, 0] for h in hp]),
            jnp.stack(cbp),
            jnp.stack(fbp),
            kp.reshape(bp, n_heads, 2, hd, t_p).transpose(0, 4, 1, 2, 3),
            vp.reshape(bp, t_p, n_heads, 2 * hd),
            jnp.stack([h[0] for h in hs]),
            jnp.stack([smajor(c, cw_rg - 1) for c in cbs]),
            jnp.stack([smajor(fb, cw_ffn - 1) for fb in fbs]),
            ks.reshape(n_heads, 2, hd, tq, n_seq).transpose(4, 3, 0, 1, 2),
            smajor(vs[None], tq).reshape(n_seq, tq, n_heads, 2 * hd))
```

```python
import functools
import math

import jax
import jax.numpy as jnp
from jax import lax
from jax.experimental import pallas as pl
from jax.experimental.pallas import tpu as pltpu

EPS = 1e-6
RG_C = 8.0
HEAD_DIM = 64
N_BUCKETS = 32
MAX_DISTANCE = 128
NEG_INF = -1e30
ATTN_SCALE = HEAD_DIM ** -0.5
LOG2E = math.log2(math.e)

V7X_LANES = 128
V7X_SUBLANES = 8
V7X_VMEM_BYTES = 64 * 2 ** 20
VMEM_LIMIT = V7X_VMEM_BYTES - 8 * 2 ** 20

BF16 = jnp.bfloat16
F32 = jnp.float32


def _bucket_starts():
    max_exact = N_BUCKETS // 2
    starts = list(range(max_exact + 1))
    for b in range(max_exact + 1, N_BUCKETS):
        n = starts[-1]
        while True:
            n += 1
            large = max_exact + int(math.log(n / max_exact) / math.log(MAX_DISTANCE / max_exact)
                                    * (N_BUCKETS - max_exact))
            if min(large, N_BUCKETS - 1) >= b:
                break
        starts.append(n)
    return tuple(starts)


BUCKET_STARTS = _bucket_starts()
FAR_DISTANCE = BUCKET_STARTS[-1]


def _cparams(sem, vmem=VMEM_LIMIT):
    return pltpu.CompilerParams(dimension_semantics=sem, vmem_limit_bytes=vmem)


def _rms(x, g):
    return x * lax.rsqrt(jnp.mean(x * x, axis=-1, keepdims=True) + EPS) * g


def _dot(a, b):
    return jnp.dot(a, b, preferred_element_type=F32)


def _dot_nt(a, b):
    return lax.dot_general(a, b, (((1,), (1,)), ((), ())), preferred_element_type=F32)


def _shift_rows(u, prev, k):
    n = prev.shape[0]
    rolled = pltpu.roll(u, k, 0)
    head = rolled[:V7X_SUBLANES]
    row = lax.broadcasted_iota(jnp.int32, head.shape, 0)
    for t in range(k):
        head = jnp.where(row == t, prev[n - k + t:n - k + t + 1, :], head)
    return jnp.concatenate([head, rolled[V7X_SUBLANES:]], axis=0)


def _norm_mm_kernel(x_ref, g_ref, w_ref, *out_refs, splits):
    xn = _rms(x_ref[...], g_ref[...]).astype(BF16)
    y = _dot(xn, w_ref[...])
    for o_ref, (lo, hi, scale, transposed) in zip(out_refs, splits):
        part = y[:, lo:hi]
        if scale != 1.0:
            part = part * scale
        if transposed:
            o_ref[0] = part.T.astype(o_ref.dtype)
        else:
            o_ref[...] = part.astype(o_ref.dtype)


def norm_matmul(x, g, w, splits, dtypes, tm, rows_per_seq=None):
    r, d = x.shape
    n = w.shape[1]
    out_shape, out_specs = [], []
    for (lo, hi, _, transposed), dt in zip(splits, dtypes):
        if transposed:
            nt = rows_per_seq // tm
            out_shape.append(jax.ShapeDtypeStruct((r // rows_per_seq, hi - lo, rows_per_seq), dt))
            out_specs.append(pl.BlockSpec((1, hi - lo, tm), lambda i, nt=nt: (i // nt, 0, i % nt)))
        else:
            out_shape.append(jax.ShapeDtypeStruct((r, hi - lo), dt))
            out_specs.append(pl.BlockSpec((tm, hi - lo), lambda i: (i, 0)))
    return pl.pallas_call(
        functools.partial(_norm_mm_kernel, splits=tuple(splits)),
        grid=(r // tm,),
        in_specs=[pl.BlockSpec((tm, d), lambda i: (i, 0)),
                  pl.BlockSpec((1, d), lambda i: (0, 0)),
                  pl.BlockSpec((d, n), lambda i: (0, 0))],
        out_specs=out_specs,
        out_shape=out_shape,
        name="norm_matmul",
        compiler_params=_cparams(("parallel",)),
    )(x, g, w)


def _res_mm_kernel(x_ref, a_ref, w_ref, o_ref):
    o_ref[...] = x_ref[...] + _dot(a_ref[...].astype(BF16), w_ref[...])


def residual_matmul(x, a, w, tm):
    r, d = x.shape
    k = a.shape[1]
    return pl.pallas_call(
        _res_mm_kernel,
        grid=(r // tm,),
        in_specs=[pl.BlockSpec((tm, d), lambda i: (i, 0)),
                  pl.BlockSpec((tm, k), lambda i: (i, 0)),
                  pl.BlockSpec((k, d), lambda i: (0, 0))],
        out_specs=pl.BlockSpec((tm, d), lambda i: (i, 0)),
        out_shape=jax.ShapeDtypeStruct((r, d), F32),
        name="residual_matmul",
        compiler_params=_cparams(("parallel",)),
    )(x, a, w)


def _rg_kernel(x_ref, h0_ref, cb0_ref, g_ref, win_ref, cw_ref, cbias_ref, wax_ref, ba_ref, bx_ref,
               lam_ref, wout_ref, out_ref, hlast_ref, cbout_ref,
               rbuf, a_scr, b_scr, h_scr, *, tm, ts, pos0, conv_w, n_blocks):
    ti = pl.program_id(1)
    d = a_scr.shape[1]
    hdr = rbuf.shape[0] - tm
    nprev = (conv_w - 1) * ts

    @pl.when(ti == 0)
    def _():
        rbuf[hdr - nprev:hdr, :] = cb0_ref[0]
        h_scr[...] = h0_ref[0]

    x = x_ref[0]
    xn = _rms(x, g_ref[...]).astype(BF16)
    u = _dot(xn, win_ref[...])
    gate = u[:, :d]
    rbuf[hdr:hdr + tm, :] = u[:, d:]
    xc = cbias_ref[...] + rbuf[hdr:hdr + tm, :] * cw_ref[conv_w - 1:conv_w, :]
    for j in range(conv_w - 1):
        off = hdr - (conv_w - 1 - j) * ts
        xc = xc + rbuf[off:off + tm, :] * cw_ref[j:j + 1, :]
    new_prev = rbuf[hdr + tm - nprev:hdr + tm, :]
    cbout_ref[0] = new_prev
    rbuf[hdr - nprev:hdr, :] = new_prev

    xcb = xc.astype(BF16)
    blk = d // n_blocks
    ras, rxs = [], []
    for n in range(n_blocks):
        rr = _dot(xcb[:, n * blk:(n + 1) * blk], wax_ref[n])
        ras.append(rr[:, :blk])
        rxs.append(rr[:, blk:])
    r = jax.nn.sigmoid(jnp.concatenate(ras, axis=1) + ba_ref[...])
    i = jax.nn.sigmoid(jnp.concatenate(rxs, axis=1) + bx_ref[...])
    z = -lam_ref[...]
    softplus = jnp.maximum(z, 0.0) + jnp.log(1.0 + jnp.exp(-jnp.abs(z)))
    log_a = (-RG_C) * r * softplus
    a = jnp.exp(log_a)
    mult = jnp.sqrt(1.0 - a * a)
    if pos0 == 0:
        row = lax.broadcasted_iota(jnp.int32, (tm, 1), 0) + ti * tm
        reset = row < ts
        a = jnp.where(reset, 0.0, a)
        mult = jnp.where(reset, 1.0, mult)
    a_scr[...] = a
    b_scr[...] = mult * i * xc

    if ts == 1:
        def group(gi, h):
            base = pl.multiple_of(gi * V7X_SUBLANES, V7X_SUBLANES)
            for rr_ in range(V7X_SUBLANES):
                h = a_scr[pl.ds(base + rr_, 1), :] * h + b_scr[pl.ds(base + rr_, 1), :]
                b_scr[pl.ds(base + rr_, 1), :] = h
            return h
        h = lax.fori_loop(0, tm // V7X_SUBLANES, group, h_scr[...])
    else:
        h = h_scr[...]
        for t in range(tm // ts):
            h = a_scr[t * ts:(t + 1) * ts, :] * h + b_scr[t * ts:(t + 1) * ts, :]
            b_scr[t * ts:(t + 1) * ts, :] = h
    h_scr[...] = h
    hlast_ref[0] = h

    hg = (b_scr[...] * jax.nn.gelu(gate)).astype(BF16)
    out_ref[0] = x + _dot(hg, wout_ref[...])


def rglru_layer(x, h0, cb0, g, w_in, conv_w, conv_b, w_ax, b_a, b_x, lam, w_out, *, tm, ts, pos0):
    bk, r, d = x.shape
    cw = conv_w.shape[0]
    nprev = (cw - 1) * ts
    hdr = max(V7X_SUBLANES, nprev)
    n_blocks = w_ax.shape[0]
    const2 = lambda b, t: (0, 0)
    kern = functools.partial(_rg_kernel, tm=tm, ts=ts, pos0=pos0, conv_w=cw, n_blocks=n_blocks)
    return pl.pallas_call(
        kern,
        grid=(bk, r // tm),
        in_specs=[pl.BlockSpec((1, tm, d), lambda b, t: (b, t, 0)),
                  pl.BlockSpec((1, ts, d), lambda b, t: (b, 0, 0)),
                  pl.BlockSpec((1, nprev, d), lambda b, t: (b, 0, 0)),
                  pl.BlockSpec((1, d), const2),
                  pl.BlockSpec((d, 2 * d), const2),
                  pl.BlockSpec((cw, d), const2),
                  pl.BlockSpec((1, d), const2),
                  pl.BlockSpec(w_ax.shape, lambda b, t: (0, 0, 0)),
                  pl.BlockSpec((1, d), const2),
                  pl.BlockSpec((1, d), const2),
                  pl.BlockSpec((1, d), const2),
                  pl.BlockSpec((d, d), const2)],
        out_specs=[pl.BlockSpec((1, tm, d), lambda b, t: (b, t, 0)),
                   pl.BlockSpec((1, ts, d), lambda b, t: (b, 0, 0)),
                   pl.BlockSpec((1, nprev, d), lambda b, t: (b, 0, 0))],
        out_shape=[jax.ShapeDtypeStruct((bk, r, d), F32),
                   jax.ShapeDtypeStruct((bk, ts, d), F32),
                   jax.ShapeDtypeStruct((bk, nprev, d), F32)],
        scratch_shapes=[pltpu.VMEM((hdr + tm, d), F32),
                        pltpu.VMEM((tm, d), F32),
                        pltpu.VMEM((tm, d), F32),
                        pltpu.VMEM((ts, d), F32)],
        name="rglru_layer",
        compiler_params=_cparams(("parallel", "arbitrary")),
    )(x, h0, cb0, g, w_in, conv_w, conv_b, w_ax, b_a, b_x, lam, w_out)


def _ffn_kernel(x_ref, fb0_ref, g_ref, wup_ref, cw_ref, cb_ref, wd_ref, gfin_ref, out_ref, fb_ref,
                xn_scr, acc_scr, u_scr, h_scr, carry, *, tm, tn, ts, conv_w, final_norm):
    ti = pl.program_id(1)
    f = wd_ref.shape[0]
    nc = f // tn
    hdr = u_scr.shape[2] - tm
    nprev = (conv_w - 1) * ts

    xn_scr[...] = _rms(x_ref[0], g_ref[...]).astype(BF16)

    def stage_up(c, slot):
        xn = xn_scr[...]
        for half in range(2):
            col = half * f + c * tn
            u_scr[slot, half, hdr:hdr + tm, :] = _dot(xn, wup_ref[:, col:col + tn])

    def conv(c, slot, half):
        col = half * f + c * tn
        prev = jnp.where(ti == 0, fb0_ref[0, :, col:col + tn], carry[half, c])
        cw = cw_ref[:, col:col + tn]
        if ts == 1:
            u = u_scr[slot, half]
            uc = cb_ref[:, col:col + tn] + u * cw[conv_w - 1:conv_w, :]
            for jj in range(conv_w - 1):
                uc = uc + _shift_rows(u, prev, conv_w - 1 - jj) * cw[jj:jj + 1, :]
            new_prev = u[tm - nprev:, :]
        else:
            u_scr[slot, half, :hdr, :] = prev
            uc = cb_ref[:, col:col + tn] + u_scr[slot, half, hdr:hdr + tm, :] * cw[conv_w - 1:conv_w, :]
            for jj in range(conv_w - 1):
                off = hdr - (conv_w - 1 - jj) * ts
                uc = uc + u_scr[slot, half, off:off + tm, :] * cw[jj:jj + 1, :]
            new_prev = u_scr[slot, half, tm:hdr + tm, :]
        carry[half, c] = new_prev
        fb_ref[0, 0, :, col:col + tn] = new_prev
        return uc

    def stage_gate(c, slot):
        val = conv(c, slot, 0)
        gte = conv(c, slot, 1)
        h_scr[slot] = (jax.nn.gelu(gte) * val).astype(BF16)

    def stage_down(c, slot):
        y = _dot(h_scr[slot], wd_ref[c * tn:(c + 1) * tn, :])
        if c == 0:
            acc_scr[...] = y
        else:
            acc_scr[...] += y

    stage_up(0, 0)
    stage_gate(0, 0)
    if nc > 1:
        stage_up(1, 1)
    for c in range(nc - 1):
        if c + 2 < nc:
            stage_up(c + 2, c % 2)
        stage_gate(c + 1, (c + 1) % 2)
        stage_down(c, c % 2)
    stage_down(nc - 1, (nc - 1) % 2)

    y = x_ref[0] + acc_scr[...]
    if final_norm:
        y = _rms(y, gfin_ref[...])
    out_ref[0] = y


def ffn_layer(x, fb0, g, w_up, conv_w, conv_b, w_down, g_final, *, tm, tn, ts, final_norm):
    bk, r, d = x.shape
    f = w_down.shape[0]
    cw = conv_w.shape[0]
    nprev = (cw - 1) * ts
    hdr = 0 if ts == 1 else nprev
    nc = f // tn
    const2 = lambda b, t: (0, 0)
    resident = pl.Buffered(1)
    kern = functools.partial(_ffn_kernel, tm=tm, tn=tn, ts=ts, conv_w=cw, final_norm=final_norm)
    out, fb = pl.pallas_call(
        kern,
        grid=(bk, r // tm),
        in_specs=[pl.BlockSpec((1, tm, d), lambda b, t: (b, t, 0)),
                  pl.BlockSpec((1, nprev, 2 * f), lambda b, t: (b, 0, 0)),
                  pl.BlockSpec((1, d), const2),
                  pl.BlockSpec((d, 2 * f), const2, pipeline_mode=resident),
                  pl.BlockSpec((cw, 2 * f), const2),
                  pl.BlockSpec((1, 2 * f), const2),
                  pl.BlockSpec((f, d), const2, pipeline_mode=resident),
                  pl.BlockSpec((1, d), const2)],
        out_specs=[pl.BlockSpec((1, tm, d), lambda b, t: (b, t, 0)),
                   pl.BlockSpec((1, 1, nprev, 2 * f), lambda b, t: (b, t, 0, 0))],
        out_shape=[jax.ShapeDtypeStruct((bk, r, d), F32),
                   jax.ShapeDtypeStruct((bk, r // tm, nprev, 2 * f), F32)],
        scratch_shapes=[pltpu.VMEM((tm, d), BF16),
                        pltpu.VMEM((tm, d), F32),
                        pltpu.VMEM((2, 2, hdr + tm, tn), F32),
                        pltpu.VMEM((2, tm, tn), BF16),
                        pltpu.VMEM((2, nc, nprev, tn), F32)],
        name="ffn_layer",
        compiler_params=_cparams(("parallel", "arbitrary")),
    )(x, fb0, g, w_up, conv_w, conv_b, w_down, g_final)
    return out, fb[:, -1]


def _shifted_bias(rel, rb_ref, h, scale=1.0):
    last = rb_ref[N_BUCKETS - 1, h]
    val = jnp.zeros(rel.shape, F32)
    for b in range(N_BUCKETS - 2, -1, -1):
        val = jnp.where(rel < BUCKET_STARTS[b + 1], (rb_ref[b, h] - last) * scale, val)
    return jnp.where(rel >= 0, val, NEG_INF)


def _near_offsets(tq, tk):
    step = math.gcd(tq, tk)
    lo = -(tq - 1)
    hi = tk - 1 + FAR_DISTANCE - 1
    first = -((-lo) // step) * step
    if first < lo:
        first += step
    count = (hi - first) // step + 1
    return first, step, count


def _prompt_bias_kernel(rb_ref, o_ref, *, tq, tk, first, step):
    h = pl.program_id(0)
    o = pl.program_id(1)
    rel = (lax.broadcasted_iota(jnp.int32, (tk, tq), 1)
           - lax.broadcasted_iota(jnp.int32, (tk, tq), 0) + (first + o * step))
    o_ref[0, 0] = _shifted_bias(rel, rb_ref, h, LOG2E)


def prompt_bias(rel_bias, tq, tk):
    n_heads = rel_bias.shape[1]
    first, step, count = _near_offsets(tq, tk)
    return pl.pallas_call(
        functools.partial(_prompt_bias_kernel, tq=tq, tk=tk, first=first, step=step),
        grid=(n_heads, count),
        in_specs=[pl.BlockSpec(memory_space=pltpu.SMEM)],
        out_specs=pl.BlockSpec((1, 1, tk, tq), lambda h, o: (h, o, 0, 0)),
        out_shape=jax.ShapeDtypeStruct((n_heads, count, tk, tq), F32),
        name="prompt_bias",
        compiler_params=_cparams(("parallel", "parallel")),
    )(rel_bias)


def _sample_bias_kernel(rb_ref, o_ref, *, tq, page, n_heads):
    rows = 2 * tq
    row = lax.broadcasted_iota(jnp.int32, (rows, 2 * page), 0)
    col = lax.broadcasted_iota(jnp.int32, (rows, 2 * page), 1)
    t = row % tq
    rel = jnp.where(col < page, page + t - col, t - (col - page))
    for h in range(n_heads):
        o_ref[h * rows:(h + 1) * rows, :] = _shifted_bias(rel, rb_ref, h)


def sample_bias(rel_bias, tq, page):
    n_heads = rel_bias.shape[1]
    return pl.pallas_call(
        functools.partial(_sample_bias_kernel, tq=tq, page=page, n_heads=n_heads),
        in_specs=[pl.BlockSpec(memory_space=pltpu.SMEM)],
        out_specs=pl.BlockSpec(memory_space=pltpu.VMEM),
        out_shape=jax.ShapeDtypeStruct((n_heads * 2 * tq, 2 * page), F32),
        name="sample_bias",
    )(rel_bias)


def _lambda(lq1_ref, lk1_ref, lq2_ref, lk2_ref, lam_init):
    s1 = jnp.sum(lq1_ref[...] * lk1_ref[...], axis=1, keepdims=True)
    s2 = jnp.sum(lq2_ref[...] * lk2_ref[...], axis=1, keepdims=True)
    return jnp.exp(s1) - jnp.exp(s2) + lam_init


def _block_schedule(tq, tk, nq):
    first, step, count = _near_offsets(tq, tk)
    hi = first + (count - 1) * step
    far, near = [], []
    for qi in range(nq):
        n_far = max(qi * tq - hi + tk - 1, 0) // tk
        k_end = (qi * tq + tq - 1) // tk + 1
        far += [(kj, qi, count) for kj in range(n_far)]
        near += [(kj, qi, (qi * tq - kj * tk - first) // step) for kj in range(n_far, k_end)]
    order = sorted(far) + sorted(near)
    order += [order[-1]] * 2
    return ([o[1] for o in order], [o[0] for o in order], [o[2] for o in order], len(far))


def _attn_kernel(sq_ref, sk_ref, sb_ref, q_ref, k_ref, vt_ref, bias_ref, lq1_ref, lk1_ref, lq2_ref,
                 lk2_ref, subln_ref, o_ref, qs_scr, m_scr, acc_scr, s_scr, p_scr, al_scr,
                 *, tq, tk, n_blocks, n_far, lam_init):
    hw = 2 * HEAD_DIM
    nq = qs_scr.shape[0]

    def prepare(qi, carry):
        q = q_ref[0, pl.ds(pl.multiple_of(qi * tq, tq), tq), :]
        lane = lax.broadcasted_iota(jnp.int32, q.shape, 1)
        qs_scr[qi, :tq, :] = jnp.where(lane < HEAD_DIM, q, jnp.zeros_like(q))
        qs_scr[qi, tq:, :] = jnp.where(lane >= HEAD_DIM, q, jnp.zeros_like(q))
        m_scr[qi] = jnp.full(m_scr.shape[1:], NEG_INF, F32)
        acc_scr[qi] = jnp.zeros(acc_scr.shape[1:], F32)
        return carry

    lax.fori_loop(0, nq, prepare, 0)

    def stage_scores(i, slot):
        start = pl.multiple_of(sk_ref[i] * tk, tk)
        k = k_ref[0, pl.ds(start, tk), :]
        s_scr[slot] = _dot_nt(k, qs_scr[sq_ref[i]])

    kc = 32

    def stage_probs(i, slot, far):
        qi = sq_ref[i]

        def chunk(c):
            s = s_scr[slot, c * kc:(c + 1) * kc, :]
            if not far:
                bias = bias_ref[0, sb_ref[i], c * kc:(c + 1) * kc, :]
                s = s + jnp.concatenate([bias, bias], axis=1)
            return s

        m_prev = m_scr[qi]
        m_rows = None
        for c in range(tk // kc):
            part = jnp.max(chunk(c).reshape(kc // V7X_SUBLANES, V7X_SUBLANES, 2 * tq), axis=0)
            m_rows = part if m_rows is None else jnp.maximum(m_rows, part)
        m_new = jnp.maximum(m_prev, jnp.max(m_rows, axis=0, keepdims=True))
        m_scr[qi] = m_new
        for c in range(tk // kc):
            p_scr[slot, c * kc:(c + 1) * kc, :] = jnp.exp2(chunk(c) - m_new).astype(BF16)
        al_scr[slot] = jnp.exp2(m_prev - m_new)

    def stage_values(i, slot):
        qi = sq_ref[i]
        start = pl.multiple_of(sk_ref[i] * tk, tk)
        vt = vt_ref[0, 0, :, pl.ds(start, tk)]
        acc_scr[qi] = al_scr[slot] * acc_scr[qi] + _dot(vt, p_scr[slot])

    n_slots = s_scr.shape[0]
    assert n_slots == 4 and n_blocks >= n_slots

    def pipe_step(i, r, far, with_scores=True, with_probs=True):
        if with_probs:
            stage_probs(i + 2, (r + 2) % n_slots, far)
        if with_scores:
            stage_scores(i + 4, r)
        stage_values(i, r)

    def run_steps(lo, hi, far):
        while lo < hi and lo % n_slots:
            pipe_step(lo, lo % n_slots, far)
            lo += 1
        n_groups = max(hi - lo, 0) // n_slots

        def group(ii, carry):
            for r in range(n_slots):
                pipe_step(lo + n_slots * ii + r, r, far)
            return carry

        if n_groups > 0:
            lax.fori_loop(0, n_groups, group, 0)
        for i in range(lo + n_slots * n_groups, hi):
            pipe_step(i, i % n_slots, far)

    for i in range(4):
        stage_scores(i, i)
    for i in range(2):
        stage_probs(i, i, i < n_far)
    far_steps = max(n_far - 2, 0)
    run_steps(0, far_steps, True)
    run_steps(far_steps, n_blocks - 4, False)
    for i in range(n_blocks - 4, n_blocks):
        pipe_step(i, i % n_slots, False, with_scores=False, with_probs=i + 2 < n_blocks)

    lam = _lambda(lq1_ref, lk1_ref, lq2_ref, lk2_ref, lam_init)

    def finish(qi, carry):
        acc = acc_scr[qi]
        on = acc[:hw, :] / acc[hw:hw + 1, :]
        ot = on[:, :tq] - lam * on[:, tq:]
        ot = ot * lax.rsqrt(jnp.mean(ot * ot, axis=0, keepdims=True) + EPS)
        ot = ot * subln_ref[...] * (1.0 - lam_init)
        o_ref[0, pl.ds(pl.multiple_of(qi * tq, tq), tq), :] = ot.T.astype(o_ref.dtype)
        return carry

    lax.fori_loop(0, nq, finish, 0)


def prompt_attention(q, k, vt, bias, lq1, lk1, lq2, lk2, subln, *, tq, tk, lam_init):
    b, t, dk = q.shape
    hw = 2 * HEAD_DIM
    n_heads = dk // hw
    vrows = vt.shape[2]
    nq = t // tq
    sched_q, sched_k, sched_b, n_far = _block_schedule(tq, tk, nq)
    n_blocks = len(sched_q) - 2
    vec = lambda shape: pl.BlockSpec(shape, lambda bb, h, *_: (0, 0))
    grid_spec = pltpu.PrefetchScalarGridSpec(
        num_scalar_prefetch=3,
        grid=(b, n_heads),
        in_specs=[pl.BlockSpec((1, t, hw), lambda bb, h, *_: (bb, 0, h), pipeline_mode=pl.Buffered(1)),
                  pl.BlockSpec((1, t, hw), lambda bb, h, *_: (bb, 0, h)),
                  pl.BlockSpec((1, 1, vrows, t), lambda bb, h, *_: (bb, h, 0, 0)),
                  pl.BlockSpec((1,) + bias.shape[1:], lambda bb, h, *_: (h, 0, 0, 0),
                               pipeline_mode=pl.Buffered(1)),
                  vec((1, HEAD_DIM)), vec((1, HEAD_DIM)), vec((1, HEAD_DIM)), vec((1, HEAD_DIM)),
                  vec((hw, 1))],
        out_specs=pl.BlockSpec((1, t, hw), lambda bb, h, *_: (bb, 0, h)),
        scratch_shapes=[pltpu.VMEM((nq, 2 * tq, hw), BF16),
                        pltpu.VMEM((nq, 1, 2 * tq), F32),
                        pltpu.VMEM((nq, vrows, 2 * tq), F32),
                        pltpu.VMEM((4, tk, 2 * tq), F32),
                        pltpu.VMEM((4, tk, 2 * tq), BF16),
                        pltpu.VMEM((4, 1, 2 * tq), F32)])
    return pl.pallas_call(
        functools.partial(_attn_kernel, tq=tq, tk=tk, n_blocks=n_blocks, n_far=n_far, lam_init=lam_init),
        grid_spec=grid_spec,
        out_shape=jax.ShapeDtypeStruct((b, t, dk), BF16),
        name="prompt_attention",
        compiler_params=_cparams(("parallel", "parallel")),
    )(jnp.asarray(sched_q, jnp.int32), jnp.asarray(sched_k, jnp.int32), jnp.asarray(sched_b, jnp.int32),
      q, k, vt, bias, lq1, lk1, lq2, lk2, subln)


def _sample_attn_kernel(pt_ref, *refs, n_pages, page, tq, n_heads, lam_init):
    k_refs = refs[:n_pages]
    v_refs = refs[n_pages:2 * n_pages]
    (q_ref, kn_ref, vn_ref, bias_ref, lq1_ref, lk1_ref, lq2_ref, lk2_ref, subln_ref,
     o_ref, kt_scr, kn_scr, vbf) = refs[2 * n_pages:]
    past = n_pages * page
    hw = 2 * HEAD_DIM
    rows = n_heads * 2 * tq
    dk = n_heads * hw
    nn = kn_ref.shape[1]

    @pl.when(pl.program_id(0) == 0)
    def _():
        kn_scr[...] = jnp.zeros_like(kn_scr)
        for h in range(n_heads):
            vbf[h, past:, :] = jnp.zeros((page, hw), BF16)

    for p in range(n_pages):
        kt_scr[:, p * page:(p + 1) * page] = k_refs[p][0].astype(BF16)
        for h in range(n_heads):
            vbf[h, p * page:(p + 1) * page, :] = (
                v_refs[p][0, pl.ds(h, page, stride=n_heads), :].astype(BF16))
    kn_scr[:nn, :] = kn_ref[0]
    vn = vn_ref[0]
    for h in range(n_heads):
        vbf[h, past:past + nn, :] = vn[:, h * hw:(h + 1) * hw]

    qrep = jnp.concatenate([q_ref[0]] * n_heads, axis=0)
    rr = lax.broadcasted_iota(jnp.int32, (rows, dk), 0)
    cc = lax.broadcasted_iota(jnp.int32, (rows, dk), 1)
    qbd = jnp.where(cc // HEAD_DIM == rr // tq, qrep, 0.0).astype(BF16)

    s_main = _dot(qbd, kt_scr[:, :past - page])
    s_last = _dot(qbd, kt_scr[:, past - page:]) + bias_ref[:, :page]
    s_new = _dot_nt(qbd, kn_scr[...]) + bias_ref[:, page:]
    m = jnp.maximum(jnp.max(s_main, axis=1, keepdims=True),
                    jnp.max(jnp.maximum(s_last, s_new), axis=1, keepdims=True))
    p_all = jnp.concatenate([jnp.exp(s_main - m), jnp.exp(s_last - m), jnp.exp(s_new - m)], axis=1)
    l = jnp.sum(p_all, axis=1, keepdims=True)
    p_bf = p_all.astype(BF16)
    lam = _lambda(lq1_ref, lk1_ref, lq2_ref, lk2_ref, lam_init)
    outs = []
    for h in range(n_heads):
        r0 = h * 2 * tq
        o8 = _dot(p_bf[r0:r0 + 2 * tq], vbf[h]) / l[r0:r0 + 2 * tq]
        o = o8[:tq] - lam * o8[tq:]
        outs.append(_rms(o, subln_ref[...]) * (1.0 - lam_init))
    o_ref[0] = jnp.concatenate(outs, axis=1)


def sample_attention(page_table, cache_kt, cache_v, q8, k_new, v_new, bias, lq1, lk1, lq2, lk2, subln,
                     *, tq, lam_init):
    n_seq, n_pages = page_table.shape
    _, dk, page = cache_kt.shape
    hw = 2 * HEAD_DIM
    n_heads = dk // hw
    nn = k_new.shape[1]
    past = n_pages * page

    def page_spec(p, shape):
        return pl.BlockSpec((1,) + shape, lambda b, pt: (pt[b * n_pages + p], 0, 0))

    vec = lambda shape: pl.BlockSpec(shape, lambda b, pt: (0, 0))
    in_specs = ([page_spec(p, (dk, page)) for p in range(n_pages)]
                + [page_spec(p, (page * n_heads, hw)) for p in range(n_pages)]
                + [pl.BlockSpec((1, 2 * tq, dk), lambda b, pt: (b, 0, 0)),
                   pl.BlockSpec((1, nn, dk), lambda b, pt: (b, 0, 0)),
                   pl.BlockSpec((1, nn, dk), lambda b, pt: (b, 0, 0)),
                   vec(bias.shape),
                   vec((1, HEAD_DIM)), vec((1, HEAD_DIM)), vec((1, HEAD_DIM)), vec((1, HEAD_DIM)),
                   vec((1, hw))])
    grid_spec = pltpu.PrefetchScalarGridSpec(
        num_scalar_prefetch=1,
        grid=(n_seq,),
        in_specs=in_specs,
        out_specs=pl.BlockSpec((1, tq, dk), lambda b, pt: (b, 0, 0)),
        scratch_shapes=[pltpu.VMEM((dk, past), BF16),
                        pltpu.VMEM((page, dk), BF16),
                        pltpu.VMEM((n_heads, past + page, hw), BF16)])
    kern = functools.partial(_sample_attn_kernel, n_pages=n_pages, page=page, tq=tq,
                             n_heads=n_heads, lam_init=lam_init)
    return pl.pallas_call(
        kern,
        grid_spec=grid_spec,
        out_shape=jax.ShapeDtypeStruct((n_seq, tq, dk), F32),
        name="sample_attention",
        compiler_params=_cparams(("arbitrary",)),
    )(page_table.reshape(-1), *([cache_kt] * n_pages), *([cache_v] * n_pages),
      q8, k_new, v_new, bias, lq1, lk1, lq2, lk2, subln)


def _pick_tile(n, target):
    t = min(n, target)
    while n % t:
        t //= 2
    return t


def _run_trunk(x, pos0, ts, rg_h0, rg_cb0, ffn_fb0, w, attn_fn):
    bk, r, d = x.shape
    depth = w['norm_mix'].shape[0]
    n_a = w['rg_w_in'].shape[0]
    tm_rg = _pick_tile(r, 256)
    tm_ffn = _pick_tile(r, 512)
    tm_mm = _pick_tile(bk * r, 512)
    dk = w['attn_w_q'].shape[2]
    new_h, new_cb, new_fb = [], [], []
    k_new = v_new = kb = vb = None
    for l in range(depth):
        if l < n_a:
            x, h_last, cb = rglru_layer(
                x, rg_h0[l], rg_cb0[l], w['norm_mix'][l][None], w['rg_w_in'][l], w['rg_conv_w'][l],
                w['rg_conv_b'][l][None], w['rg_w_ax'][l], w['rg_b_a'][l][None], w['rg_b_x'][l][None],
                w['rg_lambda'][l][None], w['rg_w_out'][l], tm=tm_rg, ts=ts, pos0=pos0)
            new_h.append(h_last)
            new_cb.append(cb)
        else:
            x2 = x.reshape(bk * r, d)
            if l == n_a:
                k_new, v_new, kb, vb = norm_matmul(
                    x2, w['kv_norm'][None], w['w_kv'],
                    [(0, dk, 1.0, True), (dk, 2 * dk, 1.0, False), (0, dk, 1.0, False),
                     (dk, 2 * dk, 1.0, False)],
                    [F32, F32, BF16, BF16], tm_mm, rows_per_seq=r)
            j = l - n_a
            q_dt = attn_fn.q_dtype
            (q,) = norm_matmul(x2, w['norm_mix'][l][None], w['attn_w_q'][j],
                               [(0, dk, attn_fn.q_scale, False)], [q_dt], tm_mm)
            o = attn_fn(j, l, q, kb, vb)
            x = residual_matmul(x2, o, w['attn_w_o'][j], tm_mm).reshape(bk, r, d)
        last = l == depth - 1
        x, fb = ffn_layer(x, ffn_fb0[l], w['norm_ffn'][l][None], w['ffn_w_up'][l], w['ffn_conv_w'][l],
                          w['ffn_conv_b'][l][None], w['ffn_w_down'][l], w['final_norm'][None],
                          tm=tm_ffn, tn=512, ts=ts, final_norm=last)
        new_fb.append(fb)
    return x, new_h, new_cb, new_fb, k_new, v_new


def _lam_init(layer_idx):
    return 0.8 - 0.6 * math.exp(-0.3 * layer_idx)


def kernel(x_prompt, x_sample, state_rglru_h, state_rglru_conv, state_ffn_conv, cache_k, cache_v,
           page_table, rel_bias, norm_mix, norm_ffn, final_norm, rg_w_in, rg_conv_w, rg_conv_b,
           rg_w_a, rg_b_a, rg_w_x, rg_b_x, rg_lambda, rg_w_out, kv_norm, w_kv, attn_w_q,
           lam_q1, lam_k1, lam_q2, lam_k2, attn_subln, attn_w_o, ffn_w_up, ffn_conv_w,
           ffn_conv_b, ffn_w_down):
    bp, t_p, d = x_prompt.shape
    n_seq, tq, _ = x_sample.shape
    depth = norm_mix.shape[0]
    n_a = rg_w_in.shape[0]
    n_pool, page, n_heads, _, hd = cache_k.shape
    assert hd == HEAD_DIM
    dk = n_heads * 2 * hd
    f2 = ffn_w_up.shape[2]
    past_len = page_table.shape[1] * page
    assert page >= FAR_DISTANCE and past_len >= 2 * page and 2 * tq == V7X_SUBLANES

    w = {
        'norm_mix': norm_mix, 'norm_ffn': norm_ffn, 'final_norm': final_norm,
        'rg_w_in': rg_w_in.astype(BF16), 'rg_conv_w': rg_conv_w, 'rg_conv_b': rg_conv_b,
        'rg_w_ax': jnp.concatenate([rg_w_a, rg_w_x], axis=-1).astype(BF16),
        'rg_b_a': rg_b_a, 'rg_b_x': rg_b_x, 'rg_lambda': rg_lambda,
        'rg_w_out': rg_w_out.astype(BF16), 'kv_norm': kv_norm, 'w_kv': w_kv.astype(BF16),
        'attn_w_q': attn_w_q.astype(BF16), 'attn_w_o': attn_w_o.astype(BF16),
        'ffn_w_up': ffn_w_up.astype(BF16), 'ffn_conv_w': ffn_conv_w, 'ffn_conv_b': ffn_conv_b,
        'ffn_w_down': ffn_w_down.astype(BF16),
    }
    lam_vecs = lambda j: (lam_q1[j][None], lam_k1[j][None], lam_q2[j][None], lam_k2[j][None])

    tq_p = _pick_tile(t_p, 512)
    tk_p = _pick_tile(t_p, 512)
    bias_p = prompt_bias(rel_bias, tq_p, tk_p)

    vt_cache = {}

    def attn_prompt(j, l, q, kb, vb):
        if 'vt' not in vt_cache:
            vt = vb.reshape(bp, t_p, n_heads, 2 * hd).transpose(0, 2, 3, 1)
            extra = jnp.zeros((bp, n_heads, 16, t_p), BF16).at[:, :, 0, :].set(1.0)
            vt_cache['vt'] = jnp.concatenate([vt, extra], axis=2)
        o = prompt_attention(q.reshape(bp, t_p, dk), kb.reshape(bp, t_p, dk), vt_cache['vt'],
                             bias_p, *lam_vecs(j), attn_subln[j][:, None], tq=tq_p, tk=tk_p,
                             lam_init=_lam_init(l))
        return o.reshape(bp * t_p, dk)
    attn_prompt.q_dtype = BF16
    attn_prompt.q_scale = ATTN_SCALE * LOG2E

    cw_rg = rg_conv_w.shape[1]
    cw_ffn = ffn_conv_w.shape[1]
    yp, hp, cbp, fbp, kp, vp = _run_trunk(
        x_prompt, 0, 1,
        jnp.zeros((n_a, bp, 1, d), F32), jnp.zeros((n_a, bp, cw_rg - 1, d), F32),
        jnp.zeros((depth, bp, cw_ffn - 1, f2), F32), w, attn_prompt)

    bias_s = sample_bias(rel_bias, tq, page)
    ck = cache_k.transpose(0, 2, 3, 4, 1).reshape(n_pool, dk, page)
    cv = cache_v.reshape(n_pool, page * n_heads, 2 * hd)
    n_new = 16

    def to_seq_major(a2):
        return a2.reshape(tq, n_seq, -1).transpose(1, 0, 2)

    def attn_sample(j, l, q, kb, vb):
        q4 = to_seq_major(q)
        q8 = jnp.concatenate([q4, q4], axis=1)
        pad = ((0, 0), (0, n_new - tq), (0, 0))
        o = sample_attention(page_table, ck, cv, q8, jnp.pad(to_seq_major(kb), pad),
                             jnp.pad(to_seq_major(vb), pad), bias_s, *lam_vecs(j), attn_subln[j][None],
                             tq=tq, lam_init=_lam_init(l))
        return o.transpose(1, 0, 2).reshape(tq * n_seq, dk)
    attn_sample.q_dtype = F32
    attn_sample.q_scale = ATTN_SCALE

    def tmajor(a):
        return a.transpose(0, 2, 1, 3).reshape(a.shape[0], 1, a.shape[2] * n_seq, a.shape[3])

    def smajor(a2, steps):
        return a2.reshape(steps, n_seq, -1).transpose(1, 0, 2)

    xs = x_sample.transpose(1, 0, 2).reshape(1, tq * n_seq, d)
    ys, hs, cbs, fbs, ks, vs = _run_trunk(
        xs, past_len, n_seq, state_rglru_h[:, None], tmajor(state_rglru_conv), tmajor(state_ffn_conv),
        w, attn_sample)

    return (yp,
            smajor(ys, tq),
            jnp.stack([h[:, 0] for h in hp]),
            jnp.stack(cbp),
            jnp.stack(fbp),
            kp.reshape(bp, n_heads, 2, hd, t_p).transpose(0, 4, 1, 2, 3),
            vp.reshape(bp, t_p, n_heads, 2 * hd),
            jnp.stack([h[0] for h in hs]),
            jnp.stack([smajor(c, cw_rg - 1) for c in cbs]),
            jnp.stack([smajor(fb, cw_ffn - 1) for fb in fbs]),
            ks.reshape(n_heads, 2, hd, tq, n_seq).transpose(4, 3, 0, 1, 2),
            smajor(vs[None], tq).reshape(n_seq, tq, n_heads, 2 * hd))
```

```python
import functools
import math

import jax
import jax.numpy as jnp
from jax import lax
from jax.experimental import pallas as pl
from jax.experimental.pallas import tpu as pltpu

EPS = 1e-6
RG_C = 8.0
HEAD_DIM = 64
N_BUCKETS = 32
MAX_DISTANCE = 128
NEG_INF = -1e30
ATTN_SCALE = HEAD_DIM ** -0.5
LOG2E = math.log2(math.e)

V7X_LANES = 128
V7X_SUBLANES = 8
V7X_VMEM_BYTES = 64 * 2 ** 20
VMEM_LIMIT = V7X_VMEM_BYTES - 8 * 2 ** 20

BF16 = jnp.bfloat16
F32 = jnp.float32


def _bucket_starts():
    max_exact = N_BUCKETS // 2
    starts = list(range(max_exact + 1))
    for b in range(max_exact + 1, N_BUCKETS):
        n = starts[-1]
        while True:
            n += 1
            large = max_exact + int(math.log(n / max_exact) / math.log(MAX_DISTANCE / max_exact)
                                    * (N_BUCKETS - max_exact))
            if min(large, N_BUCKETS - 1) >= b:
                break
        starts.append(n)
    return tuple(starts)


BUCKET_STARTS = _bucket_starts()
FAR_DISTANCE = BUCKET_STARTS[-1]


def _cparams(sem, vmem=VMEM_LIMIT):
    return pltpu.CompilerParams(dimension_semantics=sem, vmem_limit_bytes=vmem)


def _rms(x, g):
    return x * lax.rsqrt(jnp.mean(x * x, axis=-1, keepdims=True) + EPS) * g


def _dot(a, b):
    return jnp.dot(a, b, preferred_element_type=F32)


def _dot_nt(a, b):
    return lax.dot_general(a, b, (((1,), (1,)), ((), ())), preferred_element_type=F32)


def _shift_rows(u, prev, k):
    n = prev.shape[0]
    rolled = pltpu.roll(u, k, 0)
    head = rolled[:V7X_SUBLANES]
    row = lax.broadcasted_iota(jnp.int32, head.shape, 0)
    for t in range(k):
        head = jnp.where(row == t, prev[n - k + t:n - k + t + 1, :], head)
    return jnp.concatenate([head, rolled[V7X_SUBLANES:]], axis=0)


def _norm_mm_kernel(x_ref, g_ref, w_ref, *out_refs, splits):
    xn = _rms(x_ref[...], g_ref[...]).astype(BF16)
    y = _dot(xn, w_ref[...])
    for o_ref, (lo, hi, scale, transposed) in zip(out_refs, splits):
        part = y[:, lo:hi]
        if scale != 1.0:
            part = part * scale
        if transposed:
            o_ref[0] = part.T.astype(o_ref.dtype)
        else:
            o_ref[...] = part.astype(o_ref.dtype)


def norm_matmul(x, g, w, splits, dtypes, tm, rows_per_seq=None):
    r, d = x.shape
    n = w.shape[1]
    out_shape, out_specs = [], []
    for (lo, hi, _, transposed), dt in zip(splits, dtypes):
        if transposed:
            nt = rows_per_seq // tm
            out_shape.append(jax.ShapeDtypeStruct((r // rows_per_seq, hi - lo, rows_per_seq), dt))
            out_specs.append(pl.BlockSpec((1, hi - lo, tm), lambda i, nt=nt: (i // nt, 0, i % nt)))
        else:
            out_shape.append(jax.ShapeDtypeStruct((r, hi - lo), dt))
            out_specs.append(pl.BlockSpec((tm, hi - lo), lambda i: (i, 0)))
    return pl.pallas_call(
        functools.partial(_norm_mm_kernel, splits=tuple(splits)),
        grid=(r // tm,),
        in_specs=[pl.BlockSpec((tm, d), lambda i: (i, 0)),
                  pl.BlockSpec((1, d), lambda i: (0, 0)),
                  pl.BlockSpec((d, n), lambda i: (0, 0))],
        out_specs=out_specs,
        out_shape=out_shape,
        name="norm_matmul",
        compiler_params=_cparams(("parallel",)),
    )(x, g, w)


def _res_mm_kernel(x_ref, a_ref, w_ref, o_ref):
    o_ref[...] = x_ref[...] + _dot(a_ref[...].astype(BF16), w_ref[...])


def residual_matmul(x, a, w, tm):
    r, d = x.shape
    k = a.shape[1]
    return pl.pallas_call(
        _res_mm_kernel,
        grid=(r // tm,),
        in_specs=[pl.BlockSpec((tm, d), lambda i: (i, 0)),
                  pl.BlockSpec((tm, k), lambda i: (i, 0)),
                  pl.BlockSpec((k, d), lambda i: (0, 0))],
        out_specs=pl.BlockSpec((tm, d), lambda i: (i, 0)),
        out_shape=jax.ShapeDtypeStruct((r, d), F32),
        name="residual_matmul",
        compiler_params=_cparams(("parallel",)),
    )(x, a, w)


def _rg_kernel(x_ref, h0_ref, cb0_ref, g_ref, win_ref, cw_ref, cbias_ref, wax_ref, ba_ref, bx_ref,
               lam_ref, wout_ref, out_ref, hlast_ref, cbout_ref,
               rbuf, a_scr, b_scr, h_scr, *, tm, ts, pos0, conv_w, n_blocks):
    ti = pl.program_id(1)
    d = a_scr.shape[1]
    hdr = rbuf.shape[0] - tm
    nprev = (conv_w - 1) * ts

    @pl.when(ti == 0)
    def _():
        rbuf[hdr - nprev:hdr, :] = cb0_ref[0]
        h_scr[...] = h0_ref[0]

    x = x_ref[0]
    xn = _rms(x, g_ref[...]).astype(BF16)
    u = _dot(xn, win_ref[...])
    gate = u[:, :d]
    rbuf[hdr:hdr + tm, :] = u[:, d:]
    xc = cbias_ref[...] + rbuf[hdr:hdr + tm, :] * cw_ref[conv_w - 1:conv_w, :]
    for j in range(conv_w - 1):
        off = hdr - (conv_w - 1 - j) * ts
        xc = xc + rbuf[off:off + tm, :] * cw_ref[j:j + 1, :]
    new_prev = rbuf[hdr + tm - nprev:hdr + tm, :]
    cbout_ref[0] = new_prev
    rbuf[hdr - nprev:hdr, :] = new_prev

    xcb = xc.astype(BF16)
    blk = d // n_blocks
    ras, rxs = [], []
    for n in range(n_blocks):
        rr = _dot(xcb[:, n * blk:(n + 1) * blk], wax_ref[n])
        ras.append(rr[:, :blk])
        rxs.append(rr[:, blk:])
    r = jax.nn.sigmoid(jnp.concatenate(ras, axis=1) + ba_ref[...])
    i = jax.nn.sigmoid(jnp.concatenate(rxs, axis=1) + bx_ref[...])
    z = -lam_ref[...]
    softplus = jnp.maximum(z, 0.0) + jnp.log(1.0 + jnp.exp(-jnp.abs(z)))
    log_a = (-RG_C) * r * softplus
    a = jnp.exp(log_a)
    mult = jnp.sqrt(1.0 - a * a)
    if pos0 == 0:
        row = lax.broadcasted_iota(jnp.int32, (tm, 1), 0) + ti * tm
        reset = row < ts
        a = jnp.where(reset, 0.0, a)
        mult = jnp.where(reset, 1.0, mult)
    a_scr[...] = a
    b_scr[...] = mult * i * xc

    if ts == 1:
        def group(gi, h):
            base = pl.multiple_of(gi * V7X_SUBLANES, V7X_SUBLANES)
            for rr_ in range(V7X_SUBLANES):
                h = a_scr[pl.ds(base + rr_, 1), :] * h + b_scr[pl.ds(base + rr_, 1), :]
                b_scr[pl.ds(base + rr_, 1), :] = h
            return h
        h = lax.fori_loop(0, tm // V7X_SUBLANES, group, h_scr[...])
    else:
        h = h_scr[...]
        for t in range(tm // ts):
            h = a_scr[t * ts:(t + 1) * ts, :] * h + b_scr[t * ts:(t + 1) * ts, :]
            b_scr[t * ts:(t + 1) * ts, :] = h
    h_scr[...] = h
    hlast_ref[0] = h

    hg = (b_scr[...] * jax.nn.gelu(gate)).astype(BF16)
    out_ref[0] = x + _dot(hg, wout_ref[...])


def rglru_layer(x, h0, cb0, g, w_in, conv_w, conv_b, w_ax, b_a, b_x, lam, w_out, *, tm, ts, pos0):
    bk, r, d = x.shape
    cw = conv_w.shape[0]
    nprev = (cw - 1) * ts
    hdr = max(V7X_SUBLANES, nprev)
    n_blocks = w_ax.shape[0]
    const2 = lambda b, t: (0, 0)
    kern = functools.partial(_rg_kernel, tm=tm, ts=ts, pos0=pos0, conv_w=cw, n_blocks=n_blocks)
    return pl.pallas_call(
        kern,
        grid=(bk, r // tm),
        in_specs=[pl.BlockSpec((1, tm, d), lambda b, t: (b, t, 0)),
                  pl.BlockSpec((1, ts, d), lambda b, t: (b, 0, 0)),
                  pl.BlockSpec((1, nprev, d), lambda b, t: (b, 0, 0)),
                  pl.BlockSpec((1, d), const2),
                  pl.BlockSpec((d, 2 * d), const2),
                  pl.BlockSpec((cw, d), const2),
                  pl.BlockSpec((1, d), const2),
                  pl.BlockSpec(w_ax.shape, lambda b, t: (0, 0, 0)),
                  pl.BlockSpec((1, d), const2),
                  pl.BlockSpec((1, d), const2),
                  pl.BlockSpec((1, d), const2),
                  pl.BlockSpec((d, d), const2)],
        out_specs=[pl.BlockSpec((1, tm, d), lambda b, t: (b, t, 0)),
                   pl.BlockSpec((1, ts, d), lambda b, t: (b, 0, 0)),
                   pl.BlockSpec((1, nprev, d), lambda b, t: (b, 0, 0))],
        out_shape=[jax.ShapeDtypeStruct((bk, r, d), F32),
                   jax.ShapeDtypeStruct((bk, ts, d), F32),
                   jax.ShapeDtypeStruct((bk, nprev, d), F32)],
        scratch_shapes=[pltpu.VMEM((hdr + tm, d), F32),
                        pltpu.VMEM((tm, d), F32),
                        pltpu.VMEM((tm, d), F32),
                        pltpu.VMEM((ts, d), F32)],
        name="rglru_layer",
        compiler_params=_cparams(("parallel", "arbitrary")),
    )(x, h0, cb0, g, w_in, conv_w, conv_b, w_ax, b_a, b_x, lam, w_out)


def _ffn_kernel(*refs, tm, tn, ts, conv_w, final_norm, has_mixer):
    if has_mixer:
        x_ref, attn_ref, wo_ref = refs[:3]
        refs = refs[3:]
    else:
        x_ref = refs[0]
        refs = refs[1:]
    (fb0_ref, g_ref, wup_ref, cw_ref, cb_ref, wd_ref, gfin_ref, out_ref, fb_ref,
     xn_scr, acc_scr, u_scr, h_scr, carry, xin_scr) = refs
    ti = pl.program_id(1)
    f = wd_ref.shape[0]
    nc = f // tn
    hdr = u_scr.shape[2] - tm
    nprev = (conv_w - 1) * ts

    xin = x_ref[0]
    if has_mixer:
        xin = xin + _dot(attn_ref[0].astype(BF16), wo_ref[...])
    xin_scr[...] = xin
    xn_scr[...] = _rms(xin, g_ref[...]).astype(BF16)

    def stage_up(c, slot):
        xn = xn_scr[...]
        for half in range(2):
            col = half * f + c * tn
            u_scr[slot, half, hdr:hdr + tm, :] = _dot(xn, wup_ref[:, col:col + tn])

    def conv(c, slot, half):
        col = half * f + c * tn
        prev = jnp.where(ti == 0, fb0_ref[0, :, col:col + tn], carry[half, c])
        cw = cw_ref[:, col:col + tn]
        if ts == 1:
            u = u_scr[slot, half]
            uc = cb_ref[:, col:col + tn] + u * cw[conv_w - 1:conv_w, :]
            for jj in range(conv_w - 1):
                uc = uc + _shift_rows(u, prev, conv_w - 1 - jj) * cw[jj:jj + 1, :]
            new_prev = u[tm - nprev:, :]
        else:
            u_scr[slot, half, :hdr, :] = prev
            uc = cb_ref[:, col:col + tn] + u_scr[slot, half, hdr:hdr + tm, :] * cw[conv_w - 1:conv_w, :]
            for jj in range(conv_w - 1):
                off = hdr - (conv_w - 1 - jj) * ts
                uc = uc + u_scr[slot, half, off:off + tm, :] * cw[jj:jj + 1, :]
            new_prev = u_scr[slot, half, tm:hdr + tm, :]
        carry[half, c] = new_prev
        fb_ref[0, 0, :, col:col + tn] = new_prev
        return uc

    def stage_gate(c, slot):
        val = conv(c, slot, 0)
        gte = conv(c, slot, 1)
        h_scr[slot] = (jax.nn.gelu(gte) * val).astype(BF16)

    def stage_down(c, slot):
        y = _dot(h_scr[slot], wd_ref[c * tn:(c + 1) * tn, :])
        if c == 0:
            acc_scr[...] = y
        else:
            acc_scr[...] += y

    stage_up(0, 0)
    stage_gate(0, 0)
    if nc > 1:
        stage_up(1, 1)
    for c in range(nc - 1):
        if c + 2 < nc:
            stage_up(c + 2, c % 2)
        stage_gate(c + 1, (c + 1) % 2)
        stage_down(c, c % 2)
    stage_down(nc - 1, (nc - 1) % 2)

    y = xin_scr[...] + acc_scr[...]
    if final_norm:
        y = _rms(y, gfin_ref[...])
    out_ref[0] = y


def ffn_layer(x, mixer, fb0, g, w_up, conv_w, conv_b, w_down, g_final, *, tm, tn, ts, final_norm):
    bk, r, d = x.shape
    f = w_down.shape[0]
    cw = conv_w.shape[0]
    nprev = (cw - 1) * ts
    hdr = 0 if ts == 1 else nprev
    nc = f // tn
    const2 = lambda b, t: (0, 0)
    resident = pl.Buffered(1)
    kern = functools.partial(_ffn_kernel, tm=tm, tn=tn, ts=ts, conv_w=cw, final_norm=final_norm,
                             has_mixer=mixer is not None)
    mixer_specs, mixer_args = [], []
    if mixer is not None:
        attn, w_o = mixer
        mixer_specs = [pl.BlockSpec((1, tm, attn.shape[2]), lambda b, t: (b, t, 0)),
                       pl.BlockSpec(w_o.shape, const2, pipeline_mode=resident)]
        mixer_args = [attn, w_o]
    out, fb = pl.pallas_call(
        kern,
        grid=(bk, r // tm),
        in_specs=[pl.BlockSpec((1, tm, d), lambda b, t: (b, t, 0))] + mixer_specs + [
                  pl.BlockSpec((1, nprev, 2 * f), lambda b, t: (b, 0, 0)),
                  pl.BlockSpec((1, d), const2),
                  pl.BlockSpec((d, 2 * f), const2, pipeline_mode=resident),
                  pl.BlockSpec((cw, 2 * f), const2),
                  pl.BlockSpec((1, 2 * f), const2),
                  pl.BlockSpec((f, d), const2, pipeline_mode=resident),
                  pl.BlockSpec((1, d), const2)],
        out_specs=[pl.BlockSpec((1, tm, d), lambda b, t: (b, t, 0)),
                   pl.BlockSpec((1, 1, nprev, 2 * f), lambda b, t: (b, t, 0, 0))],
        out_shape=[jax.ShapeDtypeStruct((bk, r, d), F32),
                   jax.ShapeDtypeStruct((bk, r // tm, nprev, 2 * f), F32)],
        scratch_shapes=[pltpu.VMEM((tm, d), BF16),
                        pltpu.VMEM((tm, d), F32),
                        pltpu.VMEM((2, 2, hdr + tm, tn), F32),
                        pltpu.VMEM((2, tm, tn), BF16),
                        pltpu.VMEM((2, nc, nprev, tn), F32),
                        pltpu.VMEM((tm, d), F32)],
        name="ffn_layer",
        compiler_params=_cparams(("parallel", "arbitrary")),
    )(x, *mixer_args, fb0, g, w_up, conv_w, conv_b, w_down, g_final)
    return out, fb[:, -1]


def _shifted_bias(rel, rb_ref, h, scale=1.0):
    last = rb_ref[N_BUCKETS - 1, h]
    val = jnp.zeros(rel.shape, F32)
    for b in range(N_BUCKETS - 2, -1, -1):
        val = jnp.where(rel < BUCKET_STARTS[b + 1], (rb_ref[b, h] - last) * scale, val)
    return jnp.where(rel >= 0, val, NEG_INF)


def _near_offsets(tq, tk):
    step = math.gcd(tq, tk)
    lo = -(tq - 1)
    hi = tk - 1 + FAR_DISTANCE - 1
    first = -((-lo) // step) * step
    if first < lo:
        first += step
    count = (hi - first) // step + 1
    return first, step, count


def _prompt_bias_kernel(rb_ref, o_ref, *, tq, tk, first, step):
    h = pl.program_id(0)
    o = pl.program_id(1)
    rel = (lax.broadcasted_iota(jnp.int32, (tk, tq), 1)
           - lax.broadcasted_iota(jnp.int32, (tk, tq), 0) + (first + o * step))
    o_ref[0, 0] = _shifted_bias(rel, rb_ref, h, LOG2E)


def prompt_bias(rel_bias, tq, tk):
    n_heads = rel_bias.shape[1]
    first, step, count = _near_offsets(tq, tk)
    return pl.pallas_call(
        functools.partial(_prompt_bias_kernel, tq=tq, tk=tk, first=first, step=step),
        grid=(n_heads, count),
        in_specs=[pl.BlockSpec(memory_space=pltpu.SMEM)],
        out_specs=pl.BlockSpec((1, 1, tk, tq), lambda h, o: (h, o, 0, 0)),
        out_shape=jax.ShapeDtypeStruct((n_heads, count, tk, tq), F32),
        name="prompt_bias",
        compiler_params=_cparams(("parallel", "parallel")),
    )(rel_bias)


def _sample_bias_kernel(rb_ref, o_ref, *, tq, page, n_heads):
    rows = 2 * tq
    row = lax.broadcasted_iota(jnp.int32, (rows, 2 * page), 0)
    col = lax.broadcasted_iota(jnp.int32, (rows, 2 * page), 1)
    t = row % tq
    rel = jnp.where(col < page, page + t - col, t - (col - page))
    for h in range(n_heads):
        o_ref[h * rows:(h + 1) * rows, :] = _shifted_bias(rel, rb_ref, h)


def sample_bias(rel_bias, tq, page):
    n_heads = rel_bias.shape[1]
    return pl.pallas_call(
        functools.partial(_sample_bias_kernel, tq=tq, page=page, n_heads=n_heads),
        in_specs=[pl.BlockSpec(memory_space=pltpu.SMEM)],
        out_specs=pl.BlockSpec(memory_space=pltpu.VMEM),
        out_shape=jax.ShapeDtypeStruct((n_heads * 2 * tq, 2 * page), F32),
        name="sample_bias",
    )(rel_bias)


def _lambda(lq1_ref, lk1_ref, lq2_ref, lk2_ref, lam_init):
    s1 = jnp.sum(lq1_ref[...] * lk1_ref[...], axis=1, keepdims=True)
    s2 = jnp.sum(lq2_ref[...] * lk2_ref[...], axis=1, keepdims=True)
    return jnp.exp(s1) - jnp.exp(s2) + lam_init


def _block_schedule(tq, tk, nq):
    first, step, count = _near_offsets(tq, tk)
    hi = first + (count - 1) * step
    far, near = [], []
    for qi in range(nq):
        n_far = max(qi * tq - hi + tk - 1, 0) // tk
        k_end = (qi * tq + tq - 1) // tk + 1
        far += [(kj, qi, count) for kj in range(n_far)]
        near += [(kj, qi, (qi * tq - kj * tk - first) // step) for kj in range(n_far, k_end)]
    order = sorted(far) + sorted(near)
    order += [order[-1]] * 2
    return ([o[1] for o in order], [o[0] for o in order], [o[2] for o in order], len(far))


def _attn_kernel(sq_ref, sk_ref, sb_ref, q_ref, k_ref, vt_ref, bias_ref, lq1_ref, lk1_ref, lq2_ref,
                 lk2_ref, subln_ref, o_ref, qs_scr, m_scr, acc_scr, s_scr, p_scr, al_scr,
                 *, tq, tk, n_blocks, n_far, lam_init):
    hw = 2 * HEAD_DIM
    nq = qs_scr.shape[0]

    def prepare(qi, carry):
        q = q_ref[0, pl.ds(pl.multiple_of(qi * tq, tq), tq), :]
        lane = lax.broadcasted_iota(jnp.int32, q.shape, 1)
        qs_scr[qi, :tq, :] = jnp.where(lane < HEAD_DIM, q, jnp.zeros_like(q))
        qs_scr[qi, tq:, :] = jnp.where(lane >= HEAD_DIM, q, jnp.zeros_like(q))
        m_scr[qi] = jnp.full(m_scr.shape[1:], NEG_INF, F32)
        acc_scr[qi] = jnp.zeros(acc_scr.shape[1:], F32)
        return carry

    lax.fori_loop(0, nq, prepare, 0)

    def stage_scores(i, slot):
        start = pl.multiple_of(sk_ref[i] * tk, tk)
        k = k_ref[0, pl.ds(start, tk), :]
        s_scr[slot] = _dot_nt(k, qs_scr[sq_ref[i]])

    kc = 32

    def stage_probs(i, slot, far):
        qi = sq_ref[i]

        def chunk(c):
            s = s_scr[slot, c * kc:(c + 1) * kc, :]
            if not far:
                bias = bias_ref[0, sb_ref[i], c * kc:(c + 1) * kc, :]
                s = s + jnp.concatenate([bias, bias], axis=1)
            return s

        m_prev = m_scr[qi]
        m_rows = None
        for c in range(tk // kc):
            part = jnp.max(chunk(c).reshape(kc // V7X_SUBLANES, V7X_SUBLANES, 2 * tq), axis=0)
            m_rows = part if m_rows is None else jnp.maximum(m_rows, part)
        m_new = jnp.maximum(m_prev, jnp.max(m_rows, axis=0, keepdims=True))
        m_scr[qi] = m_new
        for c in range(tk // kc):
            p_scr[slot, c * kc:(c + 1) * kc, :] = jnp.exp2(chunk(c) - m_new).astype(BF16)
        al_scr[slot] = jnp.exp2(m_prev - m_new)

    def stage_values(i, slot):
        qi = sq_ref[i]
        start = pl.multiple_of(sk_ref[i] * tk, tk)
        vt = vt_ref[0, 0, :, pl.ds(start, tk)]
        acc_scr[qi] = al_scr[slot] * acc_scr[qi] + _dot(vt, p_scr[slot])

    n_slots = s_scr.shape[0]
    assert n_slots == 4 and n_blocks >= n_slots

    def pipe_step(i, r, far, with_scores=True, with_probs=True):
        if with_probs:
            stage_probs(i + 2, (r + 2) % n_slots, far)
        if with_scores:
            stage_scores(i + 4, r)
        stage_values(i, r)

    def run_steps(lo, hi, far):
        while lo < hi and lo % n_slots:
            pipe_step(lo, lo % n_slots, far)
            lo += 1
        n_groups = max(hi - lo, 0) // n_slots

        def group(ii, carry):
            for r in range(n_slots):
                pipe_step(lo + n_slots * ii + r, r, far)
            return carry

        if n_groups > 0:
            lax.fori_loop(0, n_groups, group, 0)
        for i in range(lo + n_slots * n_groups, hi):
            pipe_step(i, i % n_slots, far)

    for i in range(4):
        stage_scores(i, i)
    for i in range(2):
        stage_probs(i, i, i < n_far)
    far_steps = max(n_far - 2, 0)
    run_steps(0, far_steps, True)
    run_steps(far_steps, n_blocks - 4, False)
    for i in range(n_blocks - 4, n_blocks):
        pipe_step(i, i % n_slots, False, with_scores=False, with_probs=i + 2 < n_blocks)

    lam = _lambda(lq1_ref, lk1_ref, lq2_ref, lk2_ref, lam_init)

    def finish(qi, carry):
        acc = acc_scr[qi]
        on = acc[:hw, :] / acc[hw:hw + 1, :]
        ot = on[:, :tq] - lam * on[:, tq:]
        ot = ot * lax.rsqrt(jnp.mean(ot * ot, axis=0, keepdims=True) + EPS)
        ot = ot * subln_ref[...] * (1.0 - lam_init)
        o_ref[0, pl.ds(pl.multiple_of(qi * tq, tq), tq), :] = ot.T.astype(o_ref.dtype)
        return carry

    lax.fori_loop(0, nq, finish, 0)


def prompt_attention(q, k, vt, bias, lq1, lk1, lq2, lk2, subln, *, tq, tk, lam_init):
    b, t, dk = q.shape
    hw = 2 * HEAD_DIM
    n_heads = dk // hw
    vrows = vt.shape[2]
    nq = t // tq
    sched_q, sched_k, sched_b, n_far = _block_schedule(tq, tk, nq)
    n_blocks = len(sched_q) - 2
    vec = lambda shape: pl.BlockSpec(shape, lambda bb, h, *_: (0, 0))
    grid_spec = pltpu.PrefetchScalarGridSpec(
        num_scalar_prefetch=3,
        grid=(b, n_heads),
        in_specs=[pl.BlockSpec((1, t, hw), lambda bb, h, *_: (bb, 0, h), pipeline_mode=pl.Buffered(1)),
                  pl.BlockSpec((1, t, hw), lambda bb, h, *_: (bb, 0, h)),
                  pl.BlockSpec((1, 1, vrows, t), lambda bb, h, *_: (bb, h, 0, 0)),
                  pl.BlockSpec((1,) + bias.shape[1:], lambda bb, h, *_: (h, 0, 0, 0),
                               pipeline_mode=pl.Buffered(1)),
                  vec((1, HEAD_DIM)), vec((1, HEAD_DIM)), vec((1, HEAD_DIM)), vec((1, HEAD_DIM)),
                  vec((hw, 1))],
        out_specs=pl.BlockSpec((1, t, hw), lambda bb, h, *_: (bb, 0, h)),
        scratch_shapes=[pltpu.VMEM((nq, 2 * tq, hw), BF16),
                        pltpu.VMEM((nq, 1, 2 * tq), F32),
                        pltpu.VMEM((nq, vrows, 2 * tq), F32),
                        pltpu.VMEM((4, tk, 2 * tq), F32),
                        pltpu.VMEM((4, tk, 2 * tq), BF16),
                        pltpu.VMEM((4, 1, 2 * tq), F32)])
    return pl.pallas_call(
        functools.partial(_attn_kernel, tq=tq, tk=tk, n_blocks=n_blocks, n_far=n_far, lam_init=lam_init),
        grid_spec=grid_spec,
        out_shape=jax.ShapeDtypeStruct((b, t, dk), BF16),
        name="prompt_attention",
        compiler_params=_cparams(("parallel", "parallel")),
    )(jnp.asarray(sched_q, jnp.int32), jnp.asarray(sched_k, jnp.int32), jnp.asarray(sched_b, jnp.int32),
      q, k, vt, bias, lq1, lk1, lq2, lk2, subln)


def _sample_attn_kernel(pt_ref, *refs, n_pages, page, tq, n_heads, lam_init):
    k_refs = refs[:n_pages]
    v_refs = refs[n_pages:2 * n_pages]
    (q_ref, kn_ref, vn_ref, bias_ref, lq1_ref, lk1_ref, lq2_ref, lk2_ref, subln_ref,
     o_ref, kt_scr, kn_scr, vbf) = refs[2 * n_pages:]
    past = n_pages * page
    hw = 2 * HEAD_DIM
    rows = n_heads * 2 * tq
    dk = n_heads * hw
    nn = kn_ref.shape[1]

    @pl.when(pl.program_id(0) == 0)
    def _():
        kn_scr[...] = jnp.zeros_like(kn_scr)
        for h in range(n_heads):
            vbf[h, past:, :] = jnp.zeros((page, hw), BF16)

    for p in range(n_pages):
        kt_scr[:, p * page:(p + 1) * page] = k_refs[p][0].astype(BF16)
        for h in range(n_heads):
            vbf[h, p * page:(p + 1) * page, :] = (
                v_refs[p][0, pl.ds(h, page, stride=n_heads), :].astype(BF16))
    kn_scr[:nn, :] = kn_ref[0]
    vn = vn_ref[0]
    for h in range(n_heads):
        vbf[h, past:past + nn, :] = vn[:, h * hw:(h + 1) * hw]

    qrep = jnp.concatenate([q_ref[0]] * n_heads, axis=0)
    rr = lax.broadcasted_iota(jnp.int32, (rows, dk), 0)
    cc = lax.broadcasted_iota(jnp.int32, (rows, dk), 1)
    qbd = jnp.where(cc // HEAD_DIM == rr // tq, qrep, 0.0).astype(BF16)

    s_main = _dot(qbd, kt_scr[:, :past - page])
    s_last = _dot(qbd, kt_scr[:, past - page:]) + bias_ref[:, :page]
    s_new = _dot_nt(qbd, kn_scr[...]) + bias_ref[:, page:]
    m = jnp.maximum(jnp.max(s_main, axis=1, keepdims=True),
                    jnp.max(jnp.maximum(s_last, s_new), axis=1, keepdims=True))
    p_all = jnp.concatenate([jnp.exp(s_main - m), jnp.exp(s_last - m), jnp.exp(s_new - m)], axis=1)
    l = jnp.sum(p_all, axis=1, keepdims=True)
    p_bf = p_all.astype(BF16)
    lam = _lambda(lq1_ref, lk1_ref, lq2_ref, lk2_ref, lam_init)
    outs = []
    for h in range(n_heads):
        r0 = h * 2 * tq
        o8 = _dot(p_bf[r0:r0 + 2 * tq], vbf[h]) / l[r0:r0 + 2 * tq]
        o = o8[:tq] - lam * o8[tq:]
        outs.append(_rms(o, subln_ref[...]) * (1.0 - lam_init))
    o_ref[0] = jnp.concatenate(outs, axis=1)


def sample_attention(page_table, cache_kt, cache_v, q8, k_new, v_new, bias, lq1, lk1, lq2, lk2, subln,
                     *, tq, lam_init):
    n_seq, n_pages = page_table.shape
    _, dk, page = cache_kt.shape
    hw = 2 * HEAD_DIM
    n_heads = dk // hw
    nn = k_new.shape[1]
    past = n_pages * page

    def page_spec(p, shape):
        return pl.BlockSpec((1,) + shape, lambda b, pt: (pt[b * n_pages + p], 0, 0))

    vec = lambda shape: pl.BlockSpec(shape, lambda b, pt: (0, 0))
    in_specs = ([page_spec(p, (dk, page)) for p in range(n_pages)]
                + [page_spec(p, (page * n_heads, hw)) for p in range(n_pages)]
                + [pl.BlockSpec((1, 2 * tq, dk), lambda b, pt: (b, 0, 0)),
                   pl.BlockSpec((1, nn, dk), lambda b, pt: (b, 0, 0)),
                   pl.BlockSpec((1, nn, dk), lambda b, pt: (b, 0, 0)),
                   vec(bias.shape),
                   vec((1, HEAD_DIM)), vec((1, HEAD_DIM)), vec((1, HEAD_DIM)), vec((1, HEAD_DIM)),
                   vec((1, hw))])
    grid_spec = pltpu.PrefetchScalarGridSpec(
        num_scalar_prefetch=1,
        grid=(n_seq,),
        in_specs=in_specs,
        out_specs=pl.BlockSpec((1, tq, dk), lambda b, pt: (b, 0, 0)),
        scratch_shapes=[pltpu.VMEM((dk, past), BF16),
                        pltpu.VMEM((page, dk), BF16),
                        pltpu.VMEM((n_heads, past + page, hw), BF16)])
    kern = functools.partial(_sample_attn_kernel, n_pages=n_pages, page=page, tq=tq,
                             n_heads=n_heads, lam_init=lam_init)
    return pl.pallas_call(
        kern,
        grid_spec=grid_spec,
        out_shape=jax.ShapeDtypeStruct((n_seq, tq, dk), F32),
        name="sample_attention",
        compiler_params=_cparams(("arbitrary",)),
    )(page_table.reshape(-1), *([cache_kt] * n_pages), *([cache_v] * n_pages),
      q8, k_new, v_new, bias, lq1, lk1, lq2, lk2, subln)


def _pick_tile(n, target):
    t = min(n, target)
    while n % t:
        t //= 2
    return t


def _run_trunk(x, pos0, ts, rg_h0, rg_cb0, ffn_fb0, w, attn_fn):
    bk, r, d = x.shape
    depth = w['norm_mix'].shape[0]
    n_a = len(w['rg_w_in'])
    tm_rg = _pick_tile(r, 256)
    tm_ffn = _pick_tile(r, 512)
    tm_mm = _pick_tile(bk * r, 512)
    dk = w['attn_w_q'][0].shape[1]
    new_h, new_cb, new_fb = [], [], []
    k_new = v_new = kb = vb = None
    for l in range(depth):
        mixer = None
        if l < n_a:
            x, h_last, cb = rglru_layer(
                x, rg_h0[l], rg_cb0[l], w['norm_mix'][l][None], w['rg_w_in'][l], w['rg_conv_w'][l],
                w['rg_conv_b'][l][None], w['rg_w_ax'][l], w['rg_b_a'][l][None], w['rg_b_x'][l][None],
                w['rg_lambda'][l][None], w['rg_w_out'][l], tm=tm_rg, ts=ts, pos0=pos0)
            new_h.append(h_last)
            new_cb.append(cb)
        else:
            x2 = x.reshape(bk * r, d)
            if l == n_a:
                k_new, v_new, kb, vb = norm_matmul(
                    x2, w['kv_norm'][None], w['w_kv'],
                    [(0, dk, 1.0, True), (dk, 2 * dk, 1.0, False), (0, dk, 1.0, False),
                     (dk, 2 * dk, 1.0, False)],
                    [F32, F32, BF16, BF16], tm_mm, rows_per_seq=r)
            j = l - n_a
            q_dt = attn_fn.q_dtype
            (q,) = norm_matmul(x2, w['norm_mix'][l][None], w['attn_w_q'][j],
                               [(0, dk, attn_fn.q_scale, False)], [q_dt], tm_mm)
            mixer = (attn_fn(j, l, q, kb, vb).reshape(bk, r, dk), w['attn_w_o'][j])
        last = l == depth - 1
        x, fb = ffn_layer(x, mixer, ffn_fb0[l], w['norm_ffn'][l][None], w['ffn_w_up'][l], w['ffn_conv_w'][l],
                          w['ffn_conv_b'][l][None], w['ffn_w_down'][l], w['final_norm'][None],
                          tm=tm_ffn, tn=512, ts=ts, final_norm=last)
        new_fb.append(fb)
    return x, new_h, new_cb, new_fb, k_new, v_new


def _lam_init(layer_idx):
    return 0.8 - 0.6 * math.exp(-0.3 * layer_idx)


def kernel(x_prompt, x_sample, state_rglru_h, state_rglru_conv, state_ffn_conv, cache_k, cache_v,
           page_table, rel_bias, norm_mix, norm_ffn, final_norm, rg_w_in, rg_conv_w, rg_conv_b,
           rg_w_a, rg_b_a, rg_w_x, rg_b_x, rg_lambda, rg_w_out, kv_norm, w_kv, attn_w_q,
           lam_q1, lam_k1, lam_q2, lam_k2, attn_subln, attn_w_o, ffn_w_up, ffn_conv_w,
           ffn_conv_b, ffn_w_down):
    bp, t_p, d = x_prompt.shape
    n_seq, tq, _ = x_sample.shape
    depth = norm_mix.shape[0]
    n_a = rg_w_in.shape[0]
    n_pool, page, n_heads, _, hd = cache_k.shape
    assert hd == HEAD_DIM
    dk = n_heads * 2 * hd
    f2 = ffn_w_up.shape[2]
    past_len = page_table.shape[1] * page
    assert page >= FAR_DISTANCE and past_len >= 2 * page and 2 * tq == V7X_SUBLANES

    per_layer = lambda a: [a[l].astype(BF16) for l in range(a.shape[0])]
    w = {
        'norm_mix': norm_mix, 'norm_ffn': norm_ffn, 'final_norm': final_norm,
        'rg_w_in': per_layer(rg_w_in), 'rg_conv_w': rg_conv_w, 'rg_conv_b': rg_conv_b,
        'rg_w_ax': [jnp.concatenate([a, b], axis=-1).astype(BF16) for a, b in zip(rg_w_a, rg_w_x)],
        'rg_b_a': rg_b_a, 'rg_b_x': rg_b_x, 'rg_lambda': rg_lambda,
        'rg_w_out': per_layer(rg_w_out), 'kv_norm': kv_norm, 'w_kv': w_kv.astype(BF16),
        'attn_w_q': per_layer(attn_w_q), 'attn_w_o': per_layer(attn_w_o),
        'ffn_w_up': per_layer(ffn_w_up), 'ffn_conv_w': ffn_conv_w, 'ffn_conv_b': ffn_conv_b,
        'ffn_w_down': per_layer(ffn_w_down),
    }
    lam_vecs = lambda j: (lam_q1[j][None], lam_k1[j][None], lam_q2[j][None], lam_k2[j][None])

    tq_p = _pick_tile(t_p, 512)
    tk_p = _pick_tile(t_p, 512)
    bias_p = prompt_bias(rel_bias, tq_p, tk_p)

    vt_cache = {}

    def attn_prompt(j, l, q, kb, vb):
        if 'vt' not in vt_cache:
            vt = vb.reshape(bp, t_p, n_heads, 2 * hd).transpose(0, 2, 3, 1)
            extra = jnp.zeros((bp, n_heads, 16, t_p), BF16).at[:, :, 0, :].set(1.0)
            vt_cache['vt'] = jnp.concatenate([vt, extra], axis=2)
        o = prompt_attention(q.reshape(bp, t_p, dk), kb.reshape(bp, t_p, dk), vt_cache['vt'],
                             bias_p, *lam_vecs(j), attn_subln[j][:, None], tq=tq_p, tk=tk_p,
                             lam_init=_lam_init(l))
        return o.reshape(bp * t_p, dk)
    attn_prompt.q_dtype = BF16
    attn_prompt.q_scale = ATTN_SCALE * LOG2E

    cw_rg = rg_conv_w.shape[1]
    cw_ffn = ffn_conv_w.shape[1]
    yp, hp, cbp, fbp, kp, vp = _run_trunk(
        x_prompt, 0, 1,
        jnp.zeros((n_a, bp, 1, d), F32), jnp.zeros((n_a, bp, cw_rg - 1, d), F32),
        jnp.zeros((depth, bp, cw_ffn - 1, f2), F32), w, attn_prompt)

    bias_s = sample_bias(rel_bias, tq, page)
    ck = cache_k.transpose(0, 2, 3, 4, 1).reshape(n_pool, dk, page)
    cv = cache_v.reshape(n_pool, page * n_heads, 2 * hd)
    n_new = 16

    def to_seq_major(a2):
        return a2.reshape(tq, n_seq, -1).transpose(1, 0, 2)

    def attn_sample(j, l, q, kb, vb):
        q4 = to_seq_major(q)
        q8 = jnp.concatenate([q4, q4], axis=1)
        pad = ((0, 0), (0, n_new - tq), (0, 0))
        o = sample_attention(page_table, ck, cv, q8, jnp.pad(to_seq_major(kb), pad),
                             jnp.pad(to_seq_major(vb), pad), bias_s, *lam_vecs(j), attn_subln[j][None],
                             tq=tq, lam_init=_lam_init(l))
        return o.transpose(1, 0, 2).reshape(tq * n_seq, dk)
    attn_sample.q_dtype = F32
    attn_sample.q_scale = ATTN_SCALE

    def tmajor(a):
        return a.transpose(0, 2, 1, 3).reshape(a.shape[0], 1, a.shape[2] * n_seq, a.shape[3])

    def smajor(a2, steps):
        return a2.reshape(steps, n_seq, -1).transpose(1, 0, 2)

    xs = x_sample.transpose(1, 0, 2).reshape(1, tq * n_seq, d)
    ys, hs, cbs, fbs, ks, vs = _run_trunk(
        xs, past_len, n_seq, state_rglru_h[:, None], tmajor(state_rglru_conv), tmajor(state_ffn_conv),
        w, attn_sample)

    return (yp,
            smajor(ys, tq),
            jnp.stack([h[:, 0] for h in hp]),
            jnp.stack(cbp),
            jnp.stack(fbp),
            kp.reshape(bp, n_heads, 2, hd, t_p).transpose(0, 4, 1, 2, 3),
            vp.reshape(bp, t_p, n_heads, 2 * hd),
            jnp.stack([h[0] for h in hs]),
            jnp.stack([smajor(c, cw_rg - 1) for c in cbs]),
            jnp.stack([smajor(fb, cw_ffn - 1) for fb in fbs]),
            ks.reshape(n_heads, 2, hd, tq, n_seq).transpose(4, 3, 0, 1, 2),
            smajor(vs[None], tq).reshape(n_seq, tq, n_heads, 2 * hd))
```

```python
import functools
import math

import jax
import jax.numpy as jnp
from jax import lax
from jax.experimental import pallas as pl
from jax.experimental.pallas import tpu as pltpu

EPS = 1e-6
RG_C = 8.0
HEAD_DIM = 64
N_BUCKETS = 32
MAX_DISTANCE = 128
NEG_INF = -1e30
ATTN_SCALE = HEAD_DIM ** -0.5
LOG2E = math.log2(math.e)

V7X_LANES = 128
V7X_SUBLANES = 8
V7X_VMEM_BYTES = 64 * 2 ** 20
VMEM_LIMIT = V7X_VMEM_BYTES - 8 * 2 ** 20

BF16 = jnp.bfloat16
F32 = jnp.float32


def _bucket_starts():
    max_exact = N_BUCKETS // 2
    starts = list(range(max_exact + 1))
    for b in range(max_exact + 1, N_BUCKETS):
        n = starts[-1]
        while True:
            n += 1
            large = max_exact + int(math.log(n / max_exact) / math.log(MAX_DISTANCE / max_exact)
                                    * (N_BUCKETS - max_exact))
            if min(large, N_BUCKETS - 1) >= b:
                break
        starts.append(n)
    return tuple(starts)


BUCKET_STARTS = _bucket_starts()
FAR_DISTANCE = BUCKET_STARTS[-1]


def _cparams(sem, vmem=VMEM_LIMIT):
    return pltpu.CompilerParams(dimension_semantics=sem, vmem_limit_bytes=vmem)


def _rms(x, g):
    return x * lax.rsqrt(jnp.mean(x * x, axis=-1, keepdims=True) + EPS) * g


def _dot(a, b):
    return jnp.dot(a, b, preferred_element_type=F32)


def _dot_nt(a, b):
    return lax.dot_general(a, b, (((1,), (1,)), ((), ())), preferred_element_type=F32)


def _shift_rows(u, prev, k):
    n = prev.shape[0]
    rolled = pltpu.roll(u, k, 0)
    head = rolled[:V7X_SUBLANES]
    row = lax.broadcasted_iota(jnp.int32, head.shape, 0)
    for t in range(k):
        head = jnp.where(row == t, prev[n - k + t:n - k + t + 1, :], head)
    return jnp.concatenate([head, rolled[V7X_SUBLANES:]], axis=0)


def _norm_mm_kernel(x_ref, g_ref, w_ref, *out_refs, splits):
    xn = _rms(x_ref[...], g_ref[...]).astype(BF16)
    y = _dot(xn, w_ref[...])
    for o_ref, (lo, hi, scale, transposed) in zip(out_refs, splits):
        part = y[:, lo:hi]
        if scale != 1.0:
            part = part * scale
        if transposed:
            o_ref[0] = part.T.astype(o_ref.dtype)
        else:
            o_ref[...] = part.astype(o_ref.dtype)


def norm_matmul(x, g, w, splits, dtypes, tm, rows_per_seq=None):
    r, d = x.shape
    n = w.shape[1]
    out_shape, out_specs = [], []
    for (lo, hi, _, transposed), dt in zip(splits, dtypes):
        if transposed:
            nt = rows_per_seq // tm
            out_shape.append(jax.ShapeDtypeStruct((r // rows_per_seq, hi - lo, rows_per_seq), dt))
            out_specs.append(pl.BlockSpec((1, hi - lo, tm), lambda i, nt=nt: (i // nt, 0, i % nt)))
        else:
            out_shape.append(jax.ShapeDtypeStruct((r, hi - lo), dt))
            out_specs.append(pl.BlockSpec((tm, hi - lo), lambda i: (i, 0)))
    return pl.pallas_call(
        functools.partial(_norm_mm_kernel, splits=tuple(splits)),
        grid=(r // tm,),
        in_specs=[pl.BlockSpec((tm, d), lambda i: (i, 0)),
                  pl.BlockSpec((1, d), lambda i: (0, 0)),
                  pl.BlockSpec((d, n), lambda i: (0, 0))],
        out_specs=out_specs,
        out_shape=out_shape,
        name="norm_matmul",
        compiler_params=_cparams(("parallel",)),
    )(x, g, w)


def _res_mm_kernel(x_ref, a_ref, w_ref, o_ref):
    o_ref[...] = x_ref[...] + _dot(a_ref[...].astype(BF16), w_ref[...])


def residual_matmul(x, a, w, tm):
    r, d = x.shape
    k = a.shape[1]
    return pl.pallas_call(
        _res_mm_kernel,
        grid=(r // tm,),
        in_specs=[pl.BlockSpec((tm, d), lambda i: (i, 0)),
                  pl.BlockSpec((tm, k), lambda i: (i, 0)),
                  pl.BlockSpec((k, d), lambda i: (0, 0))],
        out_specs=pl.BlockSpec((tm, d), lambda i: (i, 0)),
        out_shape=jax.ShapeDtypeStruct((r, d), F32),
        name="residual_matmul",
        compiler_params=_cparams(("parallel",)),
    )(x, a, w)


def _rg_kernel(x_ref, h0_ref, cb0_ref, g_ref, win_ref, cw_ref, cbias_ref, wax_ref, ba_ref, bx_ref,
               lam_ref, wout_ref, out_ref, hlast_ref, cbout_ref,
               rbuf, a_scr, b_scr, h_scr, *, tm, ts, pos0, conv_w, n_blocks):
    ti = pl.program_id(1)
    d = a_scr.shape[1]
    hdr = rbuf.shape[0] - tm
    nprev = (conv_w - 1) * ts

    @pl.when(ti == 0)
    def _():
        rbuf[hdr - nprev:hdr, :] = cb0_ref[0]
        h_scr[...] = h0_ref[0]

    x = x_ref[0]
    xn = _rms(x, g_ref[...]).astype(BF16)
    u = _dot(xn, win_ref[...])
    gate = u[:, :d]
    rbuf[hdr:hdr + tm, :] = u[:, d:]
    xc = cbias_ref[...] + rbuf[hdr:hdr + tm, :] * cw_ref[conv_w - 1:conv_w, :]
    for j in range(conv_w - 1):
        off = hdr - (conv_w - 1 - j) * ts
        xc = xc + rbuf[off:off + tm, :] * cw_ref[j:j + 1, :]
    new_prev = rbuf[hdr + tm - nprev:hdr + tm, :]
    cbout_ref[0] = new_prev
    rbuf[hdr - nprev:hdr, :] = new_prev

    xcb = xc.astype(BF16)
    blk = d // n_blocks
    ras, rxs = [], []
    for n in range(n_blocks):
        rr = _dot(xcb[:, n * blk:(n + 1) * blk], wax_ref[n])
        ras.append(rr[:, :blk])
        rxs.append(rr[:, blk:])
    r = jax.nn.sigmoid(jnp.concatenate(ras, axis=1) + ba_ref[...])
    i = jax.nn.sigmoid(jnp.concatenate(rxs, axis=1) + bx_ref[...])
    z = -lam_ref[...]
    softplus = jnp.maximum(z, 0.0) + jnp.log(1.0 + jnp.exp(-jnp.abs(z)))
    log_a = (-RG_C) * r * softplus
    a = jnp.exp(log_a)
    om = 1.0 - a * a
    mult = jnp.where(om > 0.0, om * lax.rsqrt(om), 0.0)
    if pos0 == 0:
        row = lax.broadcasted_iota(jnp.int32, (tm, 1), 0) + ti * tm
        reset = row < ts
        a = jnp.where(reset, 0.0, a)
        mult = jnp.where(reset, 1.0, mult)
    a_scr[...] = a
    b_scr[...] = mult * i * xc

    if ts == 1:
        def group(gi, h):
            base = pl.multiple_of(gi * V7X_SUBLANES, V7X_SUBLANES)
            comp_a = comp_b = None
            outs = []
            for k in range(V7X_SUBLANES):
                a_r = a_scr[pl.ds(base + k, 1), :]
                b_r = b_scr[pl.ds(base + k, 1), :]
                comp_a = a_r if comp_a is None else a_r * comp_a
                comp_b = b_r if comp_b is None else a_r * comp_b + b_r
                outs.append(comp_a * h + comp_b)
            for k in range(V7X_SUBLANES):
                b_scr[pl.ds(base + k, 1), :] = outs[k]
            return outs[-1]
        h = lax.fori_loop(0, tm // V7X_SUBLANES, group, h_scr[...])
    else:
        h = h_scr[...]
        for t in range(tm // ts):
            h = a_scr[t * ts:(t + 1) * ts, :] * h + b_scr[t * ts:(t + 1) * ts, :]
            b_scr[t * ts:(t + 1) * ts, :] = h
    h_scr[...] = h
    hlast_ref[0] = h

    hg = (b_scr[...] * jax.nn.gelu(gate)).astype(BF16)
    out_ref[0] = x + _dot(hg, wout_ref[...])


def rglru_layer(x, h0, cb0, g, w_in, conv_w, conv_b, w_ax, b_a, b_x, lam, w_out, *, tm, ts, pos0):
    bk, r, d = x.shape
    cw = conv_w.shape[0]
    nprev = (cw - 1) * ts
    hdr = max(V7X_SUBLANES, nprev)
    n_blocks = w_ax.shape[0]
    const2 = lambda b, t: (0, 0)
    kern = functools.partial(_rg_kernel, tm=tm, ts=ts, pos0=pos0, conv_w=cw, n_blocks=n_blocks)
    return pl.pallas_call(
        kern,
        grid=(bk, r // tm),
        in_specs=[pl.BlockSpec((1, tm, d), lambda b, t: (b, t, 0)),
                  pl.BlockSpec((1, ts, d), lambda b, t: (b, 0, 0)),
                  pl.BlockSpec((1, nprev, d), lambda b, t: (b, 0, 0)),
                  pl.BlockSpec((1, d), const2),
                  pl.BlockSpec((d, 2 * d), const2),
                  pl.BlockSpec((cw, d), const2),
                  pl.BlockSpec((1, d), const2),
                  pl.BlockSpec(w_ax.shape, lambda b, t: (0, 0, 0)),
                  pl.BlockSpec((1, d), const2),
                  pl.BlockSpec((1, d), const2),
                  pl.BlockSpec((1, d), const2),
                  pl.BlockSpec((d, d), const2)],
        out_specs=[pl.BlockSpec((1, tm, d), lambda b, t: (b, t, 0)),
                   pl.BlockSpec((1, ts, d), lambda b, t: (b, 0, 0)),
                   pl.BlockSpec((1, nprev, d), lambda b, t: (b, 0, 0))],
        out_shape=[jax.ShapeDtypeStruct((bk, r, d), F32),
                   jax.ShapeDtypeStruct((bk, ts, d), F32),
                   jax.ShapeDtypeStruct((bk, nprev, d), F32)],
        scratch_shapes=[pltpu.VMEM((hdr + tm, d), F32),
                        pltpu.VMEM((tm, d), F32),
                        pltpu.VMEM((tm, d), F32),
                        pltpu.VMEM((ts, d), F32)],
        name="rglru_layer",
        compiler_params=_cparams(("parallel", "arbitrary")),
    )(x, h0, cb0, g, w_in, conv_w, conv_b, w_ax, b_a, b_x, lam, w_out)


def _ffn_kernel(*refs, tm, tn, ts, conv_w, final_norm, has_mixer):
    if has_mixer:
        x_ref, attn_ref, wo_ref = refs[:3]
        refs = refs[3:]
    else:
        x_ref = refs[0]
        refs = refs[1:]
    (fb0_ref, g_ref, wup_ref, cw_ref, cb_ref, wd_ref, gfin_ref, out_ref, fb_ref,
     xn_scr, acc_scr, u_scr, h_scr, carry, xin_scr) = refs
    ti = pl.program_id(1)
    f = wd_ref.shape[0]
    nc = f // tn
    hdr = u_scr.shape[2] - tm
    nprev = (conv_w - 1) * ts

    xin = x_ref[0]
    if has_mixer:
        xin = xin + _dot(attn_ref[0].astype(BF16), wo_ref[...])
    xin_scr[...] = xin
    xn_scr[...] = _rms(xin, g_ref[...]).astype(BF16)

    def stage_up(c, slot):
        xn = xn_scr[...]
        for half in range(2):
            col = half * f + c * tn
            u_scr[slot, half, hdr:hdr + tm, :] = _dot(xn, wup_ref[:, col:col + tn])

    def conv(c, slot, half):
        col = half * f + c * tn
        prev = jnp.where(ti == 0, fb0_ref[0, :, col:col + tn], carry[half, c])
        cw = cw_ref[:, col:col + tn]
        if ts == 1:
            u = u_scr[slot, half]
            uc = cb_ref[:, col:col + tn] + u * cw[conv_w - 1:conv_w, :]
            for jj in range(conv_w - 1):
                uc = uc + _shift_rows(u, prev, conv_w - 1 - jj) * cw[jj:jj + 1, :]
            new_prev = u[tm - nprev:, :]
        else:
            u_scr[slot, half, :hdr, :] = prev
            uc = cb_ref[:, col:col + tn] + u_scr[slot, half, hdr:hdr + tm, :] * cw[conv_w - 1:conv_w, :]
            for jj in range(conv_w - 1):
                off = hdr - (conv_w - 1 - jj) * ts
                uc = uc + u_scr[slot, half, off:off + tm, :] * cw[jj:jj + 1, :]
            new_prev = u_scr[slot, half, tm:hdr + tm, :]
        carry[half, c] = new_prev
        fb_ref[0, 0, :, col:col + tn] = new_prev
        return uc

    def stage_gate(c, slot):
        val = conv(c, slot, 0)
        gte = conv(c, slot, 1)
        h_scr[slot] = (jax.nn.gelu(gte) * val).astype(BF16)

    def stage_down(c, slot):
        y = _dot(h_scr[slot], wd_ref[c * tn:(c + 1) * tn, :])
        if c == 0:
            acc_scr[...] = y
        else:
            acc_scr[...] += y

    stage_up(0, 0)
    stage_gate(0, 0)
    if nc > 1:
        stage_up(1, 1)
    for c in range(nc - 1):
        if c + 2 < nc:
            stage_up(c + 2, c % 2)
        stage_gate(c + 1, (c + 1) % 2)
        stage_down(c, c % 2)
    stage_down(nc - 1, (nc - 1) % 2)

    y = xin_scr[...] + acc_scr[...]
    if final_norm:
        y = _rms(y, gfin_ref[...])
    out_ref[0] = y


def ffn_layer(x, mixer, fb0, g, w_up, conv_w, conv_b, w_down, g_final, *, tm, tn, ts, final_norm):
    bk, r, d = x.shape
    f = w_down.shape[0]
    cw = conv_w.shape[0]
    nprev = (cw - 1) * ts
    hdr = 0 if ts == 1 else nprev
    nc = f // tn
    const2 = lambda b, t: (0, 0)
    resident = pl.Buffered(1)
    kern = functools.partial(_ffn_kernel, tm=tm, tn=tn, ts=ts, conv_w=cw, final_norm=final_norm,
                             has_mixer=mixer is not None)
    mixer_specs, mixer_args = [], []
    if mixer is not None:
        attn, w_o = mixer
        mixer_specs = [pl.BlockSpec((1, tm, attn.shape[2]), lambda b, t: (b, t, 0)),
                       pl.BlockSpec(w_o.shape, const2, pipeline_mode=resident)]
        mixer_args = [attn, w_o]
    out, fb = pl.pallas_call(
        kern,
        grid=(bk, r // tm),
        in_specs=[pl.BlockSpec((1, tm, d), lambda b, t: (b, t, 0))] + mixer_specs + [
                  pl.BlockSpec((1, nprev, 2 * f), lambda b, t: (b, 0, 0)),
                  pl.BlockSpec((1, d), const2),
                  pl.BlockSpec((d, 2 * f), const2, pipeline_mode=resident),
                  pl.BlockSpec((cw, 2 * f), const2),
                  pl.BlockSpec((1, 2 * f), const2),
                  pl.BlockSpec((f, d), const2, pipeline_mode=resident),
                  pl.BlockSpec((1, d), const2)],
        out_specs=[pl.BlockSpec((1, tm, d), lambda b, t: (b, t, 0)),
                   pl.BlockSpec((1, 1, nprev, 2 * f), lambda b, t: (b, t, 0, 0))],
        out_shape=[jax.ShapeDtypeStruct((bk, r, d), F32),
                   jax.ShapeDtypeStruct((bk, r // tm, nprev, 2 * f), F32)],
        scratch_shapes=[pltpu.VMEM((tm, d), BF16),
                        pltpu.VMEM((tm, d), F32),
                        pltpu.VMEM((2, 2, hdr + tm, tn), F32),
                        pltpu.VMEM((2, tm, tn), BF16),
                        pltpu.VMEM((2, nc, nprev, tn), F32),
                        pltpu.VMEM((tm, d), F32)],
        name="ffn_layer",
        compiler_params=_cparams(("parallel", "arbitrary")),
    )(x, *mixer_args, fb0, g, w_up, conv_w, conv_b, w_down, g_final)
    return out, fb[:, -1]


def _shifted_bias(rel, rb_ref, h, scale=1.0):
    last = rb_ref[N_BUCKETS - 1, h]
    val = jnp.zeros(rel.shape, F32)
    for b in range(N_BUCKETS - 2, -1, -1):
        val = jnp.where(rel < BUCKET_STARTS[b + 1], (rb_ref[b, h] - last) * scale, val)
    return jnp.where(rel >= 0, val, NEG_INF)


def _near_offsets(tq, tk):
    step = math.gcd(tq, tk)
    lo = -(tq - 1)
    hi = tk - 1 + FAR_DISTANCE - 1
    first = -((-lo) // step) * step
    if first < lo:
        first += step
    count = (hi - first) // step + 1
    return first, step, count


def _prompt_bias_kernel(rb_ref, o_ref, *, tq, tk, first, step):
    h = pl.program_id(0)
    o = pl.program_id(1)
    rel = (lax.broadcasted_iota(jnp.int32, (tk, tq), 1)
           - lax.broadcasted_iota(jnp.int32, (tk, tq), 0) + (first + o * step))
    o_ref[0, 0] = _shifted_bias(rel, rb_ref, h, LOG2E)


def prompt_bias(rel_bias, tq, tk):
    n_heads = rel_bias.shape[1]
    first, step, count = _near_offsets(tq, tk)
    return pl.pallas_call(
        functools.partial(_prompt_bias_kernel, tq=tq, tk=tk, first=first, step=step),
        grid=(n_heads, count),
        in_specs=[pl.BlockSpec(memory_space=pltpu.SMEM)],
        out_specs=pl.BlockSpec((1, 1, tk, tq), lambda h, o: (h, o, 0, 0)),
        out_shape=jax.ShapeDtypeStruct((n_heads, count, tk, tq), F32),
        name="prompt_bias",
        compiler_params=_cparams(("parallel", "parallel")),
    )(rel_bias)


def _sample_bias_kernel(rb_ref, o_ref, *, tq, page, n_heads):
    rows = 2 * tq
    row = lax.broadcasted_iota(jnp.int32, (rows, 2 * page), 0)
    col = lax.broadcasted_iota(jnp.int32, (rows, 2 * page), 1)
    t = row % tq
    rel = jnp.where(col < page, page + t - col, t - (col - page))
    for h in range(n_heads):
        o_ref[h * rows:(h + 1) * rows, :] = _shifted_bias(rel, rb_ref, h)


def sample_bias(rel_bias, tq, page):
    n_heads = rel_bias.shape[1]
    return pl.pallas_call(
        functools.partial(_sample_bias_kernel, tq=tq, page=page, n_heads=n_heads),
        in_specs=[pl.BlockSpec(memory_space=pltpu.SMEM)],
        out_specs=pl.BlockSpec(memory_space=pltpu.VMEM),
        out_shape=jax.ShapeDtypeStruct((n_heads * 2 * tq, 2 * page), F32),
        name="sample_bias",
    )(rel_bias)


def _lambda(lq1_ref, lk1_ref, lq2_ref, lk2_ref, lam_init):
    s1 = jnp.sum(lq1_ref[...] * lk1_ref[...], axis=1, keepdims=True)
    s2 = jnp.sum(lq2_ref[...] * lk2_ref[...], axis=1, keepdims=True)
    return jnp.exp(s1) - jnp.exp(s2) + lam_init


def _block_schedule(tq, tk, nq):
    first, step, count = _near_offsets(tq, tk)
    hi = first + (count - 1) * step
    far, near = [], []
    for qi in range(nq):
        n_far = max(qi * tq - hi + tk - 1, 0) // tk
        k_end = (qi * tq + tq - 1) // tk + 1
        far += [(kj, qi, count) for kj in range(n_far)]
        near += [(kj, qi, (qi * tq - kj * tk - first) // step) for kj in range(n_far, k_end)]
    order = sorted(far) + sorted(near)
    order += [order[-1]] * 2
    return ([o[1] for o in order], [o[0] for o in order], [o[2] for o in order], len(far))


def _attn_kernel(sq_ref, sk_ref, sb_ref, q_ref, k_ref, vt_ref, bias_ref, lq1_ref, lk1_ref, lq2_ref,
                 lk2_ref, subln_ref, o_ref, qs_scr, m_scr, acc_scr, s_scr, p_scr, al_scr,
                 *, tq, tk, n_blocks, n_far, lam_init):
    hw = 2 * HEAD_DIM
    nq = qs_scr.shape[0]

    def prepare(qi, carry):
        q = q_ref[0, pl.ds(pl.multiple_of(qi * tq, tq), tq), :]
        lane = lax.broadcasted_iota(jnp.int32, q.shape, 1)
        qs_scr[qi, :tq, :] = jnp.where(lane < HEAD_DIM, q, jnp.zeros_like(q))
        qs_scr[qi, tq:, :] = jnp.where(lane >= HEAD_DIM, q, jnp.zeros_like(q))
        m_scr[qi] = jnp.full(m_scr.shape[1:], NEG_INF, F32)
        acc_scr[qi] = jnp.zeros(acc_scr.shape[1:], F32)
        return carry

    lax.fori_loop(0, nq, prepare, 0)

    def stage_scores(i, slot):
        start = pl.multiple_of(sk_ref[i] * tk, tk)
        k = k_ref[0, pl.ds(start, tk), :]
        s_scr[slot] = _dot_nt(k, qs_scr[sq_ref[i]])

    kc = 32

    def stage_probs(i, slot, far):
        qi = sq_ref[i]

        def chunk(c):
            s = s_scr[slot, c * kc:(c + 1) * kc, :]
            if not far:
                bias = bias_ref[0, sb_ref[i], c * kc:(c + 1) * kc, :]
                s = s + jnp.concatenate([bias, bias], axis=1)
            return s

        m_prev = m_scr[qi]
        m_rows = None
        for c in range(tk // kc):
            part = jnp.max(chunk(c).reshape(kc // V7X_SUBLANES, V7X_SUBLANES, 2 * tq), axis=0)
            m_rows = part if m_rows is None else jnp.maximum(m_rows, part)
        m_new = jnp.maximum(m_prev, jnp.max(m_rows, axis=0, keepdims=True))
        m_scr[qi] = m_new
        for c in range(tk // kc):
            p_scr[slot, c * kc:(c + 1) * kc, :] = jnp.exp2(chunk(c) - m_new).astype(BF16)
        al_scr[slot] = jnp.exp2(m_prev - m_new)

    def stage_values(i, slot):
        qi = sq_ref[i]
        start = pl.multiple_of(sk_ref[i] * tk, tk)
        vt = vt_ref[0, 0, :, pl.ds(start, tk)]
        acc_scr[qi] = al_scr[slot] * acc_scr[qi] + _dot(vt, p_scr[slot])

    n_slots = s_scr.shape[0]
    assert n_slots == 4 and n_blocks >= n_slots

    def pipe_step(i, r, far, with_scores=True, with_probs=True):
        if with_probs:
            stage_probs(i + 2, (r + 2) % n_slots, far)
        if with_scores:
            stage_scores(i + 4, r)
        stage_values(i, r)

    def run_steps(lo, hi, far):
        while lo < hi and lo % n_slots:
            pipe_step(lo, lo % n_slots, far)
            lo += 1
        n_groups = max(hi - lo, 0) // n_slots

        def group(ii, carry):
            for r in range(n_slots):
                pipe_step(lo + n_slots * ii + r, r, far)
            return carry

        if n_groups > 0:
            lax.fori_loop(0, n_groups, group, 0)
        for i in range(lo + n_slots * n_groups, hi):
            pipe_step(i, i % n_slots, far)

    for i in range(4):
        stage_scores(i, i)
    for i in range(2):
        stage_probs(i, i, i < n_far)
    far_steps = max(n_far - 2, 0)
    run_steps(0, far_steps, True)
    run_steps(far_steps, n_blocks - 4, False)
    for i in range(n_blocks - 4, n_blocks):
        pipe_step(i, i % n_slots, False, with_scores=False, with_probs=i + 2 < n_blocks)

    lam = _lambda(lq1_ref, lk1_ref, lq2_ref, lk2_ref, lam_init)

    def finish(qi, carry):
        acc = acc_scr[qi]
        on = acc[:hw, :] / acc[hw:hw + 1, :]
        ot = on[:, :tq] - lam * on[:, tq:]
        ot = ot * lax.rsqrt(jnp.mean(ot * ot, axis=0, keepdims=True) + EPS)
        ot = ot * subln_ref[...] * (1.0 - lam_init)
        o_ref[0, pl.ds(pl.multiple_of(qi * tq, tq), tq), :] = ot.T.astype(o_ref.dtype)
        return carry

    lax.fori_loop(0, nq, finish, 0)


def prompt_attention(q, k, vt, bias, lq1, lk1, lq2, lk2, subln, *, tq, tk, lam_init):
    b, t, dk = q.shape
    hw = 2 * HEAD_DIM
    n_heads = dk // hw
    vrows = vt.shape[2]
    nq = t // tq
    sched_q, sched_k, sched_b, n_far = _block_schedule(tq, tk, nq)
    n_blocks = len(sched_q) - 2
    vec = lambda shape: pl.BlockSpec(shape, lambda bb, h, *_: (0, 0))
    grid_spec = pltpu.PrefetchScalarGridSpec(
        num_scalar_prefetch=3,
        grid=(b, n_heads),
        in_specs=[pl.BlockSpec((1, t, hw), lambda bb, h, *_: (bb, 0, h), pipeline_mode=pl.Buffered(1)),
                  pl.BlockSpec((1, t, hw), lambda bb, h, *_: (bb, 0, h)),
                  pl.BlockSpec((1, 1, vrows, t), lambda bb, h, *_: (bb, h, 0, 0)),
                  pl.BlockSpec((1,) + bias.shape[1:], lambda bb, h, *_: (h, 0, 0, 0),
                               pipeline_mode=pl.Buffered(1)),
                  vec((1, HEAD_DIM)), vec((1, HEAD_DIM)), vec((1, HEAD_DIM)), vec((1, HEAD_DIM)),
                  vec((hw, 1))],
        out_specs=pl.BlockSpec((1, t, hw), lambda bb, h, *_: (bb, 0, h)),
        scratch_shapes=[pltpu.VMEM((nq, 2 * tq, hw), BF16),
                        pltpu.VMEM((nq, 1, 2 * tq), F32),
                        pltpu.VMEM((nq, vrows, 2 * tq), F32),
                        pltpu.VMEM((4, tk, 2 * tq), F32),
                        pltpu.VMEM((4, tk, 2 * tq), BF16),
                        pltpu.VMEM((4, 1, 2 * tq), F32)])
    return pl.pallas_call(
        functools.partial(_attn_kernel, tq=tq, tk=tk, n_blocks=n_blocks, n_far=n_far, lam_init=lam_init),
        grid_spec=grid_spec,
        out_shape=jax.ShapeDtypeStruct((b, t, dk), BF16),
        name="prompt_attention",
        compiler_params=_cparams(("parallel", "parallel")),
    )(jnp.asarray(sched_q, jnp.int32), jnp.asarray(sched_k, jnp.int32), jnp.asarray(sched_b, jnp.int32),
      q, k, vt, bias, lq1, lk1, lq2, lk2, subln)


def _sample_attn_kernel(pt_ref, *refs, n_pages, page, tq, n_heads, lam_init):
    k_refs = refs[:n_pages]
    v_refs = refs[n_pages:2 * n_pages]
    (q_ref, kn_ref, vn_ref, bias_ref, lq1_ref, lk1_ref, lq2_ref, lk2_ref, subln_ref,
     o_ref, kt_scr, kn_scr, vbf) = refs[2 * n_pages:]
    past = n_pages * page
    hw = 2 * HEAD_DIM
    rows = n_heads * 2 * tq
    dk = n_heads * hw
    nn = kn_ref.shape[1]

    @pl.when(pl.program_id(0) == 0)
    def _():
        kn_scr[...] = jnp.zeros_like(kn_scr)
        for h in range(n_heads):
            vbf[h, past:, :] = jnp.zeros((page, hw), BF16)

    for p in range(n_pages):
        kt_scr[:, p * page:(p + 1) * page] = k_refs[p][0].astype(BF16)
        for h in range(n_heads):
            vbf[h, p * page:(p + 1) * page, :] = (
                v_refs[p][0, pl.ds(h, page, stride=n_heads), :].astype(BF16))
    kn_scr[:nn, :] = kn_ref[0]
    vn = vn_ref[0]
    for h in range(n_heads):
        vbf[h, past:past + nn, :] = vn[:, h * hw:(h + 1) * hw]

    qrep = jnp.concatenate([q_ref[0]] * n_heads, axis=0)
    rr = lax.broadcasted_iota(jnp.int32, (rows, dk), 0)
    cc = lax.broadcasted_iota(jnp.int32, (rows, dk), 1)
    qbd = jnp.where(cc // HEAD_DIM == rr // tq, qrep, 0.0).astype(BF16)

    s_main = _dot(qbd, kt_scr[:, :past - page])
    s_last = _dot(qbd, kt_scr[:, past - page:]) + bias_ref[:, :page]
    s_new = _dot_nt(qbd, kn_scr[...]) + bias_ref[:, page:]
    m = jnp.maximum(jnp.max(s_main, axis=1, keepdims=True),
                    jnp.max(jnp.maximum(s_last, s_new), axis=1, keepdims=True))
    p_all = jnp.concatenate([jnp.exp(s_main - m), jnp.exp(s_last - m), jnp.exp(s_new - m)], axis=1)
    l = jnp.sum(p_all, axis=1, keepdims=True)
    p_bf = p_all.astype(BF16)
    lam = _lambda(lq1_ref, lk1_ref, lq2_ref, lk2_ref, lam_init)
    outs = []
    for h in range(n_heads):
        r0 = h * 2 * tq
        o8 = _dot(p_bf[r0:r0 + 2 * tq], vbf[h]) / l[r0:r0 + 2 * tq]
        o = o8[:tq] - lam * o8[tq:]
        outs.append(_rms(o, subln_ref[...]) * (1.0 - lam_init))
    o_ref[0] = jnp.concatenate(outs, axis=1)


def sample_attention(page_table, cache_kt, cache_v, q8, k_new, v_new, bias, lq1, lk1, lq2, lk2, subln,
                     *, tq, lam_init):
    n_seq, n_pages = page_table.shape
    _, dk, page = cache_kt.shape
    hw = 2 * HEAD_DIM
    n_heads = dk // hw
    nn = k_new.shape[1]
    past = n_pages * page

    def page_spec(p, shape):
        return pl.BlockSpec((1,) + shape, lambda b, pt: (pt[b * n_pages + p], 0, 0))

    vec = lambda shape: pl.BlockSpec(shape, lambda b, pt: (0, 0))
    in_specs = ([page_spec(p, (dk, page)) for p in range(n_pages)]
                + [page_spec(p, (page * n_heads, hw)) for p in range(n_pages)]
                + [pl.BlockSpec((1, 2 * tq, dk), lambda b, pt: (b, 0, 0)),
                   pl.BlockSpec((1, nn, dk), lambda b, pt: (b, 0, 0)),
                   pl.BlockSpec((1, nn, dk), lambda b, pt: (b, 0, 0)),
                   vec(bias.shape),
                   vec((1, HEAD_DIM)), vec((1, HEAD_DIM)), vec((1, HEAD_DIM)), vec((1, HEAD_DIM)),
                   vec((1, hw))])
    grid_spec = pltpu.PrefetchScalarGridSpec(
        num_scalar_prefetch=1,
        grid=(n_seq,),
        in_specs=in_specs,
        out_specs=pl.BlockSpec((1, tq, dk), lambda b, pt: (b, 0, 0)),
        scratch_shapes=[pltpu.VMEM((dk, past), BF16),
                        pltpu.VMEM((page, dk), BF16),
                        pltpu.VMEM((n_heads, past + page, hw), BF16)])
    kern = functools.partial(_sample_attn_kernel, n_pages=n_pages, page=page, tq=tq,
                             n_heads=n_heads, lam_init=lam_init)
    return pl.pallas_call(
        kern,
        grid_spec=grid_spec,
        out_shape=jax.ShapeDtypeStruct((n_seq, tq, dk), F32),
        name="sample_attention",
        compiler_params=_cparams(("arbitrary",)),
    )(page_table.reshape(-1), *([cache_kt] * n_pages), *([cache_v] * n_pages),
      q8, k_new, v_new, bias, lq1, lk1, lq2, lk2, subln)


def _pick_tile(n, target):
    t = min(n, target)
    while n % t:
        t //= 2
    return t


def _run_trunk(x, pos0, ts, rg_h0, rg_cb0, ffn_fb0, w, attn_fn):
    bk, r, d = x.shape
    depth = w['norm_mix'].shape[0]
    n_a = len(w['rg_w_in'])
    tm_rg = _pick_tile(r, 256)
    tm_ffn = _pick_tile(r, 512)
    tm_mm = _pick_tile(bk * r, 512)
    dk = w['attn_w_q'][0].shape[1]
    new_h, new_cb, new_fb = [], [], []
    k_new = v_new = kb = vb = None
    for l in range(depth):
        mixer = None
        if l < n_a:
            x, h_last, cb = rglru_layer(
                x, rg_h0[l], rg_cb0[l], w['norm_mix'][l][None], w['rg_w_in'][l], w['rg_conv_w'][l],
                w['rg_conv_b'][l][None], w['rg_w_ax'][l], w['rg_b_a'][l][None], w['rg_b_x'][l][None],
                w['rg_lambda'][l][None], w['rg_w_out'][l], tm=tm_rg, ts=ts, pos0=pos0)
            new_h.append(h_last)
            new_cb.append(cb)
        else:
            x2 = x.reshape(bk * r, d)
            if l == n_a:
                k_new, v_new, kb, vb = norm_matmul(
                    x2, w['kv_norm'][None], w['w_kv'],
                    [(0, dk, 1.0, True), (dk, 2 * dk, 1.0, False), (0, dk, 1.0, False),
                     (dk, 2 * dk, 1.0, False)],
                    [F32, F32, BF16, BF16], tm_mm, rows_per_seq=r)
            j = l - n_a
            q_dt = attn_fn.q_dtype
            (q,) = norm_matmul(x2, w['norm_mix'][l][None], w['attn_w_q'][j],
                               [(0, dk, attn_fn.q_scale, False)], [q_dt], tm_mm)
            mixer = (attn_fn(j, l, q, kb, vb).reshape(bk, r, dk), w['attn_w_o'][j])
        last = l == depth - 1
        x, fb = ffn_layer(x, mixer, ffn_fb0[l], w['norm_ffn'][l][None], w['ffn_w_up'][l], w['ffn_conv_w'][l],
                          w['ffn_conv_b'][l][None], w['ffn_w_down'][l], w['final_norm'][None],
                          tm=tm_ffn, tn=512, ts=ts, final_norm=last)
        new_fb.append(fb)
    return x, new_h, new_cb, new_fb, k_new, v_new


def _lam_init(layer_idx):
    return 0.8 - 0.6 * math.exp(-0.3 * layer_idx)


def kernel(x_prompt, x_sample, state_rglru_h, state_rglru_conv, state_ffn_conv, cache_k, cache_v,
           page_table, rel_bias, norm_mix, norm_ffn, final_norm, rg_w_in, rg_conv_w, rg_conv_b,
           rg_w_a, rg_b_a, rg_w_x, rg_b_x, rg_lambda, rg_w_out, kv_norm, w_kv, attn_w_q,
           lam_q1, lam_k1, lam_q2, lam_k2, attn_subln, attn_w_o, ffn_w_up, ffn_conv_w,
           ffn_conv_b, ffn_w_down):
    bp, t_p, d = x_prompt.shape
    n_seq, tq, _ = x_sample.shape
    depth = norm_mix.shape[0]
    n_a = rg_w_in.shape[0]
    n_pool, page, n_heads, _, hd = cache_k.shape
    assert hd == HEAD_DIM
    dk = n_heads * 2 * hd
    f2 = ffn_w_up.shape[2]
    past_len = page_table.shape[1] * page
    assert page >= FAR_DISTANCE and past_len >= 2 * page and 2 * tq == V7X_SUBLANES

    per_layer = lambda a: [a[l].astype(BF16) for l in range(a.shape[0])]
    w = {
        'norm_mix': norm_mix, 'norm_ffn': norm_ffn, 'final_norm': final_norm,
        'rg_w_in': per_layer(rg_w_in), 'rg_conv_w': rg_conv_w, 'rg_conv_b': rg_conv_b,
        'rg_w_ax': [jnp.concatenate([a, b], axis=-1).astype(BF16) for a, b in zip(rg_w_a, rg_w_x)],
        'rg_b_a': rg_b_a, 'rg_b_x': rg_b_x, 'rg_lambda': rg_lambda,
        'rg_w_out': per_layer(rg_w_out), 'kv_norm': kv_norm, 'w_kv': w_kv.astype(BF16),
        'attn_w_q': per_layer(attn_w_q), 'attn_w_o': per_layer(attn_w_o),
        'ffn_w_up': per_layer(ffn_w_up), 'ffn_conv_w': ffn_conv_w, 'ffn_conv_b': ffn_conv_b,
        'ffn_w_down': per_layer(ffn_w_down),
    }
    lam_vecs = lambda j: (lam_q1[j][None], lam_k1[j][None], lam_q2[j][None], lam_k2[j][None])

    tq_p = _pick_tile(t_p, 512)
    tk_p = _pick_tile(t_p, 512)
    bias_p = prompt_bias(rel_bias, tq_p, tk_p)

    vt_cache = {}

    def attn_prompt(j, l, q, kb, vb):
        if 'vt' not in vt_cache:
            vt = vb.reshape(bp, t_p, n_heads, 2 * hd).transpose(0, 2, 3, 1)
            extra = jnp.zeros((bp, n_heads, 16, t_p), BF16).at[:, :, 0, :].set(1.0)
            vt_cache['vt'] = jnp.concatenate([vt, extra], axis=2)
        o = prompt_attention(q.reshape(bp, t_p, dk), kb.reshape(bp, t_p, dk), vt_cache['vt'],
                             bias_p, *lam_vecs(j), attn_subln[j][:, None], tq=tq_p, tk=tk_p,
                             lam_init=_lam_init(l))
        return o.reshape(bp * t_p, dk)
    attn_prompt.q_dtype = BF16
    attn_prompt.q_scale = ATTN_SCALE * LOG2E

    cw_rg = rg_conv_w.shape[1]
    cw_ffn = ffn_conv_w.shape[1]
    yp, hp, cbp, fbp, kp, vp = _run_trunk(
        x_prompt, 0, 1,
        jnp.zeros((n_a, bp, 1, d), F32), jnp.zeros((n_a, bp, cw_rg - 1, d), F32),
        jnp.zeros((depth, bp, cw_ffn - 1, f2), F32), w, attn_prompt)

    bias_s = sample_bias(rel_bias, tq, page)
    ck = cache_k.transpose(0, 2, 3, 4, 1).reshape(n_pool, dk, page)
    cv = cache_v.reshape(n_pool, page * n_heads, 2 * hd)
    n_new = 16

    def to_seq_major(a2):
        return a2.reshape(tq, n_seq, -1).transpose(1, 0, 2)

    def attn_sample(j, l, q, kb, vb):
        q4 = to_seq_major(q)
        q8 = jnp.concatenate([q4, q4], axis=1)
        pad = ((0, 0), (0, n_new - tq), (0, 0))
        o = sample_attention(page_table, ck, cv, q8, jnp.pad(to_seq_major(kb), pad),
                             jnp.pad(to_seq_major(vb), pad), bias_s, *lam_vecs(j), attn_subln[j][None],
                             tq=tq, lam_init=_lam_init(l))
        return o.transpose(1, 0, 2).reshape(tq * n_seq, dk)
    attn_sample.q_dtype = F32
    attn_sample.q_scale = ATTN_SCALE

    def tmajor(a):
        return a.transpose(0, 2, 1, 3).reshape(a.shape[0], 1, a.shape[2] * n_seq, a.shape[3])

    def smajor(a2, steps):
        return a2.reshape(steps, n_seq, -1).transpose(1, 0, 2)

    xs = x_sample.transpose(1, 0, 2).reshape(1, tq * n_seq, d)
    ys, hs, cbs, fbs, ks, vs = _run_trunk(
        xs, past_len, n_seq, state_rglru_h[:, None], tmajor(state_rglru_conv), tmajor(state_ffn_conv),
        w, attn_sample)

    return (yp,
            smajor(ys, tq),
            jnp.stack([h[:, 0] for h in hp]),
            jnp.stack(cbp),
            jnp.stack(fbp),
            kp.reshape(bp, n_heads, 2, hd, t_p).transpose(0, 4, 1, 2, 3),
            vp.reshape(bp, t_p, n_heads, 2 * hd),
            jnp.stack([h[0] for h in hs]),
            jnp.stack([smajor(c, cw_rg - 1) for c in cbs]),
            jnp.stack([smajor(fb, cw_ffn - 1) for fb in fbs]),
            ks.reshape(n_heads, 2, hd, tq, n_seq).transpose(4, 3, 0, 1, 2),
            smajor(vs[None], tq).reshape(n_seq, tq, n_heads, 2 * hd))
```

```python
import functools
import math

import jax
import jax.numpy as jnp
from jax import lax
from jax.experimental import pallas as pl
from jax.experimental.pallas import tpu as pltpu

EPS = 1e-6
RG_C = 8.0
HEAD_DIM = 64
N_BUCKETS = 32
MAX_DISTANCE = 128
NEG_INF = -1e30
ATTN_SCALE = HEAD_DIM ** -0.5
LOG2E = math.log2(math.e)

V7X_LANES = 128
V7X_SUBLANES = 8
V7X_VMEM_BYTES = 64 * 2 ** 20
VMEM_LIMIT = V7X_VMEM_BYTES - 8 * 2 ** 20

BF16 = jnp.bfloat16
F32 = jnp.float32


def _bucket_starts():
    max_exact = N_BUCKETS // 2
    starts = list(range(max_exact + 1))
    for b in range(max_exact + 1, N_BUCKETS):
        n = starts[-1]
        while True:
            n += 1
            large = max_exact + int(math.log(n / max_exact) / math.log(MAX_DISTANCE / max_exact)
                                    * (N_BUCKETS - max_exact))
            if min(large, N_BUCKETS - 1) >= b:
                break
        starts.append(n)
    return tuple(starts)


BUCKET_STARTS = _bucket_starts()
FAR_DISTANCE = BUCKET_STARTS[-1]


def _cparams(sem, vmem=VMEM_LIMIT):
    return pltpu.CompilerParams(dimension_semantics=sem, vmem_limit_bytes=vmem)


def _rms(x, g):
    return x * lax.rsqrt(jnp.mean(x * x, axis=-1, keepdims=True) + EPS) * g


def _dot(a, b):
    return jnp.dot(a, b, preferred_element_type=F32)


def _dot_nt(a, b):
    return lax.dot_general(a, b, (((1,), (1,)), ((), ())), preferred_element_type=F32)


def _shift_rows(u, prev, k):
    n = prev.shape[0]
    rolled = pltpu.roll(u, k, 0)
    head = rolled[:V7X_SUBLANES]
    row = lax.broadcasted_iota(jnp.int32, head.shape, 0)
    for t in range(k):
        head = jnp.where(row == t, prev[n - k + t:n - k + t + 1, :], head)
    return jnp.concatenate([head, rolled[V7X_SUBLANES:]], axis=0)


def _norm_mm_kernel(x_ref, *refs, groups):
    n_in = 2 * len(groups)
    out_refs = iter(refs[n_in:])
    x = x_ref[...]
    xs = x * lax.rsqrt(jnp.mean(x * x, axis=-1, keepdims=True) + EPS)
    for gi, splits in enumerate(groups):
        g_ref, w_ref = refs[2 * gi], refs[2 * gi + 1]
        y = _dot((xs * g_ref[...]).astype(BF16), w_ref[...])
        for lo, hi, scale, transposed in splits:
            o_ref = next(out_refs)
            part = y[:, lo:hi]
            if scale != 1.0:
                part = part * scale
            if transposed:
                o_ref[0] = part.T.astype(o_ref.dtype)
            else:
                o_ref[...] = part.astype(o_ref.dtype)


def norm_matmul(x, groups, tm, rows_per_seq=None):
    r, d = x.shape
    out_shape, out_specs, in_specs, args = [], [], [pl.BlockSpec((tm, d), lambda i: (i, 0))], [x]
    for g, w, splits, dtypes in groups:
        in_specs += [pl.BlockSpec((1, d), lambda i: (0, 0)),
                     pl.BlockSpec(w.shape, lambda i: (0, 0), pipeline_mode=pl.Buffered(1))]
        args += [g, w]
        for (lo, hi, _, transposed), dt in zip(splits, dtypes):
            if transposed:
                nt = rows_per_seq // tm
                out_shape.append(jax.ShapeDtypeStruct((r // rows_per_seq, hi - lo, rows_per_seq), dt))
                out_specs.append(pl.BlockSpec((1, hi - lo, tm), lambda i, nt=nt: (i // nt, 0, i % nt)))
            else:
                out_shape.append(jax.ShapeDtypeStruct((r, hi - lo), dt))
                out_specs.append(pl.BlockSpec((tm, hi - lo), lambda i: (i, 0)))
    return pl.pallas_call(
        functools.partial(_norm_mm_kernel, groups=tuple(tuple(grp[2]) for grp in groups)),
        grid=(r // tm,),
        in_specs=in_specs,
        out_specs=out_specs,
        out_shape=out_shape,
        name="norm_matmul",
        compiler_params=_cparams(("parallel",)),
    )(*args)


def _rg_kernel(x_ref, h0_ref, cb0_ref, g_ref, win_ref, cw_ref, cbias_ref, wax_ref, ba_ref, bx_ref,
               lam_ref, wout_ref, out_ref, hlast_ref, cbout_ref,
               rbuf, a_scr, b_scr, h_scr, *, tm, ts, pos0, conv_w, n_blocks):
    ti = pl.program_id(1)
    d = a_scr.shape[1]
    hdr = rbuf.shape[0] - tm
    nprev = (conv_w - 1) * ts

    @pl.when(ti == 0)
    def _():
        rbuf[hdr - nprev:hdr, :] = cb0_ref[0]
        h_scr[...] = h0_ref[0]

    x = x_ref[0]
    xn = _rms(x, g_ref[...]).astype(BF16)
    u = _dot(xn, win_ref[...])
    gate = u[:, :d]
    rbuf[hdr:hdr + tm, :] = u[:, d:]
    xc = cbias_ref[...] + rbuf[hdr:hdr + tm, :] * cw_ref[conv_w - 1:conv_w, :]
    for j in range(conv_w - 1):
        off = hdr - (conv_w - 1 - j) * ts
        xc = xc + rbuf[off:off + tm, :] * cw_ref[j:j + 1, :]
    new_prev = rbuf[hdr + tm - nprev:hdr + tm, :]
    cbout_ref[0] = new_prev
    rbuf[hdr - nprev:hdr, :] = new_prev

    xcb = xc.astype(BF16)
    blk = d // n_blocks
    ras, rxs = [], []
    for n in range(n_blocks):
        rr = _dot(xcb[:, n * blk:(n + 1) * blk], wax_ref[n])
        ras.append(rr[:, :blk])
        rxs.append(rr[:, blk:])
    r = jax.nn.sigmoid(jnp.concatenate(ras, axis=1) + ba_ref[...])
    i = jax.nn.sigmoid(jnp.concatenate(rxs, axis=1) + bx_ref[...])
    z = -lam_ref[...]
    softplus = jnp.maximum(z, 0.0) + jnp.log(1.0 + jnp.exp(-jnp.abs(z)))
    log_a = (-RG_C) * r * softplus
    a = jnp.exp(log_a)
    om = 1.0 - a * a
    mult = jnp.where(om > 0.0, om * lax.rsqrt(om), 0.0)
    if pos0 == 0:
        row = lax.broadcasted_iota(jnp.int32, (tm, 1), 0) + ti * tm
        reset = row < ts
        a = jnp.where(reset, 0.0, a)
        mult = jnp.where(reset, 1.0, mult)
    a_scr[...] = a
    b_scr[...] = mult * i * xc

    if ts == 1:
        def group(gi, h):
            base = pl.multiple_of(gi * V7X_SUBLANES, V7X_SUBLANES)
            comp_a = comp_b = None
            outs = []
            for k in range(V7X_SUBLANES):
                a_r = a_scr[pl.ds(base + k, 1), :]
                b_r = b_scr[pl.ds(base + k, 1), :]
                comp_a = a_r if comp_a is None else a_r * comp_a
                comp_b = b_r if comp_b is None else a_r * comp_b + b_r
                outs.append(comp_a * h + comp_b)
            for k in range(V7X_SUBLANES):
                b_scr[pl.ds(base + k, 1), :] = outs[k]
            return outs[-1]
        h = lax.fori_loop(0, tm // V7X_SUBLANES, group, h_scr[...])
    else:
        h = h_scr[...]
        for t in range(tm // ts):
            h = a_scr[t * ts:(t + 1) * ts, :] * h + b_scr[t * ts:(t + 1) * ts, :]
            b_scr[t * ts:(t + 1) * ts, :] = h
    h_scr[...] = h
    hlast_ref[0] = h

    hg = (b_scr[...] * jax.nn.gelu(gate)).astype(BF16)
    out_ref[0] = x + _dot(hg, wout_ref[...])


def rglru_layer(x, h0, cb0, g, w_in, conv_w, conv_b, w_ax, b_a, b_x, lam, w_out, *, tm, ts, pos0):
    bk, r, d = x.shape
    cw = conv_w.shape[0]
    nprev = (cw - 1) * ts
    hdr = max(V7X_SUBLANES, nprev)
    n_blocks = w_ax.shape[0]
    const2 = lambda b, t: (0, 0)
    kern = functools.partial(_rg_kernel, tm=tm, ts=ts, pos0=pos0, conv_w=cw, n_blocks=n_blocks)
    return pl.pallas_call(
        kern,
        grid=(bk, r // tm),
        in_specs=[pl.BlockSpec((1, tm, d), lambda b, t: (b, t, 0)),
                  pl.BlockSpec((1, ts, d), lambda b, t: (b, 0, 0)),
                  pl.BlockSpec((1, nprev, d), lambda b, t: (b, 0, 0)),
                  pl.BlockSpec((1, d), const2),
                  pl.BlockSpec((d, 2 * d), const2),
                  pl.BlockSpec((cw, d), const2),
                  pl.BlockSpec((1, d), const2),
                  pl.BlockSpec(w_ax.shape, lambda b, t: (0, 0, 0)),
                  pl.BlockSpec((1, d), const2),
                  pl.BlockSpec((1, d), const2),
                  pl.BlockSpec((1, d), const2),
                  pl.BlockSpec((d, d), const2)],
        out_specs=[pl.BlockSpec((1, tm, d), lambda b, t: (b, t, 0)),
                   pl.BlockSpec((1, ts, d), lambda b, t: (b, 0, 0)),
                   pl.BlockSpec((1, nprev, d), lambda b, t: (b, 0, 0))],
        out_shape=[jax.ShapeDtypeStruct((bk, r, d), F32),
                   jax.ShapeDtypeStruct((bk, ts, d), F32),
                   jax.ShapeDtypeStruct((bk, nprev, d), F32)],
        scratch_shapes=[pltpu.VMEM((hdr + tm, d), F32),
                        pltpu.VMEM((tm, d), F32),
                        pltpu.VMEM((tm, d), F32),
                        pltpu.VMEM((ts, d), F32)],
        name="rglru_layer",
        compiler_params=_cparams(("parallel", "arbitrary")),
    )(x, h0, cb0, g, w_in, conv_w, conv_b, w_ax, b_a, b_x, lam, w_out)


def _ffn_kernel(*refs, tm, tn, ts, conv_w, final_norm, has_mixer):
    if has_mixer:
        x_ref, attn_ref, wo_ref = refs[:3]
        refs = refs[3:]
    else:
        x_ref = refs[0]
        refs = refs[1:]
    (fb0_ref, g_ref, wup_ref, cw_ref, cb_ref, wd_ref, gfin_ref, out_ref, fb_ref,
     xn_scr, acc_scr, u_scr, h_scr, carry, xin_scr) = refs
    ti = pl.program_id(1)
    f = wd_ref.shape[0]
    nc = f // tn
    hdr = u_scr.shape[2] - tm
    nprev = (conv_w - 1) * ts

    xin = x_ref[0]
    if has_mixer:
        xin = xin + _dot(attn_ref[0].astype(BF16), wo_ref[...])
    xin_scr[...] = xin
    xn_scr[...] = _rms(xin, g_ref[...]).astype(BF16)

    def stage_up(c, slot):
        xn = xn_scr[...]
        for half in range(2):
            col = half * f + c * tn
            u_scr[slot, half, hdr:hdr + tm, :] = _dot(xn, wup_ref[:, col:col + tn])

    def conv(c, slot, half):
        col = half * f + c * tn
        prev = jnp.where(ti == 0, fb0_ref[0, :, col:col + tn], carry[half, c])
        cw = cw_ref[:, col:col + tn]
        if ts == 1:
            u = u_scr[slot, half]
            uc = cb_ref[:, col:col + tn] + u * cw[conv_w - 1:conv_w, :]
            for jj in range(conv_w - 1):
                uc = uc + _shift_rows(u, prev, conv_w - 1 - jj) * cw[jj:jj + 1, :]
            new_prev = u[tm - nprev:, :]
        else:
            u_scr[slot, half, :hdr, :] = prev
            uc = cb_ref[:, col:col + tn] + u_scr[slot, half, hdr:hdr + tm, :] * cw[conv_w - 1:conv_w, :]
            for jj in range(conv_w - 1):
                off = hdr - (conv_w - 1 - jj) * ts
                uc = uc + u_scr[slot, half, off:off + tm, :] * cw[jj:jj + 1, :]
            new_prev = u_scr[slot, half, tm:hdr + tm, :]
        carry[half, c] = new_prev
        fb_ref[0, 0, :, col:col + tn] = new_prev
        return uc

    def stage_gate(c, slot):
        val = conv(c, slot, 0)
        gte = conv(c, slot, 1)
        h_scr[slot] = (jax.nn.gelu(gte) * val).astype(BF16)

    def stage_down(c, slot):
        y = _dot(h_scr[slot], wd_ref[c * tn:(c + 1) * tn, :])
        if c == 0:
            acc_scr[...] = y
        else:
            acc_scr[...] += y

    stage_up(0, 0)
    stage_gate(0, 0)
    if nc > 1:
        stage_up(1, 1)
    for c in range(nc - 1):
        if c + 2 < nc:
            stage_up(c + 2, c % 2)
        stage_gate(c + 1, (c + 1) % 2)
        stage_down(c, c % 2)
    stage_down(nc - 1, (nc - 1) % 2)

    y = xin_scr[...] + acc_scr[...]
    if final_norm:
        y = _rms(y, gfin_ref[...])
    out_ref[0] = y


def ffn_layer(x, mixer, fb0, g, w_up, conv_w, conv_b, w_down, g_final, *, layer, tm, tn, ts,
              final_norm):
    bk, r, d = x.shape
    f = w_down.shape[1]
    cw = conv_w.shape[0]
    nprev = (cw - 1) * ts
    hdr = 0 if ts == 1 else nprev
    nc = f // tn
    const2 = lambda b, t: (0, 0)
    resident = pl.Buffered(1)
    kern = functools.partial(_ffn_kernel, tm=tm, tn=tn, ts=ts, conv_w=cw, final_norm=final_norm,
                             has_mixer=mixer is not None)
    mixer_specs, mixer_args = [], []
    if mixer is not None:
        attn, w_o = mixer
        mixer_specs = [pl.BlockSpec((1, tm, attn.shape[2]), lambda b, t: (b, t, 0)),
                       pl.BlockSpec(w_o.shape, const2, pipeline_mode=resident)]
        mixer_args = [attn, w_o]
    out, fb = pl.pallas_call(
        kern,
        grid=(bk, r // tm),
        in_specs=[pl.BlockSpec((1, tm, d), lambda b, t: (b, t, 0))] + mixer_specs + [
                  pl.BlockSpec((1, nprev, 2 * f), lambda b, t: (b, 0, 0)),
                  pl.BlockSpec((1, d), const2),
                  pl.BlockSpec((None, d, 2 * f), lambda b, t: (layer, 0, 0), pipeline_mode=resident),
                  pl.BlockSpec((cw, 2 * f), const2),
                  pl.BlockSpec((1, 2 * f), const2),
                  pl.BlockSpec((None, f, d), lambda b, t: (layer, 0, 0), pipeline_mode=resident),
                  pl.BlockSpec((1, d), const2)],
        out_specs=[pl.BlockSpec((1, tm, d), lambda b, t: (b, t, 0)),
                   pl.BlockSpec((1, 1, nprev, 2 * f), lambda b, t: (b, t, 0, 0))],
        out_shape=[jax.ShapeDtypeStruct((bk, r, d), F32),
                   jax.ShapeDtypeStruct((bk, r // tm, nprev, 2 * f), F32)],
        scratch_shapes=[pltpu.VMEM((tm, d), BF16),
                        pltpu.VMEM((tm, d), F32),
                        pltpu.VMEM((2, 2, hdr + tm, tn), F32),
                        pltpu.VMEM((2, tm, tn), BF16),
                        pltpu.VMEM((2, nc, nprev, tn), F32),
                        pltpu.VMEM((tm, d), F32)],
        name="ffn_layer",
        compiler_params=_cparams(("parallel", "arbitrary")),
    )(x, *mixer_args, fb0, g, w_up, conv_w, conv_b, w_down, g_final)
    return out, fb[:, -1]


def _shifted_bias(rel, rb_ref, h, scale=1.0):
    last = rb_ref[N_BUCKETS - 1, h]
    val = jnp.zeros(rel.shape, F32)
    for b in range(N_BUCKETS - 2, -1, -1):
        val = jnp.where(rel < BUCKET_STARTS[b + 1], (rb_ref[b, h] - last) * scale, val)
    return jnp.where(rel >= 0, val, NEG_INF)


def _near_offsets(tq, tk):
    step = math.gcd(tq, tk)
    lo = -(tq - 1)
    hi = tk - 1 + FAR_DISTANCE - 1
    first = -((-lo) // step) * step
    if first < lo:
        first += step
    count = (hi - first) // step + 1
    return first, step, count


def _prompt_bias_kernel(rb_ref, o_ref, *, tq, tk, first, step):
    h = pl.program_id(0)
    o = pl.program_id(1)
    rel = (lax.broadcasted_iota(jnp.int32, (tk, tq), 1)
           - lax.broadcasted_iota(jnp.int32, (tk, tq), 0) + (first + o * step))
    o_ref[0, 0] = _shifted_bias(rel, rb_ref, h, LOG2E)


def prompt_bias(rel_bias, tq, tk):
    n_heads = rel_bias.shape[1]
    first, step, count = _near_offsets(tq, tk)
    return pl.pallas_call(
        functools.partial(_prompt_bias_kernel, tq=tq, tk=tk, first=first, step=step),
        grid=(n_heads, count),
        in_specs=[pl.BlockSpec(memory_space=pltpu.SMEM)],
        out_specs=pl.BlockSpec((1, 1, tk, tq), lambda h, o: (h, o, 0, 0)),
        out_shape=jax.ShapeDtypeStruct((n_heads, count, tk, tq), F32),
        name="prompt_bias",
        compiler_params=_cparams(("parallel", "parallel")),
    )(rel_bias)


def _sample_bias_kernel(rb_ref, o_ref, *, tq, page, n_heads):
    rows = 2 * tq
    row = lax.broadcasted_iota(jnp.int32, (rows, 2 * page), 0)
    col = lax.broadcasted_iota(jnp.int32, (rows, 2 * page), 1)
    t = row % tq
    rel = jnp.where(col < page, page + t - col, t - (col - page))
    for h in range(n_heads):
        o_ref[h * rows:(h + 1) * rows, :] = _shifted_bias(rel, rb_ref, h)


def sample_bias(rel_bias, tq, page):
    n_heads = rel_bias.shape[1]
    return pl.pallas_call(
        functools.partial(_sample_bias_kernel, tq=tq, page=page, n_heads=n_heads),
        in_specs=[pl.BlockSpec(memory_space=pltpu.SMEM)],
        out_specs=pl.BlockSpec(memory_space=pltpu.VMEM),
        out_shape=jax.ShapeDtypeStruct((n_heads * 2 * tq, 2 * page), F32),
        name="sample_bias",
    )(rel_bias)


def _lambda(lq1_ref, lk1_ref, lq2_ref, lk2_ref, lam_init):
    s1 = jnp.sum(lq1_ref[...] * lk1_ref[...], axis=1, keepdims=True)
    s2 = jnp.sum(lq2_ref[...] * lk2_ref[...], axis=1, keepdims=True)
    return jnp.exp(s1) - jnp.exp(s2) + lam_init


def _block_schedule(tq, tk, nq):
    first, step, count = _near_offsets(tq, tk)
    hi = first + (count - 1) * step
    far, near = [], []
    for qi in range(nq):
        n_far = max(qi * tq - hi + tk - 1, 0) // tk
        k_end = (qi * tq + tq - 1) // tk + 1
        far += [(kj, qi, count) for kj in range(n_far)]
        near += [(kj, qi, (qi * tq - kj * tk - first) // step) for kj in range(n_far, k_end)]
    order = sorted(far) + sorted(near)
    order += [order[-1]] * 2
    return ([o[1] for o in order], [o[0] for o in order], [o[2] for o in order], len(far))


def _attn_kernel(sq_ref, sk_ref, sb_ref, q_ref, k_ref, vt_ref, bias_ref, lq1_ref, lk1_ref, lq2_ref,
                 lk2_ref, subln_ref, o_ref, qs_scr, m_scr, acc_scr, s_scr, p_scr, al_scr,
                 *, tq, tk, n_blocks, n_far, lam_init):
    hw = 2 * HEAD_DIM
    nq = qs_scr.shape[0]

    def prepare(qi, carry):
        q = q_ref[0, pl.ds(pl.multiple_of(qi * tq, tq), tq), :]
        lane = lax.broadcasted_iota(jnp.int32, q.shape, 1)
        qs_scr[qi, :tq, :] = jnp.where(lane < HEAD_DIM, q, jnp.zeros_like(q))
        qs_scr[qi, tq:, :] = jnp.where(lane >= HEAD_DIM, q, jnp.zeros_like(q))
        m_scr[qi] = jnp.full(m_scr.shape[1:], NEG_INF, F32)
        acc_scr[qi] = jnp.zeros(acc_scr.shape[1:], F32)
        return carry

    lax.fori_loop(0, nq, prepare, 0)

    def stage_scores(i, slot):
        start = pl.multiple_of(sk_ref[i] * tk, tk)
        k = k_ref[0, pl.ds(start, tk), :]
        s_scr[slot] = _dot_nt(k, qs_scr[sq_ref[i]])

    kc = 32

    def stage_probs(i, slot, far):
        qi = sq_ref[i]

        def chunk(c):
            s = s_scr[slot, c * kc:(c + 1) * kc, :]
            if not far:
                bias = bias_ref[0, sb_ref[i], c * kc:(c + 1) * kc, :]
                s = s + jnp.concatenate([bias, bias], axis=1)
            return s

        m_prev = m_scr[qi]
        m_rows = None
        for c in range(tk // kc):
            part = jnp.max(chunk(c).reshape(kc // V7X_SUBLANES, V7X_SUBLANES, 2 * tq), axis=0)
            m_rows = part if m_rows is None else jnp.maximum(m_rows, part)
        m_new = jnp.maximum(m_prev, jnp.max(m_rows, axis=0, keepdims=True))
        m_scr[qi] = m_new
        for c in range(tk // kc):
            p_scr[slot, c * kc:(c + 1) * kc, :] = jnp.exp2(chunk(c) - m_new).astype(BF16)
        al_scr[slot] = jnp.exp2(m_prev - m_new)

    def stage_values(i, slot):
        qi = sq_ref[i]
        start = pl.multiple_of(sk_ref[i] * tk, tk)
        vt = vt_ref[0, 0, :, pl.ds(start, tk)]
        acc_scr[qi] = al_scr[slot] * acc_scr[qi] + _dot(vt, p_scr[slot])

    n_slots = s_scr.shape[0]
    assert n_slots == 4 and n_blocks >= n_slots

    def pipe_step(i, r, far, with_scores=True, with_probs=True):
        if with_probs:
            stage_probs(i + 2, (r + 2) % n_slots, far)
        if with_scores:
            stage_scores(i + 4, r)
        stage_values(i, r)

    def run_steps(lo, hi, far):
        while lo < hi and lo % n_slots:
            pipe_step(lo, lo % n_slots, far)
            lo += 1
        n_groups = max(hi - lo, 0) // n_slots

        def group(ii, carry):
            for r in range(n_slots):
                pipe_step(lo + n_slots * ii + r, r, far)
            return carry

        if n_groups > 0:
            lax.fori_loop(0, n_groups, group, 0)
        for i in range(lo + n_slots * n_groups, hi):
            pipe_step(i, i % n_slots, far)

    for i in range(4):
        stage_scores(i, i)
    for i in range(2):
        stage_probs(i, i, i < n_far)
    far_steps = max(n_far - 2, 0)
    run_steps(0, far_steps, True)
    run_steps(far_steps, n_blocks - 4, False)
    for i in range(n_blocks - 4, n_blocks):
        pipe_step(i, i % n_slots, False, with_scores=False, with_probs=i + 2 < n_blocks)

    lam = _lambda(lq1_ref, lk1_ref, lq2_ref, lk2_ref, lam_init)

    def finish(qi, carry):
        acc = acc_scr[qi]
        on = acc[:hw, :] / acc[hw:hw + 1, :]
        ot = on[:, :tq] - lam * on[:, tq:]
        ot = ot * lax.rsqrt(jnp.mean(ot * ot, axis=0, keepdims=True) + EPS)
        ot = ot * subln_ref[...] * (1.0 - lam_init)
        o_ref[0, pl.ds(pl.multiple_of(qi * tq, tq), tq), :] = ot.T.astype(o_ref.dtype)
        return carry

    lax.fori_loop(0, nq, finish, 0)


def prompt_attention(q, k, vt, bias, lq1, lk1, lq2, lk2, subln, *, tq, tk, lam_init):
    b, t, dk = q.shape
    hw = 2 * HEAD_DIM
    n_heads = dk // hw
    vrows = vt.shape[2]
    nq = t // tq
    sched_q, sched_k, sched_b, n_far = _block_schedule(tq, tk, nq)
    n_blocks = len(sched_q) - 2
    vec = lambda shape: pl.BlockSpec(shape, lambda bb, h, *_: (0, 0))
    grid_spec = pltpu.PrefetchScalarGridSpec(
        num_scalar_prefetch=3,
        grid=(b, n_heads),
        in_specs=[pl.BlockSpec((1, t, hw), lambda bb, h, *_: (bb, 0, h), pipeline_mode=pl.Buffered(1)),
                  pl.BlockSpec((1, t, hw), lambda bb, h, *_: (bb, 0, h)),
                  pl.BlockSpec((1, 1, vrows, t), lambda bb, h, *_: (bb, h, 0, 0)),
                  pl.BlockSpec((1,) + bias.shape[1:], lambda bb, h, *_: (h, 0, 0, 0),
                               pipeline_mode=pl.Buffered(1)),
                  vec((1, HEAD_DIM)), vec((1, HEAD_DIM)), vec((1, HEAD_DIM)), vec((1, HEAD_DIM)),
                  vec((hw, 1))],
        out_specs=pl.BlockSpec((1, t, hw), lambda bb, h, *_: (bb, 0, h)),
        scratch_shapes=[pltpu.VMEM((nq, 2 * tq, hw), BF16),
                        pltpu.VMEM((nq, 1, 2 * tq), F32),
                        pltpu.VMEM((nq, vrows, 2 * tq), F32),
                        pltpu.VMEM((4, tk, 2 * tq), F32),
                        pltpu.VMEM((4, tk, 2 * tq), BF16),
                        pltpu.VMEM((4, 1, 2 * tq), F32)])
    return pl.pallas_call(
        functools.partial(_attn_kernel, tq=tq, tk=tk, n_blocks=n_blocks, n_far=n_far, lam_init=lam_init),
        grid_spec=grid_spec,
        out_shape=jax.ShapeDtypeStruct((b, t, dk), BF16),
        name="prompt_attention",
        compiler_params=_cparams(("parallel", "parallel")),
    )(jnp.asarray(sched_q, jnp.int32), jnp.asarray(sched_k, jnp.int32), jnp.asarray(sched_b, jnp.int32),
      q, k, vt, bias, lq1, lk1, lq2, lk2, subln)


def _sample_attn_kernel(pt_ref, *refs, n_pages, page, tq, n_heads, lam_init):
    k_refs = refs[:n_pages]
    v_refs = refs[n_pages:2 * n_pages]
    (q_ref, kn_ref, vn_ref, bias_ref, lq1_ref, lk1_ref, lq2_ref, lk2_ref, subln_ref,
     o_ref, kt_scr, kn_scr, vbf) = refs[2 * n_pages:]
    past = n_pages * page
    hw = 2 * HEAD_DIM
    rows = n_heads * 2 * tq
    dk = n_heads * hw
    nn = kn_ref.shape[1]

    @pl.when(pl.program_id(0) == 0)
    def _():
        kn_scr[...] = jnp.zeros_like(kn_scr)
        for h in range(n_heads):
            vbf[h, past:, :] = jnp.zeros((page, hw), BF16)

    for p in range(n_pages):
        kt_scr[:, p * page:(p + 1) * page] = k_refs[p][0].astype(BF16)
        for h in range(n_heads):
            vbf[h, p * page:(p + 1) * page, :] = (
                v_refs[p][0, pl.ds(h, page, stride=n_heads), :].astype(BF16))
    kn_scr[:nn, :] = kn_ref[0]
    vn = vn_ref[0]
    for h in range(n_heads):
        vbf[h, past:past + nn, :] = vn[:, h * hw:(h + 1) * hw]

    qrep = jnp.concatenate([q_ref[0]] * n_heads, axis=0)
    rr = lax.broadcasted_iota(jnp.int32, (rows, dk), 0)
    cc = lax.broadcasted_iota(jnp.int32, (rows, dk), 1)
    qbd = jnp.where(cc // HEAD_DIM == rr // tq, qrep, 0.0).astype(BF16)

    s_main = _dot(qbd, kt_scr[:, :past - page])
    s_last = _dot(qbd, kt_scr[:, past - page:]) + bias_ref[:, :page]
    s_new = _dot_nt(qbd, kn_scr[...]) + bias_ref[:, page:]
    m = jnp.maximum(jnp.max(s_main, axis=1, keepdims=True),
                    jnp.max(jnp.maximum(s_last, s_new), axis=1, keepdims=True))
    p_all = jnp.concatenate([jnp.exp(s_main - m), jnp.exp(s_last - m), jnp.exp(s_new - m)], axis=1)
    l = jnp.sum(p_all, axis=1, keepdims=True)
    p_bf = p_all.astype(BF16)
    lam = _lambda(lq1_ref, lk1_ref, lq2_ref, lk2_ref, lam_init)
    outs = []
    for h in range(n_heads):
        r0 = h * 2 * tq
        o8 = _dot(p_bf[r0:r0 + 2 * tq], vbf[h]) / l[r0:r0 + 2 * tq]
        o = o8[:tq] - lam * o8[tq:]
        outs.append(_rms(o, subln_ref[...]) * (1.0 - lam_init))
    o_ref[0] = jnp.concatenate(outs, axis=1)


def sample_attention(page_table, cache_kt, cache_v, q8, k_new, v_new, bias, lq1, lk1, lq2, lk2, subln,
                     *, tq, lam_init):
    n_seq, n_pages = page_table.shape
    _, dk, page = cache_kt.shape
    hw = 2 * HEAD_DIM
    n_heads = dk // hw
    nn = k_new.shape[1]
    past = n_pages * page

    def page_spec(p, shape):
        return pl.BlockSpec((1,) + shape, lambda b, pt: (pt[b * n_pages + p], 0, 0))

    vec = lambda shape: pl.BlockSpec(shape, lambda b, pt: (0, 0))
    in_specs = ([page_spec(p, (dk, page)) for p in range(n_pages)]
                + [page_spec(p, (page * n_heads, hw)) for p in range(n_pages)]
                + [pl.BlockSpec((1, 2 * tq, dk), lambda b, pt: (b, 0, 0)),
                   pl.BlockSpec((1, nn, dk), lambda b, pt: (b, 0, 0)),
                   pl.BlockSpec((1, nn, dk), lambda b, pt: (b, 0, 0)),
                   vec(bias.shape),
                   vec((1, HEAD_DIM)), vec((1, HEAD_DIM)), vec((1, HEAD_DIM)), vec((1, HEAD_DIM)),
                   vec((1, hw))])
    grid_spec = pltpu.PrefetchScalarGridSpec(
        num_scalar_prefetch=1,
        grid=(n_seq,),
        in_specs=in_specs,
        out_specs=pl.BlockSpec((1, tq, dk), lambda b, pt: (b, 0, 0)),
        scratch_shapes=[pltpu.VMEM((dk, past), BF16),
                        pltpu.VMEM((page, dk), BF16),
                        pltpu.VMEM((n_heads, past + page, hw), BF16)])
    kern = functools.partial(_sample_attn_kernel, n_pages=n_pages, page=page, tq=tq,
                             n_heads=n_heads, lam_init=lam_init)
    return pl.pallas_call(
        kern,
        grid_spec=grid_spec,
        out_shape=jax.ShapeDtypeStruct((n_seq, tq, dk), F32),
        name="sample_attention",
        compiler_params=_cparams(("arbitrary",)),
    )(page_table.reshape(-1), *([cache_kt] * n_pages), *([cache_v] * n_pages),
      q8, k_new, v_new, bias, lq1, lk1, lq2, lk2, subln)


def _pick_tile(n, target):
    t = min(n, target)
    while n % t:
        t //= 2
    return t


def _run_trunk(x, pos0, ts, rg_h0, rg_cb0, ffn_fb0, w, attn_fn):
    bk, r, d = x.shape
    depth = w['norm_mix'].shape[0]
    n_a = len(w['rg_w_in'])
    tm_rg = _pick_tile(r, 256)
    tm_ffn = _pick_tile(r, 512)
    tm_mm = _pick_tile(bk * r, 512)
    dk = w['attn_w_q'][0].shape[1]
    new_h, new_cb, new_fb = [], [], []
    k_new = v_new = kb = vb = None
    for l in range(depth):
        mixer = None
        if l < n_a:
            x, h_last, cb = rglru_layer(
                x, rg_h0[l], rg_cb0[l], w['norm_mix'][l][None], w['rg_w_in'][l], w['rg_conv_w'][l],
                w['rg_conv_b'][l][None], w['rg_w_ax'][l], w['rg_b_a'][l][None], w['rg_b_x'][l][None],
                w['rg_lambda'][l][None], w['rg_w_out'][l], tm=tm_rg, ts=ts, pos0=pos0)
            new_h.append(h_last)
            new_cb.append(cb)
        else:
            x2 = x.reshape(bk * r, d)
            j = l - n_a
            groups = [(w['norm_mix'][l][None], w['attn_w_q'][j], [(0, dk, attn_fn.q_scale, False)],
                       [attn_fn.q_dtype])]
            if l == n_a:
                groups.append((w['kv_norm'][None], w['w_kv'],
                               [(0, dk, 1.0, True), (dk, 2 * dk, 1.0, False), (0, dk, 1.0, False),
                                (dk, 2 * dk, 1.0, False)], [F32, F32, BF16, BF16]))
            outs = norm_matmul(x2, groups, tm_mm, rows_per_seq=r)
            q = outs[0]
            if l == n_a:
                k_new, v_new, kb, vb = outs[1:]
            mixer = (attn_fn(j, l, q, kb, vb).reshape(bk, r, dk), w['attn_w_o'][j])
        last = l == depth - 1
        x, fb = ffn_layer(x, mixer, ffn_fb0[l], w['norm_ffn'][l][None], w['ffn_w_up'], w['ffn_conv_w'][l],
                          w['ffn_conv_b'][l][None], w['ffn_w_down'], w['final_norm'][None],
                          layer=l, tm=tm_ffn, tn=512, ts=ts, final_norm=last)
        new_fb.append(fb)
    return x, new_h, new_cb, new_fb, k_new, v_new


def _lam_init(layer_idx):
    return 0.8 - 0.6 * math.exp(-0.3 * layer_idx)


def kernel(x_prompt, x_sample, state_rglru_h, state_rglru_conv, state_ffn_conv, cache_k, cache_v,
           page_table, rel_bias, norm_mix, norm_ffn, final_norm, rg_w_in, rg_conv_w, rg_conv_b,
           rg_w_a, rg_b_a, rg_w_x, rg_b_x, rg_lambda, rg_w_out, kv_norm, w_kv, attn_w_q,
           lam_q1, lam_k1, lam_q2, lam_k2, attn_subln, attn_w_o, ffn_w_up, ffn_conv_w,
           ffn_conv_b, ffn_w_down):
    bp, t_p, d = x_prompt.shape
    n_seq, tq, _ = x_sample.shape
    depth = norm_mix.shape[0]
    n_a = rg_w_in.shape[0]
    n_pool, page, n_heads, _, hd = cache_k.shape
    assert hd == HEAD_DIM
    dk = n_heads * 2 * hd
    f2 = ffn_w_up.shape[2]
    past_len = page_table.shape[1] * page
    assert page >= FAR_DISTANCE and past_len >= 2 * page and 2 * tq == V7X_SUBLANES

    per_layer = lambda a: [a[l].astype(BF16) for l in range(a.shape[0])]
    w = {
        'norm_mix': norm_mix, 'norm_ffn': norm_ffn, 'final_norm': final_norm,
        'rg_w_in': per_layer(rg_w_in), 'rg_conv_w': rg_conv_w, 'rg_conv_b': rg_conv_b,
        'rg_w_ax': [jnp.concatenate([a, b], axis=-1).astype(BF16) for a, b in zip(rg_w_a, rg_w_x)],
        'rg_b_a': rg_b_a, 'rg_b_x': rg_b_x, 'rg_lambda': rg_lambda,
        'rg_w_out': per_layer(rg_w_out), 'kv_norm': kv_norm, 'w_kv': w_kv.astype(BF16),
        'attn_w_q': per_layer(attn_w_q), 'attn_w_o': per_layer(attn_w_o),
        'ffn_w_up': ffn_w_up.astype(BF16), 'ffn_conv_w': ffn_conv_w, 'ffn_conv_b': ffn_conv_b,
        'ffn_w_down': ffn_w_down.astype(BF16),
    }
    lam_vecs = lambda j: (lam_q1[j][None], lam_k1[j][None], lam_q2[j][None], lam_k2[j][None])

    tq_p = _pick_tile(t_p, 512)
    tk_p = _pick_tile(t_p, 512)
    bias_p = prompt_bias(rel_bias, tq_p, tk_p)

    vt_cache = {}

    def attn_prompt(j, l, q, kb, vb):
        if 'vt' not in vt_cache:
            vt = vb.reshape(bp, t_p, n_heads, 2 * hd).transpose(0, 2, 3, 1)
            extra = jnp.zeros((bp, n_heads, 16, t_p), BF16).at[:, :, 0, :].set(1.0)
            vt_cache['vt'] = jnp.concatenate([vt, extra], axis=2)
        o = prompt_attention(q.reshape(bp, t_p, dk), kb.reshape(bp, t_p, dk), vt_cache['vt'],
                             bias_p, *lam_vecs(j), attn_subln[j][:, None], tq=tq_p, tk=tk_p,
                             lam_init=_lam_init(l))
        return o.reshape(bp * t_p, dk)
    attn_prompt.q_dtype = BF16
    attn_prompt.q_scale = ATTN_SCALE * LOG2E

    cw_rg = rg_conv_w.shape[1]
    cw_ffn = ffn_conv_w.shape[1]
    yp, hp, cbp, fbp, kp, vp = _run_trunk(
        x_prompt, 0, 1,
        jnp.zeros((n_a, bp, 1, d), F32), jnp.zeros((n_a, bp, cw_rg - 1, d), F32),
        jnp.zeros((depth, bp, cw_ffn - 1, f2), F32), w, attn_prompt)

    bias_s = sample_bias(rel_bias, tq, page)
    ck = cache_k.transpose(0, 2, 3, 4, 1).reshape(n_pool, dk, page)
    cv = cache_v.reshape(n_pool, page * n_heads, 2 * hd)
    n_new = 16

    def to_seq_major(a2):
        return a2.reshape(tq, n_seq, -1).transpose(1, 0, 2)

    def attn_sample(j, l, q, kb, vb):
        q4 = to_seq_major(q)
        q8 = jnp.concatenate([q4, q4], axis=1)
        pad = ((0, 0), (0, n_new - tq), (0, 0))
        o = sample_attention(page_table, ck, cv, q8, jnp.pad(to_seq_major(kb), pad),
                             jnp.pad(to_seq_major(vb), pad), bias_s, *lam_vecs(j), attn_subln[j][None],
                             tq=tq, lam_init=_lam_init(l))
        return o.transpose(1, 0, 2).reshape(tq * n_seq, dk)
    attn_sample.q_dtype = F32
    attn_sample.q_scale = ATTN_SCALE

    def tmajor(a):
        return a.transpose(0, 2, 1, 3).reshape(a.shape[0], 1, a.shape[2] * n_seq, a.shape[3])

    def smajor(a2, steps):
        return a2.reshape(steps, n_seq, -1).transpose(1, 0, 2)

    xs = x_sample.transpose(1, 0, 2).reshape(1, tq * n_seq, d)
    ys, hs, cbs, fbs, ks, vs = _run_trunk(
        xs, past_len, n_seq, state_rglru_h[:, None], tmajor(state_rglru_conv), tmajor(state_ffn_conv),
        w, attn_sample)

    return (yp,
            smajor(ys, tq),
            jnp.stack([h[:, 0] for h in hp]),
            jnp.stack(cbp),
            jnp.stack(fbp),
            kp.reshape(bp, n_heads, 2, hd, t_p).transpose(0, 4, 1, 2, 3),
            vp.reshape(bp, t_p, n_heads, 2 * hd),
            jnp.stack([h[0] for h in hs]),
            jnp.stack([smajor(c, cw_rg - 1) for c in cbs]),
            jnp.stack([smajor(fb, cw_ffn - 1) for fb in fbs]),
            ks.reshape(n_heads, 2, hd, tq, n_seq).transpose(4, 3, 0, 1, 2),
            smajor(vs[None], tq).reshape(n_seq, tq, n_heads, 2 * hd))
```

```python
import functools
import math

import jax
import jax.numpy as jnp
from jax import lax
from jax.experimental import pallas as pl
from jax.experimental.pallas import tpu as pltpu

EPS = 1e-6
RG_C = 8.0
HEAD_DIM = 64
N_BUCKETS = 32
MAX_DISTANCE = 128
NEG_INF = -1e30
ATTN_SCALE = HEAD_DIM ** -0.5
LOG2E = math.log2(math.e)

V7X_LANES = 128
V7X_SUBLANES = 8
V7X_VMEM_BYTES = 64 * 2 ** 20
VMEM_LIMIT = V7X_VMEM_BYTES - 8 * 2 ** 20

BF16 = jnp.bfloat16
F32 = jnp.float32


def _bucket_starts():
    max_exact = N_BUCKETS // 2
    starts = list(range(max_exact + 1))
    for b in range(max_exact + 1, N_BUCKETS):
        n = starts[-1]
        while True:
            n += 1
            large = max_exact + int(math.log(n / max_exact) / math.log(MAX_DISTANCE / max_exact)
                                    * (N_BUCKETS - max_exact))
            if min(large, N_BUCKETS - 1) >= b:
                break
        starts.append(n)
    return tuple(starts)


BUCKET_STARTS = _bucket_starts()
FAR_DISTANCE = BUCKET_STARTS[-1]


def _cparams(sem, vmem=VMEM_LIMIT):
    return pltpu.CompilerParams(dimension_semantics=sem, vmem_limit_bytes=vmem)


def _rms(x, g):
    return x * lax.rsqrt(jnp.mean(x * x, axis=-1, keepdims=True) + EPS) * g


def _dot(a, b):
    return jnp.dot(a, b, preferred_element_type=F32)


def _dot_nt(a, b):
    return lax.dot_general(a, b, (((1,), (1,)), ((), ())), preferred_element_type=F32)


def _shift_rows(u, prev, k):
    n = prev.shape[0]
    rolled = pltpu.roll(u, k, 0)
    head = rolled[:V7X_SUBLANES]
    row = lax.broadcasted_iota(jnp.int32, head.shape, 0)
    for t in range(k):
        head = jnp.where(row == t, prev[n - k + t:n - k + t + 1, :], head)
    return jnp.concatenate([head, rolled[V7X_SUBLANES:]], axis=0)


def _norm_mm_kernel(x_ref, *refs, groups):
    n_in = 2 * len(groups)
    out_refs = iter(refs[n_in:])
    x = x_ref[...]
    xs = x * lax.rsqrt(jnp.mean(x * x, axis=-1, keepdims=True) + EPS)
    for gi, splits in enumerate(groups):
        g_ref, w_ref = refs[2 * gi], refs[2 * gi + 1]
        y = _dot((xs * g_ref[...]).astype(BF16), w_ref[...])
        for lo, hi, scale, transposed in splits:
            o_ref = next(out_refs)
            part = y[:, lo:hi]
            if scale != 1.0:
                part = part * scale
            if transposed:
                o_ref[0] = part.T.astype(o_ref.dtype)
            else:
                o_ref[...] = part.astype(o_ref.dtype)


def norm_matmul(x, groups, tm, rows_per_seq=None):
    r, d = x.shape
    out_shape, out_specs, in_specs, args = [], [], [pl.BlockSpec((tm, d), lambda i: (i, 0))], [x]
    for g, w, splits, dtypes in groups:
        in_specs += [pl.BlockSpec((1, d), lambda i: (0, 0)),
                     pl.BlockSpec(w.shape, lambda i: (0, 0), pipeline_mode=pl.Buffered(1))]
        args += [g, w]
        for (lo, hi, _, transposed), dt in zip(splits, dtypes):
            if transposed:
                nt = rows_per_seq // tm
                out_shape.append(jax.ShapeDtypeStruct((r // rows_per_seq, hi - lo, rows_per_seq), dt))
                out_specs.append(pl.BlockSpec((1, hi - lo, tm), lambda i, nt=nt: (i // nt, 0, i % nt)))
            else:
                out_shape.append(jax.ShapeDtypeStruct((r, hi - lo), dt))
                out_specs.append(pl.BlockSpec((tm, hi - lo), lambda i: (i, 0)))
    return pl.pallas_call(
        functools.partial(_norm_mm_kernel, groups=tuple(tuple(grp[2]) for grp in groups)),
        grid=(r // tm,),
        in_specs=in_specs,
        out_specs=out_specs,
        out_shape=out_shape,
        name="norm_matmul",
        compiler_params=_cparams(("parallel",)),
    )(*args)


def _rg_kernel(x_ref, h0_ref, cb0_ref, g_ref, win_ref, cw_ref, cbias_ref, wax_ref, ba_ref, bx_ref,
               lam_ref, wout_ref, out_ref, hlast_ref, cbout_ref,
               rbuf, a_scr, b_scr, h_scr, *, tm, ts, pos0, conv_w, n_blocks):
    ti = pl.program_id(1)
    d = a_scr.shape[1]
    hdr = rbuf.shape[0] - tm
    nprev = (conv_w - 1) * ts

    @pl.when(ti == 0)
    def _():
        rbuf[hdr - nprev:hdr, :] = cb0_ref[0]
        h_scr[...] = h0_ref[0]

    x = x_ref[0]
    xn = _rms(x, g_ref[...]).astype(BF16)
    u = _dot(xn, win_ref[...])
    gate = u[:, :d]
    rbuf[hdr:hdr + tm, :] = u[:, d:]
    xc = cbias_ref[...] + rbuf[hdr:hdr + tm, :] * cw_ref[conv_w - 1:conv_w, :]
    for j in range(conv_w - 1):
        off = hdr - (conv_w - 1 - j) * ts
        xc = xc + rbuf[off:off + tm, :] * cw_ref[j:j + 1, :]
    new_prev = rbuf[hdr + tm - nprev:hdr + tm, :]
    cbout_ref[0] = new_prev
    rbuf[hdr - nprev:hdr, :] = new_prev

    xcb = xc.astype(BF16)
    blk = d // n_blocks
    ras, rxs = [], []
    for n in range(n_blocks):
        rr = _dot(xcb[:, n * blk:(n + 1) * blk], wax_ref[n])
        ras.append(rr[:, :blk])
        rxs.append(rr[:, blk:])
    r = jax.nn.sigmoid(jnp.concatenate(ras, axis=1) + ba_ref[...])
    i = jax.nn.sigmoid(jnp.concatenate(rxs, axis=1) + bx_ref[...])
    z = -lam_ref[...]
    softplus = jnp.maximum(z, 0.0) + jnp.log(1.0 + jnp.exp(-jnp.abs(z)))
    log_a = (-RG_C) * r * softplus
    a = jnp.exp(log_a)
    om = 1.0 - a * a
    mult = jnp.where(om > 0.0, om * lax.rsqrt(om), 0.0)
    if pos0 == 0:
        row = lax.broadcasted_iota(jnp.int32, (tm, 1), 0) + ti * tm
        reset = row < ts
        a = jnp.where(reset, 0.0, a)
        mult = jnp.where(reset, 1.0, mult)
    a_scr[...] = a
    b_scr[...] = mult * i * xc

    if ts == 1:
        def group(gi, h):
            base = pl.multiple_of(gi * V7X_SUBLANES, V7X_SUBLANES)
            comp_a = comp_b = None
            outs = []
            for k in range(V7X_SUBLANES):
                a_r = a_scr[pl.ds(base + k, 1), :]
                b_r = b_scr[pl.ds(base + k, 1), :]
                comp_a = a_r if comp_a is None else a_r * comp_a
                comp_b = b_r if comp_b is None else a_r * comp_b + b_r
                outs.append(comp_a * h + comp_b)
            for k in range(V7X_SUBLANES):
                b_scr[pl.ds(base + k, 1), :] = outs[k]
            return outs[-1]
        h = lax.fori_loop(0, tm // V7X_SUBLANES, group, h_scr[...])
    else:
        h = h_scr[...]
        for t in range(tm // ts):
            h = a_scr[t * ts:(t + 1) * ts, :] * h + b_scr[t * ts:(t + 1) * ts, :]
            b_scr[t * ts:(t + 1) * ts, :] = h
    h_scr[...] = h
    hlast_ref[0] = h

    hg = (b_scr[...] * jax.nn.gelu(gate)).astype(BF16)
    out_ref[0] = x + _dot(hg, wout_ref[...])


def rglru_layer(x, h0, cb0, g, w_in, conv_w, conv_b, w_ax, b_a, b_x, lam, w_out, *, tm, ts, pos0):
    bk, r, d = x.shape
    cw = conv_w.shape[0]
    nprev = (cw - 1) * ts
    hdr = max(V7X_SUBLANES, nprev)
    n_blocks = w_ax.shape[0]
    const2 = lambda b, t: (0, 0)
    kern = functools.partial(_rg_kernel, tm=tm, ts=ts, pos0=pos0, conv_w=cw, n_blocks=n_blocks)
    return pl.pallas_call(
        kern,
        grid=(bk, r // tm),
        in_specs=[pl.BlockSpec((1, tm, d), lambda b, t: (b, t, 0)),
                  pl.BlockSpec((1, ts, d), lambda b, t: (b, 0, 0)),
                  pl.BlockSpec((1, nprev, d), lambda b, t: (b, 0, 0)),
                  pl.BlockSpec((1, d), const2),
                  pl.BlockSpec((d, 2 * d), const2),
                  pl.BlockSpec((cw, d), const2),
                  pl.BlockSpec((1, d), const2),
                  pl.BlockSpec(w_ax.shape, lambda b, t: (0, 0, 0)),
                  pl.BlockSpec((1, d), const2),
                  pl.BlockSpec((1, d), const2),
                  pl.BlockSpec((1, d), const2),
                  pl.BlockSpec((d, d), const2)],
        out_specs=[pl.BlockSpec((1, tm, d), lambda b, t: (b, t, 0)),
                   pl.BlockSpec((1, ts, d), lambda b, t: (b, 0, 0)),
                   pl.BlockSpec((1, nprev, d), lambda b, t: (b, 0, 0))],
        out_shape=[jax.ShapeDtypeStruct((bk, r, d), F32),
                   jax.ShapeDtypeStruct((bk, ts, d), F32),
                   jax.ShapeDtypeStruct((bk, nprev, d), F32)],
        scratch_shapes=[pltpu.VMEM((hdr + tm, d), F32),
                        pltpu.VMEM((tm, d), F32),
                        pltpu.VMEM((tm, d), F32),
                        pltpu.VMEM((ts, d), F32)],
        name="rglru_layer",
        compiler_params=_cparams(("parallel", "arbitrary")),
    )(x, h0, cb0, g, w_in, conv_w, conv_b, w_ax, b_a, b_x, lam, w_out)


def _ffn_kernel(*refs, tm, tn, ts, conv_w, final_norm, has_mixer):
    if has_mixer:
        x_ref, attn_ref, wo_ref = refs[:3]
        refs = refs[3:]
    else:
        x_ref = refs[0]
        refs = refs[1:]
    (fb0_ref, g_ref, wup_ref, cw_ref, cb_ref, wd_ref, gfin_ref, out_ref, fb_ref,
     xn_scr, acc_scr, u_scr, h_scr, carry, xin_scr) = refs
    ti = pl.program_id(1)
    f = wd_ref.shape[0]
    nc = f // tn
    hdr = u_scr.shape[2] - tm
    nprev = (conv_w - 1) * ts

    xin = x_ref[0]
    if has_mixer:
        xin = xin + _dot(attn_ref[0].astype(BF16), wo_ref[...])
    xin_scr[...] = xin
    xn_scr[...] = _rms(xin, g_ref[...]).astype(BF16)

    def stage_up(c, slot):
        xn = xn_scr[...]
        for half in range(2):
            col = half * f + c * tn
            u_scr[slot, half, hdr:hdr + tm, :] = _dot(xn, wup_ref[:, col:col + tn])

    def conv(c, slot, half):
        col = half * f + c * tn
        prev = jnp.where(ti == 0, fb0_ref[0, :, col:col + tn], carry[half, c])
        cw = cw_ref[:, col:col + tn]
        if ts == 1:
            u = u_scr[slot, half]
            uc = cb_ref[:, col:col + tn] + u * cw[conv_w - 1:conv_w, :]
            for jj in range(conv_w - 1):
                uc = uc + _shift_rows(u, prev, conv_w - 1 - jj) * cw[jj:jj + 1, :]
            new_prev = u[tm - nprev:, :]
        else:
            u_scr[slot, half, :hdr, :] = prev
            uc = cb_ref[:, col:col + tn] + u_scr[slot, half, hdr:hdr + tm, :] * cw[conv_w - 1:conv_w, :]
            for jj in range(conv_w - 1):
                off = hdr - (conv_w - 1 - jj) * ts
                uc = uc + u_scr[slot, half, off:off + tm, :] * cw[jj:jj + 1, :]
            new_prev = u_scr[slot, half, tm:hdr + tm, :]
        carry[half, c] = new_prev
        fb_ref[0, 0, :, col:col + tn] = new_prev
        return uc

    def stage_gate(c, slot):
        val = conv(c, slot, 0)
        gte = conv(c, slot, 1)
        h_scr[slot] = (jax.nn.gelu(gte) * val).astype(BF16)

    def stage_down(c, slot):
        y = _dot(h_scr[slot], wd_ref[c * tn:(c + 1) * tn, :])
        if c == 0:
            acc_scr[...] = y
        else:
            acc_scr[...] += y

    stage_up(0, 0)
    stage_gate(0, 0)
    if nc > 1:
        stage_up(1, 1)
    for c in range(nc - 1):
        if c + 2 < nc:
            stage_up(c + 2, c % 2)
        stage_gate(c + 1, (c + 1) % 2)
        stage_down(c, c % 2)
    stage_down(nc - 1, (nc - 1) % 2)

    y = xin_scr[...] + acc_scr[...]
    if final_norm:
        y = _rms(y, gfin_ref[...])
    out_ref[0] = y


def ffn_layer(x, mixer, fb0, g, w_up, conv_w, conv_b, w_down, g_final, *, layer, tm, tn, ts,
              final_norm):
    bk, r, d = x.shape
    f = w_down.shape[1]
    cw = conv_w.shape[0]
    nprev = (cw - 1) * ts
    hdr = 0 if ts == 1 else nprev
    nc = f // tn
    const2 = lambda b, t: (0, 0)
    resident = pl.Buffered(1)
    kern = functools.partial(_ffn_kernel, tm=tm, tn=tn, ts=ts, conv_w=cw, final_norm=final_norm,
                             has_mixer=mixer is not None)
    mixer_specs, mixer_args = [], []
    if mixer is not None:
        attn, w_o = mixer
        mixer_specs = [pl.BlockSpec((1, tm, attn.shape[2]), lambda b, t: (b, t, 0)),
                       pl.BlockSpec(w_o.shape, const2, pipeline_mode=resident)]
        mixer_args = [attn, w_o]
    out, fb = pl.pallas_call(
        kern,
        grid=(bk, r // tm),
        in_specs=[pl.BlockSpec((1, tm, d), lambda b, t: (b, t, 0))] + mixer_specs + [
                  pl.BlockSpec((1, nprev, 2 * f), lambda b, t: (b, 0, 0)),
                  pl.BlockSpec((1, d), const2),
                  pl.BlockSpec((None, d, 2 * f), lambda b, t: (layer, 0, 0), pipeline_mode=resident),
                  pl.BlockSpec((cw, 2 * f), const2),
                  pl.BlockSpec((1, 2 * f), const2),
                  pl.BlockSpec((None, f, d), lambda b, t: (layer, 0, 0), pipeline_mode=resident),
                  pl.BlockSpec((1, d), const2)],
        out_specs=[pl.BlockSpec((1, tm, d), lambda b, t: (b, t, 0)),
                   pl.BlockSpec((1, 1, nprev, 2 * f), lambda b, t: (b, t, 0, 0))],
        out_shape=[jax.ShapeDtypeStruct((bk, r, d), F32),
                   jax.ShapeDtypeStruct((bk, r // tm, nprev, 2 * f), F32)],
        scratch_shapes=[pltpu.VMEM((tm, d), BF16),
                        pltpu.VMEM((tm, d), F32),
                        pltpu.VMEM((2, 2, hdr + tm, tn), F32),
                        pltpu.VMEM((2, tm, tn), BF16),
                        pltpu.VMEM((2, nc, nprev, tn), F32),
                        pltpu.VMEM((tm, d), F32)],
        name="ffn_layer",
        compiler_params=_cparams(("parallel", "arbitrary")),
    )(x, *mixer_args, fb0, g, w_up, conv_w, conv_b, w_down, g_final)
    return out, fb[:, -1]


def _shifted_bias(rel, rb_ref, h, scale=1.0):
    last = rb_ref[N_BUCKETS - 1, h]
    val = jnp.zeros(rel.shape, F32)
    for b in range(N_BUCKETS - 2, -1, -1):
        val = jnp.where(rel < BUCKET_STARTS[b + 1], (rb_ref[b, h] - last) * scale, val)
    return jnp.where(rel >= 0, val, NEG_INF)


def _near_offsets(tq, tk):
    step = math.gcd(tq, tk)
    lo = -(tq - 1)
    hi = tk - 1 + FAR_DISTANCE - 1
    first = -((-lo) // step) * step
    if first < lo:
        first += step
    count = (hi - first) // step + 1
    return first, step, count


def _prompt_bias_kernel(rb_ref, o_ref, *, tq, tk, first, step):
    h = pl.program_id(0)
    o = pl.program_id(1)
    rel = (lax.broadcasted_iota(jnp.int32, (tk, tq), 1)
           - lax.broadcasted_iota(jnp.int32, (tk, tq), 0) + (first + o * step))
    o_ref[0, 0] = _shifted_bias(rel, rb_ref, h, LOG2E)


def prompt_bias(rel_bias, tq, tk):
    n_heads = rel_bias.shape[1]
    first, step, count = _near_offsets(tq, tk)
    return pl.pallas_call(
        functools.partial(_prompt_bias_kernel, tq=tq, tk=tk, first=first, step=step),
        grid=(n_heads, count),
        in_specs=[pl.BlockSpec(memory_space=pltpu.SMEM)],
        out_specs=pl.BlockSpec((1, 1, tk, tq), lambda h, o: (h, o, 0, 0)),
        out_shape=jax.ShapeDtypeStruct((n_heads, count, tk, tq), F32),
        name="prompt_bias",
        compiler_params=_cparams(("parallel", "parallel")),
    )(rel_bias)


def _sample_bias_kernel(rb_ref, o_ref, *, tq, page, n_heads):
    rows = 2 * tq
    row = lax.broadcasted_iota(jnp.int32, (rows, 2 * page), 0)
    col = lax.broadcasted_iota(jnp.int32, (rows, 2 * page), 1)
    t = row % tq
    rel = jnp.where(col < page, page + t - col, t - (col - page))
    for h in range(n_heads):
        o_ref[h * rows:(h + 1) * rows, :] = _shifted_bias(rel, rb_ref, h)


def sample_bias(rel_bias, tq, page):
    n_heads = rel_bias.shape[1]
    return pl.pallas_call(
        functools.partial(_sample_bias_kernel, tq=tq, page=page, n_heads=n_heads),
        in_specs=[pl.BlockSpec(memory_space=pltpu.SMEM)],
        out_specs=pl.BlockSpec(memory_space=pltpu.VMEM),
        out_shape=jax.ShapeDtypeStruct((n_heads * 2 * tq, 2 * page), F32),
        name="sample_bias",
    )(rel_bias)


def _lambda(lq1_ref, lk1_ref, lq2_ref, lk2_ref, lam_init):
    s1 = jnp.sum(lq1_ref[...] * lk1_ref[...], axis=1, keepdims=True)
    s2 = jnp.sum(lq2_ref[...] * lk2_ref[...], axis=1, keepdims=True)
    return jnp.exp(s1) - jnp.exp(s2) + lam_init


def _block_schedule(tq, tk, nq):
    first, step, count = _near_offsets(tq, tk)
    hi = first + (count - 1) * step
    far, near = [], []
    for qi in range(nq):
        n_far = max(qi * tq - hi + tk - 1, 0) // tk
        k_end = (qi * tq + tq - 1) // tk + 1
        far += [(kj, qi, count) for kj in range(n_far)]
        near += [(kj, qi, (qi * tq - kj * tk - first) // step) for kj in range(n_far, k_end)]
    order = sorted(far) + sorted(near)
    order += [order[-1]] * 2
    return ([o[1] for o in order], [o[0] for o in order], [o[2] for o in order], len(far))


def _attn_kernel(sq_ref, sk_ref, sb_ref, q_ref, k_ref, vt_ref, bias_ref, lq1_ref, lk1_ref, lq2_ref,
                 lk2_ref, subln_ref, o_ref, qs_scr, m_scr, acc_scr, s_scr, p_scr, al_scr,
                 *, tq, tk, n_blocks, n_far, lam_init):
    hw = 2 * HEAD_DIM
    nq = qs_scr.shape[0]

    def prepare(qi, carry):
        q = q_ref[0, pl.ds(pl.multiple_of(qi * tq, tq), tq), :]
        lane = lax.broadcasted_iota(jnp.int32, q.shape, 1)
        qs_scr[qi, :tq, :] = jnp.where(lane < HEAD_DIM, q, jnp.zeros_like(q))
        qs_scr[qi, tq:, :] = jnp.where(lane >= HEAD_DIM, q, jnp.zeros_like(q))
        m_scr[qi] = jnp.full(m_scr.shape[1:], NEG_INF, F32)
        acc_scr[qi] = jnp.zeros(acc_scr.shape[1:], F32)
        return carry

    lax.fori_loop(0, nq, prepare, 0)

    def stage_scores(i, slot):
        start = pl.multiple_of(sk_ref[i] * tk, tk)
        k = k_ref[0, pl.ds(start, tk), :]
        s_scr[slot] = _dot_nt(k, qs_scr[sq_ref[i]])

    kc = 32

    def stage_probs(i, slot, far):
        qi = sq_ref[i]

        def chunk(c):
            s = s_scr[slot, c * kc:(c + 1) * kc, :]
            if not far:
                bias = bias_ref[0, sb_ref[i], c * kc:(c + 1) * kc, :]
                s = s + jnp.concatenate([bias, bias], axis=1)
            return s

        m_prev = m_scr[qi]
        m_rows = None
        for c in range(tk // kc):
            part = jnp.max(chunk(c).reshape(kc // V7X_SUBLANES, V7X_SUBLANES, 2 * tq), axis=0)
            m_rows = part if m_rows is None else jnp.maximum(m_rows, part)
        m_new = jnp.maximum(m_prev, jnp.max(m_rows, axis=0, keepdims=True))
        m_scr[qi] = m_new
        for c in range(tk // kc):
            p_scr[slot, c * kc:(c + 1) * kc, :] = jnp.exp2(chunk(c) - m_new).astype(BF16)
        al_scr[slot] = jnp.exp2(m_prev - m_new)

    def stage_values(i, slot):
        qi = sq_ref[i]
        start = pl.multiple_of(sk_ref[i] * tk, tk)
        vt = vt_ref[0, 0, :, pl.ds(start, tk)]
        row = lax.broadcasted_iota(jnp.int32, (acc_scr.shape[1] - hw, tk), 0)
        ones_rows = jnp.where(row == 0, 1.0, 0.0).astype(BF16)
        acc_scr[qi] = al_scr[slot] * acc_scr[qi] + _dot(jnp.concatenate([vt, ones_rows], axis=0), p_scr[slot])

    n_slots = s_scr.shape[0]
    assert n_slots == 4 and n_blocks >= n_slots

    def pipe_step(i, r, far, with_scores=True, with_probs=True):
        if with_probs:
            stage_probs(i + 2, (r + 2) % n_slots, far)
        if with_scores:
            stage_scores(i + 4, r)
        stage_values(i, r)

    def run_steps(lo, hi, far):
        while lo < hi and lo % n_slots:
            pipe_step(lo, lo % n_slots, far)
            lo += 1
        n_groups = max(hi - lo, 0) // n_slots

        def group(ii, carry):
            for r in range(n_slots):
                pipe_step(lo + n_slots * ii + r, r, far)
            return carry

        if n_groups > 0:
            lax.fori_loop(0, n_groups, group, 0)
        for i in range(lo + n_slots * n_groups, hi):
            pipe_step(i, i % n_slots, far)

    for i in range(4):
        stage_scores(i, i)
    for i in range(2):
        stage_probs(i, i, i < n_far)
    far_steps = max(n_far - 2, 0)
    run_steps(0, far_steps, True)
    run_steps(far_steps, n_blocks - 4, False)
    for i in range(n_blocks - 4, n_blocks):
        pipe_step(i, i % n_slots, False, with_scores=False, with_probs=i + 2 < n_blocks)

    lam = _lambda(lq1_ref, lk1_ref, lq2_ref, lk2_ref, lam_init)

    def finish(qi, carry):
        acc = acc_scr[qi]
        on = acc[:hw, :] / acc[hw:hw + 1, :]
        ot = on[:, :tq] - lam * on[:, tq:]
        ot = ot * lax.rsqrt(jnp.mean(ot * ot, axis=0, keepdims=True) + EPS)
        ot = ot * subln_ref[...] * (1.0 - lam_init)
        o_ref[0, pl.ds(pl.multiple_of(qi * tq, tq), tq), :] = ot.T.astype(o_ref.dtype)
        return carry

    lax.fori_loop(0, nq, finish, 0)


def prompt_attention(q, k, vt, bias, lq1, lk1, lq2, lk2, subln, *, tq, tk, lam_init):
    b, t, dk = q.shape
    hw = 2 * HEAD_DIM
    n_heads = dk // hw
    vrows = hw + 16
    nq = t // tq
    sched_q, sched_k, sched_b, n_far = _block_schedule(tq, tk, nq)
    n_blocks = len(sched_q) - 2
    vec = lambda shape: pl.BlockSpec(shape, lambda bb, h, *_: (0, 0))
    grid_spec = pltpu.PrefetchScalarGridSpec(
        num_scalar_prefetch=3,
        grid=(b, n_heads),
        in_specs=[pl.BlockSpec((1, t, hw), lambda bb, h, *_: (bb, 0, h), pipeline_mode=pl.Buffered(1)),
                  pl.BlockSpec((1, t, hw), lambda bb, h, *_: (bb, 0, h)),
                  pl.BlockSpec((1, 1, hw, t), lambda bb, h, *_: (bb, h, 0, 0)),
                  pl.BlockSpec((1,) + bias.shape[1:], lambda bb, h, *_: (h, 0, 0, 0),
                               pipeline_mode=pl.Buffered(1)),
                  vec((1, HEAD_DIM)), vec((1, HEAD_DIM)), vec((1, HEAD_DIM)), vec((1, HEAD_DIM)),
                  vec((hw, 1))],
        out_specs=pl.BlockSpec((1, t, hw), lambda bb, h, *_: (bb, 0, h)),
        scratch_shapes=[pltpu.VMEM((nq, 2 * tq, hw), BF16),
                        pltpu.VMEM((nq, 1, 2 * tq), F32),
                        pltpu.VMEM((nq, vrows, 2 * tq), F32),
                        pltpu.VMEM((4, tk, 2 * tq), F32),
                        pltpu.VMEM((4, tk, 2 * tq), BF16),
                        pltpu.VMEM((4, 1, 2 * tq), F32)])
    return pl.pallas_call(
        functools.partial(_attn_kernel, tq=tq, tk=tk, n_blocks=n_blocks, n_far=n_far, lam_init=lam_init),
        grid_spec=grid_spec,
        out_shape=jax.ShapeDtypeStruct((b, t, dk), BF16),
        name="prompt_attention",
        compiler_params=_cparams(("parallel", "parallel")),
    )(jnp.asarray(sched_q, jnp.int32), jnp.asarray(sched_k, jnp.int32), jnp.asarray(sched_b, jnp.int32),
      q, k, vt, bias, lq1, lk1, lq2, lk2, subln)


def _sample_attn_kernel(pt_ref, *refs, n_pages, page, tq, n_heads, lam_init):
    k_refs = refs[:n_pages]
    v_refs = refs[n_pages:2 * n_pages]
    (q_ref, kn_ref, vn_ref, bias_ref, lq1_ref, lk1_ref, lq2_ref, lk2_ref, subln_ref,
     o_ref, kt_scr, kn_scr, vbf) = refs[2 * n_pages:]
    past = n_pages * page
    hw = 2 * HEAD_DIM
    rows = n_heads * 2 * tq
    dk = n_heads * hw
    nn = kn_ref.shape[1]

    @pl.when(pl.program_id(0) == 0)
    def _():
        kn_scr[...] = jnp.zeros_like(kn_scr)
        for h in range(n_heads):
            vbf[h, past:, :] = jnp.zeros((page, hw), BF16)

    for p in range(n_pages):
        kt_scr[:, p * page:(p + 1) * page] = k_refs[p][0].astype(BF16)
        for h in range(n_heads):
            vbf[h, p * page:(p + 1) * page, :] = (
                v_refs[p][0, pl.ds(h, page, stride=n_heads), :].astype(BF16))
    kn_scr[:nn, :] = kn_ref[0]
    vn = vn_ref[0]
    for h in range(n_heads):
        vbf[h, past:past + nn, :] = vn[:, h * hw:(h + 1) * hw]

    qrep = jnp.concatenate([q_ref[0]] * n_heads, axis=0)
    rr = lax.broadcasted_iota(jnp.int32, (rows, dk), 0)
    cc = lax.broadcasted_iota(jnp.int32, (rows, dk), 1)
    qbd = jnp.where(cc // HEAD_DIM == rr // tq, qrep, 0.0).astype(BF16)

    s_main = _dot(qbd, kt_scr[:, :past - page])
    s_last = _dot(qbd, kt_scr[:, past - page:]) + bias_ref[:, :page]
    s_new = _dot_nt(qbd, kn_scr[...]) + bias_ref[:, page:]
    m = jnp.maximum(jnp.max(s_main, axis=1, keepdims=True),
                    jnp.max(jnp.maximum(s_last, s_new), axis=1, keepdims=True))
    p_all = jnp.concatenate([jnp.exp(s_main - m), jnp.exp(s_last - m), jnp.exp(s_new - m)], axis=1)
    l = jnp.sum(p_all, axis=1, keepdims=True)
    p_bf = p_all.astype(BF16)
    lam = _lambda(lq1_ref, lk1_ref, lq2_ref, lk2_ref, lam_init)
    outs = []
    for h in range(n_heads):
        r0 = h * 2 * tq
        o8 = _dot(p_bf[r0:r0 + 2 * tq], vbf[h]) / l[r0:r0 + 2 * tq]
        o = o8[:tq] - lam * o8[tq:]
        outs.append(_rms(o, subln_ref[...]) * (1.0 - lam_init))
    o_ref[0] = jnp.concatenate(outs, axis=1)


def sample_attention(page_table, cache_kt, cache_v, q8, k_new, v_new, bias, lq1, lk1, lq2, lk2, subln,
                     *, tq, lam_init):
    n_seq, n_pages = page_table.shape
    _, dk, page = cache_kt.shape
    hw = 2 * HEAD_DIM
    n_heads = dk // hw
    nn = k_new.shape[1]
    past = n_pages * page

    def page_spec(p, shape):
        return pl.BlockSpec((1,) + shape, lambda b, pt: (pt[b * n_pages + p], 0, 0))

    vec = lambda shape: pl.BlockSpec(shape, lambda b, pt: (0, 0))
    in_specs = ([page_spec(p, (dk, page)) for p in range(n_pages)]
                + [page_spec(p, (page * n_heads, hw)) for p in range(n_pages)]
                + [pl.BlockSpec((1, 2 * tq, dk), lambda b, pt: (b, 0, 0)),
                   pl.BlockSpec((1, nn, dk), lambda b, pt: (b, 0, 0)),
                   pl.BlockSpec((1, nn, dk), lambda b, pt: (b, 0, 0)),
                   vec(bias.shape),
                   vec((1, HEAD_DIM)), vec((1, HEAD_DIM)), vec((1, HEAD_DIM)), vec((1, HEAD_DIM)),
                   vec((1, hw))])
    grid_spec = pltpu.PrefetchScalarGridSpec(
        num_scalar_prefetch=1,
        grid=(n_seq,),
        in_specs=in_specs,
        out_specs=pl.BlockSpec((1, tq, dk), lambda b, pt: (b, 0, 0)),
        scratch_shapes=[pltpu.VMEM((dk, past), BF16),
                        pltpu.VMEM((page, dk), BF16),
                        pltpu.VMEM((n_heads, past + page, hw), BF16)])
    kern = functools.partial(_sample_attn_kernel, n_pages=n_pages, page=page, tq=tq,
                             n_heads=n_heads, lam_init=lam_init)
    return pl.pallas_call(
        kern,
        grid_spec=grid_spec,
        out_shape=jax.ShapeDtypeStruct((n_seq, tq, dk), F32),
        name="sample_attention",
        compiler_params=_cparams(("arbitrary",)),
    )(page_table.reshape(-1), *([cache_kt] * n_pages), *([cache_v] * n_pages),
      q8, k_new, v_new, bias, lq1, lk1, lq2, lk2, subln)


def _pick_tile(n, target):
    t = min(n, target)
    while n % t:
        t //= 2
    return t


def _run_trunk(x, pos0, ts, rg_h0, rg_cb0, ffn_fb0, w, attn_fn):
    bk, r, d = x.shape
    depth = w['norm_mix'].shape[0]
    n_a = len(w['rg_w_in'])
    tm_rg = _pick_tile(r, 256)
    tm_ffn = _pick_tile(r, 512)
    tm_mm = _pick_tile(bk * r, 512)
    dk = w['attn_w_q'][0].shape[1]
    new_h, new_cb, new_fb = [], [], []
    k_new = v_new = kb = vb = None
    for l in range(depth):
        mixer = None
        if l < n_a:
            x, h_last, cb = rglru_layer(
                x, rg_h0[l], rg_cb0[l], w['norm_mix'][l][None], w['rg_w_in'][l], w['rg_conv_w'][l],
                w['rg_conv_b'][l][None], w['rg_w_ax'][l], w['rg_b_a'][l][None], w['rg_b_x'][l][None],
                w['rg_lambda'][l][None], w['rg_w_out'][l], tm=tm_rg, ts=ts, pos0=pos0)
            new_h.append(h_last)
            new_cb.append(cb)
        else:
            x2 = x.reshape(bk * r, d)
            j = l - n_a
            groups = [(w['norm_mix'][l][None], w['attn_w_q'][j], [(0, dk, attn_fn.q_scale, False)],
                       [attn_fn.q_dtype])]
            if l == n_a:
                groups.append((w['kv_norm'][None], w['w_kv'],
                               [(0, dk, 1.0, True), (dk, 2 * dk, 1.0, False), (0, dk, 1.0, False),
                                (dk, 2 * dk, 1.0, False)], [F32, F32, BF16, BF16]))
            outs = norm_matmul(x2, groups, tm_mm, rows_per_seq=r)
            q = outs[0]
            if l == n_a:
                k_new, v_new, kb, vb = outs[1:]
            mixer = (attn_fn(j, l, q, kb, vb).reshape(bk, r, dk), w['attn_w_o'][j])
        last = l == depth - 1
        x, fb = ffn_layer(x, mixer, ffn_fb0[l], w['norm_ffn'][l][None], w['ffn_w_up'], w['ffn_conv_w'][l],
                          w['ffn_conv_b'][l][None], w['ffn_w_down'], w['final_norm'][None],
                          layer=l, tm=tm_ffn, tn=512, ts=ts, final_norm=last)
        new_fb.append(fb)
    return x, new_h, new_cb, new_fb, k_new, v_new


def _lam_init(layer_idx):
    return 0.8 - 0.6 * math.exp(-0.3 * layer_idx)


def kernel(x_prompt, x_sample, state_rglru_h, state_rglru_conv, state_ffn_conv, cache_k, cache_v,
           page_table, rel_bias, norm_mix, norm_ffn, final_norm, rg_w_in, rg_conv_w, rg_conv_b,
           rg_w_a, rg_b_a, rg_w_x, rg_b_x, rg_lambda, rg_w_out, kv_norm, w_kv, attn_w_q,
           lam_q1, lam_k1, lam_q2, lam_k2, attn_subln, attn_w_o, ffn_w_up, ffn_conv_w,
           ffn_conv_b, ffn_w_down):
    bp, t_p, d = x_prompt.shape
    n_seq, tq, _ = x_sample.shape
    depth = norm_mix.shape[0]
    n_a = rg_w_in.shape[0]
    n_pool, page, n_heads, _, hd = cache_k.shape
    assert hd == HEAD_DIM
    dk = n_heads * 2 * hd
    f2 = ffn_w_up.shape[2]
    past_len = page_table.shape[1] * page
    assert page >= FAR_DISTANCE and past_len >= 2 * page and 2 * tq == V7X_SUBLANES

    per_layer = lambda a: [a[l].astype(BF16) for l in range(a.shape[0])]
    w = {
        'norm_mix': norm_mix, 'norm_ffn': norm_ffn, 'final_norm': final_norm,
        'rg_w_in': per_layer(rg_w_in), 'rg_conv_w': rg_conv_w, 'rg_conv_b': rg_conv_b,
        'rg_w_ax': [jnp.concatenate([a, b], axis=-1).astype(BF16) for a, b in zip(rg_w_a, rg_w_x)],
        'rg_b_a': rg_b_a, 'rg_b_x': rg_b_x, 'rg_lambda': rg_lambda,
        'rg_w_out': per_layer(rg_w_out), 'kv_norm': kv_norm, 'w_kv': w_kv.astype(BF16),
        'attn_w_q': per_layer(attn_w_q), 'attn_w_o': per_layer(attn_w_o),
        'ffn_w_up': ffn_w_up.astype(BF16), 'ffn_conv_w': ffn_conv_w, 'ffn_conv_b': ffn_conv_b,
        'ffn_w_down': ffn_w_down.astype(BF16),
    }
    lam_vecs = lambda j: (lam_q1[j][None], lam_k1[j][None], lam_q2[j][None], lam_k2[j][None])

    tq_p = _pick_tile(t_p, 512)
    tk_p = _pick_tile(t_p, 512)
    bias_p = prompt_bias(rel_bias, tq_p, tk_p)

    vt_cache = {}

    def attn_prompt(j, l, q, kb, vb):
        if 'vt' not in vt_cache:
            vt_cache['vt'] = vb.reshape(bp, t_p, n_heads, 2 * hd).transpose(0, 2, 3, 1)
        o = prompt_attention(q.reshape(bp, t_p, dk), kb.reshape(bp, t_p, dk), vt_cache['vt'],
                             bias_p, *lam_vecs(j), attn_subln[j][:, None], tq=tq_p, tk=tk_p,
                             lam_init=_lam_init(l))
        return o.reshape(bp * t_p, dk)
    attn_prompt.q_dtype = BF16
    attn_prompt.q_scale = ATTN_SCALE * LOG2E

    cw_rg = rg_conv_w.shape[1]
    cw_ffn = ffn_conv_w.shape[1]
    yp, hp, cbp, fbp, kp, vp = _run_trunk(
        x_prompt, 0, 1,
        jnp.zeros((n_a, bp, 1, d), F32), jnp.zeros((n_a, bp, cw_rg - 1, d), F32),
        jnp.zeros((depth, bp, cw_ffn - 1, f2), F32), w, attn_prompt)

    bias_s = sample_bias(rel_bias, tq, page)
    ck = cache_k.transpose(0, 2, 3, 4, 1).reshape(n_pool, dk, page)
    cv = cache_v.reshape(n_pool, page * n_heads, 2 * hd)
    n_new = 16

    def to_seq_major(a2):
        return a2.reshape(tq, n_seq, -1).transpose(1, 0, 2)

    def attn_sample(j, l, q, kb, vb):
        q4 = to_seq_major(q)
        q8 = jnp.concatenate([q4, q4], axis=1)
        pad = ((0, 0), (0, n_new - tq), (0, 0))
        o = sample_attention(page_table, ck, cv, q8, jnp.pad(to_seq_major(kb), pad),
                             jnp.pad(to_seq_major(vb), pad), bias_s, *lam_vecs(j), attn_subln[j][None],
                             tq=tq, lam_init=_lam_init(l))
        return o.transpose(1, 0, 2).reshape(tq * n_seq, dk)
    attn_sample.q_dtype = F32
    attn_sample.q_scale = ATTN_SCALE

    def tmajor(a):
        return a.transpose(0, 2, 1, 3).reshape(a.shape[0], 1, a.shape[2] * n_seq, a.shape[3])

    def smajor(a2, steps):
        return a2.reshape(steps, n_seq, -1).transpose(1, 0, 2)

    xs = x_sample.transpose(1, 0, 2).reshape(1, tq * n_seq, d)
    ys, hs, cbs, fbs, ks, vs = _run_trunk(
        xs, past_len, n_seq, state_rglru_h[:, None], tmajor(state_rglru_conv), tmajor(state_ffn_conv),
        w, attn_sample)

    return (yp,
            smajor(ys, tq),
            jnp.stack([h[:, 0] for h in hp]),
            jnp.stack(cbp),
            jnp.stack(fbp),
            kp.reshape(bp, n_heads, 2, hd, t_p).transpose(0, 4, 1, 2, 3),
            vp.reshape(bp, t_p, n_heads, 2 * hd),
            jnp.stack([h[0] for h in hs]),
            jnp.stack([smajor(c, cw_rg - 1) for c in cbs]),
            jnp.stack([smajor(fb, cw_ffn - 1) for fb in fbs]),
            ks.reshape(n_heads, 2, hd, tq, n_seq).transpose(4, 3, 0, 1, 2),
            smajor(vs[None], tq).reshape(n_seq, tq, n_heads, 2 * hd))
```
